```python
import jax, jax.numpy as jnp
from jax import lax
import numpy as np

D_MODEL = 2048
BATCH = 4
SEQ = 4096
DEPTH = 2

CTX_LEN = 256
GRID_W = 64
EPS = 1e-6
ROPE_BASE = 10000.0
N_BRANCH = 3

GDN_HEADS = 8
GDN_DK = 128
GDN_DV = 128
GDN_CONV = 5
GDN_CHUNK = 64
GDN_QKV = 2 * GDN_HEADS * GDN_DK + GDN_HEADS * GDN_DV

MLA_HEADS = 8
MLA_Q_RANK = 768
MLA_KV_RANK = 512
MLA_NOPE = 128
MLA_ROPE = 64
MLA_V = 128
MLA_QK = MLA_NOPE + MLA_ROPE
MLA_SCALE = MLA_QK ** -0.5
ATTN_QBLOCK = 128

SWA_HEADS = 8
SWA_KV_HEADS = 2
SWA_HD = 128
SWA_WINDOW = 128
SWA_BLOCK = 128
SWA_SCALE = SWA_HD ** -0.5

FFN_DENSE = 5632
N_EXPERTS = 8
TOP_K = 2
FFN_EXPERT = 7168
MOE_BLOCK = 512

IN_SIZES = (
    GDN_HEADS * GDN_DK, GDN_HEADS * GDN_DK, GDN_HEADS * GDN_DV, GDN_HEADS * GDN_DV,
    2 * GDN_HEADS, 2 * GDN_HEADS,
    MLA_Q_RANK, MLA_KV_RANK, MLA_ROPE,
    SWA_HEADS * SWA_HD, SWA_KV_HEADS * SWA_HD, SWA_KV_HEADS * SWA_HD,
    N_BRANCH * D_MODEL,
)
D_IN = sum(IN_SIZES)

kernel_name = "hybrid_flow_dit_block"


def rmsnorm(x, g):
    xf = x.astype(jnp.float32)
    y = xf * lax.rsqrt(jnp.mean(xf * xf, axis=-1, keepdims=True) + EPS)
    return (y * g.astype(jnp.float32)).astype(x.dtype)


def l2norm(x):
    xf = x.astype(jnp.float32)
    return xf * lax.rsqrt(jnp.sum(xf * xf, axis=-1, keepdims=True) + EPS)


def modulate(x, g, shift, scale):
    return rmsnorm(x, g) * (1 + scale) + shift


def split_in(z):
    return jnp.split(z, np.cumsum(IN_SIZES)[:-1].tolist(), axis=-1)


def axial_rope_angles(T, rot_dim):
    rows = T // GRID_W
    row = jnp.broadcast_to(jnp.arange(rows)[:, None], (rows, GRID_W)).reshape(T).astype(jnp.float32)
    col = jnp.broadcast_to(jnp.arange(GRID_W)[None, :], (rows, GRID_W)).reshape(T).astype(jnp.float32)
    n_freq = rot_dim // 4
    inv = jnp.power(ROPE_BASE, -jnp.arange(n_freq, dtype=jnp.float32) / n_freq)
    return (row[:, None] * inv)[:, None, :], (col[:, None] * inv)[:, None, :]


def _rotate(x, ang):
    x1, x2 = jnp.split(x, 2, axis=-1)
    cos = jnp.cos(ang).astype(x.dtype)
    sin = jnp.sin(ang).astype(x.dtype)
    return jnp.concatenate([x1 * cos - x2 * sin, x2 * cos + x1 * sin], axis=-1)


def axial_rope(x, ang_row, ang_col):
    xr, xc = jnp.split(x, 2, axis=-1)
    return jnp.concatenate([_rotate(xr, ang_row), _rotate(xc, ang_col)], axis=-1)


def short_conv(x, w):
    pad = (w.shape[0] - 1) // 2
    return lax.conv_general_dilated(x, w.astype(x.dtype), window_strides=(1,), padding=[(pad, pad)],
                                    dimension_numbers=('NWC', 'WIO', 'NWC'), feature_group_count=x.shape[-1])


def gdn_prepare(q, k, v, b, a, conv_w, a_log, dt_bias):
    B, T, _ = q.shape
    qkv = jax.nn.silu(short_conv(jnp.concatenate([q, k, v], axis=-1), conv_w))
    nk = GDN_HEADS * GDN_DK
    q = l2norm(qkv[..., :nk].reshape(B, T, GDN_HEADS, GDN_DK)) * (GDN_DK ** -0.5)
    k = l2norm(qkv[..., nk:2 * nk].reshape(B, T, GDN_HEADS, GDN_DK))
    v = qkv[..., 2 * nk:].reshape(B, T, GDN_HEADS, GDN_DV).astype(jnp.float32)
    beta = jax.nn.sigmoid(b.astype(jnp.float32))
    g = -jnp.exp(a_log.astype(jnp.float32)) * jax.nn.softplus(a.astype(jnp.float32) + dt_bias.astype(jnp.float32))
    return q, k, v, beta, g


def gdn_chunked(q, k, v, beta, g, state):
    B, T, H, dk = q.shape
    dv = v.shape[-1]
    C = GDN_CHUNK
    n = T // C

    def chunks(t):
        t = t.reshape((B, n, C, H) + t.shape[3:])
        return jnp.moveaxis(t, (1, 3), (0, 2))

    qc, kc, vc, bc, gc = (chunks(t) for t in (q, k, v, beta, g))
    decay = jnp.cumsum(gc, axis=-1)
    causal = jnp.tril(jnp.ones((C, C), bool))
    strict = jnp.tril(jnp.ones((C, C), bool), -1)
    seg = jnp.exp(jnp.where(causal, decay[..., :, None] - decay[..., None, :], -jnp.inf))
    kb = kc * bc[..., None]
    m = jnp.where(strict, jnp.einsum('nbhid,nbhjd->nbhij', kb, kc) * seg, 0.0)
    a_mat = m + jnp.eye(C, dtype=m.dtype)
    rhs = jnp.concatenate([kb * jnp.exp(decay)[..., None], vc * bc[..., None]], axis=-1)
    sol = lax.linalg.triangular_solve(a_mat, rhs, left_side=True, lower=True, unit_diagonal=True)
    w, u = sol[..., :dk], sol[..., dk:]
    qk = jnp.einsum('nbhid,nbhjd->nbhij', qc, kc) * seg
    q_dec = qc * jnp.exp(decay)[..., None]
    k_dec = kc * jnp.exp(decay[..., -1:] - decay)[..., None]
    tot = jnp.exp(decay[..., -1])

    def step(s, inp):
        w_i, u_i, qk_i, qd_i, kd_i, t_i = inp
        v_new = u_i - jnp.einsum('bhcd,bhde->bhce', w_i, s)
        o = jnp.einsum('bhcd,bhde->bhce', qd_i, s) + jnp.einsum('bhij,bhje->bhie', qk_i, v_new)
        s = s * t_i[..., None, None] + jnp.einsum('bhcd,bhce->bhde', kd_i, v_new)
        return s, o

    state, o = lax.scan(step, state, (w, u, qk, q_dec, k_dec, tot))
    return jnp.moveaxis(o, (0, 2), (1, 3)).reshape(B, T, H, dv), state


def gdn_bidirectional(lat, ctx):
    q, k, v, beta, g = lat
    qc, kc, vc, betac, gc = ctx
    H = GDN_HEADS
    B = q.shape[0]
    s0 = jnp.zeros((B, H, GDN_DK, GDN_DV), jnp.float32)
    fl = lambda t: jnp.flip(t, axis=1)
    oc_f, sc_f = gdn_chunked(qc, kc, vc, betac[..., :H], gc[..., :H], s0)
    o_f, _ = gdn_chunked(q, k, v, beta[..., :H], g[..., :H], sc_f)
    oc_b, sc_b = gdn_chunked(fl(qc), fl(kc), fl(vc), fl(betac[..., H:]), fl(gc[..., H:]), s0)
    o_b, _ = gdn_chunked(fl(q), fl(k), fl(v), fl(beta[..., H:]), fl(g[..., H:]), sc_b)
    return o_f + fl(o_b), oc_f + fl(oc_b)


def gdn_output(o, og, norm_g):
    B, T, H, dv = o.shape
    o = rmsnorm(o, norm_g) * jax.nn.silu(og.astype(jnp.float32)).reshape(B, T, H, dv)
    return o.reshape(B, T, H * dv).astype(og.dtype)


def mla_q(cq, q_norm_g, w_uq, qn_g, ang):
    B, T, _ = cq.shape
    q = rmsnorm((rmsnorm(cq, q_norm_g) @ w_uq).reshape(B, T, MLA_HEADS, MLA_QK), qn_g)
    if ang is None:
        return q
    return jnp.concatenate([q[..., :MLA_NOPE], axial_rope(q[..., MLA_NOPE:], *ang)], axis=-1)


def mla_kv(ckv, kr, kv_norm_g, w_ukv, kn_g, ang):
    B, T, _ = ckv.shape
    kv = (rmsnorm(ckv, kv_norm_g) @ w_ukv).reshape(B, T, MLA_HEADS, MLA_NOPE + MLA_V)
    k_rope = jnp.broadcast_to(kr[:, :, None, :], (B, T, MLA_HEADS, MLA_ROPE))
    k = rmsnorm(jnp.concatenate([kv[..., :MLA_NOPE], k_rope], axis=-1), kn_g)
    if ang is not None:
        k = jnp.concatenate([k[..., :MLA_NOPE], axial_rope(k[..., MLA_NOPE:], *ang)], axis=-1)
    return k, kv[..., MLA_NOPE:]


def block_softmax_attn(q, k, v, scale):
    B, T, H, d = q.shape
    nb = T // ATTN_QBLOCK
    qb = q.reshape(B, nb, ATTN_QBLOCK, H, d).swapaxes(0, 1)

    def attend(qi):
        s = jnp.einsum('bqhd,bkhd->bhqk', qi, k).astype(jnp.float32) * scale
        p = jax.nn.softmax(s, axis=-1).astype(v.dtype)
        return jnp.einsum('bhqk,bkhe->bqhe', p, v)

    o = lax.map(attend, qb)
    return o.swapaxes(0, 1).reshape(B, T, H, v.shape[-1])


def swa_q(q, qn_g, ang):
    B, T, _ = q.shape
    q = rmsnorm(q.reshape(B, T, SWA_HEADS, SWA_HD), qn_g)
    return q if ang is None else axial_rope(q, *ang)


def swa_kv(k, v, kn_g, ang):
    B, T, _ = k.shape
    k = rmsnorm(k.reshape(B, T, SWA_KV_HEADS, SWA_HD), kn_g)
    if ang is not None:
        k = axial_rope(k, *ang)
    return k, v.reshape(B, T, SWA_KV_HEADS, SWA_HD)


def window_sink_attn(q, k, v, kc, vc, sink, scale):
    B, T, Hq, d = q.shape
    G = k.shape[2]
    R = Hq // G
    Lc = kc.shape[1]
    Bk = SWA_BLOCK
    nside = SWA_WINDOW // SWA_BLOCK
    nb = T // Bk
    span = (2 * nside + 1) * Bk
    qb = q.reshape(B, nb, Bk, G, R, d)
    padw = ((0, 0), (SWA_WINDOW, SWA_WINDOW), (0, 0), (0, 0))
    kp = jnp.pad(k, padw).reshape(B, nb + 2 * nside, Bk, G, d)
    vp = jnp.pad(v, padw).reshape(B, nb + 2 * nside, Bk, G, d)
    kb = jnp.concatenate([kp[:, j:j + nb] for j in range(2 * nside + 1)], axis=2)
    vb = jnp.concatenate([vp[:, j:j + nb] for j in range(2 * nside + 1)], axis=2)
    qpos = jnp.arange(nb)[:, None, None] * Bk + jnp.arange(Bk)[None, :, None]
    kpos = jnp.arange(nb)[:, None, None] * Bk - SWA_WINDOW + jnp.arange(span)[None, None, :]
    valid = (jnp.abs(qpos - kpos) <= SWA_WINDOW) & (kpos >= 0) & (kpos < T)
    s_loc = jnp.einsum('bnqgrd,bnkgd->bngrqk', qb, kb).astype(jnp.float32) * scale
    s_loc = jnp.where(valid[None, :, None, None], s_loc, -jnp.inf)
    s_ctx = jnp.einsum('bnqgrd,bcgd->bngrqc', qb, kc).astype(jnp.float32) * scale
    s_sink = jnp.broadcast_to(sink.reshape(G, R, 1, 1).astype(jnp.float32), s_loc.shape[:-1] + (1,))
    p = jax.nn.softmax(jnp.concatenate([s_loc, s_ctx, s_sink], axis=-1), axis=-1).astype(v.dtype)
    o = (jnp.einsum('bngrqk,bnkgd->bnqgrd', p[..., :span], vb)
         + jnp.einsum('bngrqc,bcgd->bnqgrd', p[..., span:span + Lc], vc))
    return o.reshape(B, T, Hq, d)


def ctx_sink_attn(qc, kc, vc, sink, scale):
    B, L, Hq, d = qc.shape
    G = kc.shape[2]
    R = Hq // G
    q = qc.reshape(B, L, G, R, d)
    s = jnp.einsum('bqgrd,bcgd->bgrqc', q, kc).astype(jnp.float32) * scale
    s_sink = jnp.broadcast_to(sink.reshape(G, R, 1, 1).astype(jnp.float32), s.shape[:-1] + (1,))
    p = jax.nn.softmax(jnp.concatenate([s, s_sink], axis=-1), axis=-1)[..., :L].astype(vc.dtype)
    return jnp.einsum('bgrqc,bcgd->bqgrd', p, vc).reshape(B, L, Hq, d)


def merge_branches(gates, o_a, o_b, o_c, w_a, w_b, w_c, w_out):
    g_a, g_b, g_c = jnp.split(jax.nn.sigmoid(gates), N_BRANCH, axis=-1)
    return (g_a * (o_a @ w_a) + g_b * (o_b @ w_b) + g_c * (o_c @ w_c)) @ w_out


def hybrid_mixer(h, hc, w_in, conv_w, a_log, dt_bias, gdn_norm_g,
                 mla_q_norm_g, mla_kv_norm_g, mla_w_uq, mla_w_ukv, mla_qn_g, mla_kn_g,
                 swa_qn_g, swa_kn_g, swa_sink, w_a, w_b, w_c, w_out, need_ctx):
    B, T, _ = h.shape
    Lc = hc.shape[1]
    ang_mla = axial_rope_angles(T, MLA_ROPE)
    ang_swa = axial_rope_angles(T, SWA_HD)
    (gq, gk, gv, gog, gb, ga, mcq, mckv, mkr, sq, sk, sv, gates) = split_in(h @ w_in)
    (cgq, cgk, cgv, cgog, cgb, cga, cmcq, cmckv, cmkr, csq, csk, csv, cgates) = split_in(hc @ w_in)

    lat = gdn_prepare(gq, gk, gv, gb, ga, conv_w, a_log, dt_bias)
    ctxp = gdn_prepare(cgq, cgk, cgv, cgb, cga, conv_w, a_log, dt_bias)
    o_a, oc_a = gdn_bidirectional(lat, ctxp)

    mk, mv = mla_kv(mckv, mkr, mla_kv_norm_g, mla_w_ukv, mla_kn_g, ang_mla)
    mkc, mvc = mla_kv(cmckv, cmkr, mla_kv_norm_g, mla_w_ukv, mla_kn_g, None)
    mq = mla_q(mcq, mla_q_norm_g, mla_w_uq, mla_qn_g, ang_mla)
    o_b = block_softmax_attn(mq, jnp.concatenate([mk, mkc], axis=1), jnp.concatenate([mv, mvc], axis=1), MLA_SCALE)

    k_s, v_s = swa_kv(sk, sv, swa_kn_g, ang_swa)
    k_sc, v_sc = swa_kv(csk, csv, swa_kn_g, None)
    o_c = window_sink_attn(swa_q(sq, swa_qn_g, ang_swa), k_s, v_s, k_sc, v_sc, swa_sink, SWA_SCALE)

    y = merge_branches(gates, gdn_output(o_a, gog, gdn_norm_g), o_b.reshape(B, T, MLA_HEADS * MLA_V),
                       o_c.reshape(B, T, SWA_HEADS * SWA_HD), w_a, w_b, w_c, w_out)
    yc = None
    if need_ctx:
        oc_b = block_softmax_attn(mla_q(cmcq, mla_q_norm_g, mla_w_uq, mla_qn_g, None), mkc, mvc, MLA_SCALE)
        oc_c = ctx_sink_attn(swa_q(csq, swa_qn_g, None), k_sc, v_sc, swa_sink, SWA_SCALE)
        yc = merge_branches(cgates, gdn_output(oc_a, cgog, gdn_norm_g), oc_b.reshape(B, Lc, MLA_HEADS * MLA_V),
                            oc_c.reshape(B, Lc, SWA_HEADS * SWA_HD), w_a, w_b, w_c, w_out)
    return y, yc


def swiglu(h, wg, wu, wd):
    return (jax.nn.silu(h @ wg) * (h @ wu)) @ wd


def moe_swiglu(h, router_w, router_b, w_gate, w_up, w_down):
    N, D = h.shape
    E = w_gate.shape[0]
    logits = (h @ router_w).astype(jnp.float32) + router_b.astype(jnp.float32)
    top_logit, top_idx = lax.top_k(logits, TOP_K)
    top_w = jax.nn.softmax(top_logit, axis=-1)
    flat_e = top_idx.reshape(-1)
    order = jnp.argsort(flat_e)
    e_sorted = flat_e[order]
    tok_sorted = order // TOP_K
    w_sorted = top_w.reshape(-1)[order]
    counts = jnp.bincount(flat_e, length=E)
    padded = (counts + MOE_BLOCK - 1) // MOE_BLOCK * MOE_BLOCK
    start = jnp.cumsum(counts) - counts
    ends = jnp.cumsum(padded)
    pstart = ends - padded
    dest = pstart[e_sorted] + (jnp.arange(N * TOP_K) - start[e_sorted])
    n_blocks = -(-(N * TOP_K) // MOE_BLOCK) + E
    slots = n_blocks * MOE_BLOCK
    xs = jnp.zeros((slots, D), h.dtype).at[dest].set(h[tok_sorted])
    block_e = jnp.minimum(jnp.searchsorted(ends, jnp.arange(n_blocks) * MOE_BLOCK, side='right'), E - 1)

    def expert_block(args):
        xb, e = args
        return (jax.nn.silu(xb @ w_gate[e]) * (xb @ w_up[e])) @ w_down[e]

    ys = lax.map(expert_block, (xs.reshape(n_blocks, MOE_BLOCK, D), block_e)).reshape(slots, D)
    return jnp.zeros((N, D), h.dtype).at[tok_sorted].add(ys[dest] * w_sorted[:, None].astype(h.dtype))


def setup_inputs(seed: int = 0) -> dict:
    key = jax.random.key(seed)
    k = jax.random.split(key, 36)
    f32 = jnp.float32
    D = D_MODEL
    L = DEPTH
    Le = (DEPTH + 1) // 2
    Lo = DEPTH // 2

    def nrm(kk, shape, scale):
        return jax.random.normal(kk, shape, f32) * scale

    def gain(kk, shape):
        return 1.0 + 0.02 * jax.random.normal(kk, shape, f32)

    return {
        "x": nrm(k[0], (BATCH, SEQ, D), 1.0),
        "c": nrm(k[1], (BATCH, D), 1.0),
        "ctx": nrm(k[2], (BATCH, CTX_LEN, D), 1.0),
        "c_ctx": nrm(k[3], (D,), 1.0),
        "w_mod": nrm(k[4], (L, D, 6 * D), 0.5 * D ** -0.5),
        "b_mod": nrm(k[5], (L, 6 * D), 0.01),
        "norm_mix_g": gain(k[6], (L, D)),
        "norm_ffn_g": gain(k[7], (L, D)),
        "w_in": nrm(k[8], (L, D, D_IN), D ** -0.5),
        "gdn_conv_w": nrm(k[9], (L, GDN_CONV, 1, GDN_QKV), GDN_CONV ** -0.5),
        "gdn_a_log": jnp.log(jax.random.uniform(k[10], (L, 2 * GDN_HEADS), f32, 1.0, 16.0)),
        "gdn_dt_bias": nrm(k[11], (L, 2 * GDN_HEADS), 0.5),
        "gdn_norm_g": gain(k[12], (L, GDN_DV)),
        "mla_q_norm_g": gain(k[13], (L, MLA_Q_RANK)),
        "mla_kv_norm_g": gain(k[14], (L, MLA_KV_RANK)),
        "mla_w_uq": nrm(k[15], (L, MLA_Q_RANK, MLA_HEADS * MLA_QK), MLA_Q_RANK ** -0.5),
        "mla_w_ukv": nrm(k[16], (L, MLA_KV_RANK, MLA_HEADS * (MLA_NOPE + MLA_V)), MLA_KV_RANK ** -0.5),
        "mla_qn_g": gain(k[17], (L, MLA_QK)),
        "mla_kn_g": gain(k[18], (L, MLA_QK)),
        "swa_qn_g": gain(k[19], (L, SWA_HD)),
        "swa_kn_g": gain(k[20], (L, SWA_HD)),
        "swa_sink": nrm(k[21], (L, SWA_HEADS), 0.5),
        "w_branch_a": nrm(k[22], (L, GDN_HEADS * GDN_DV, D), (GDN_HEADS * GDN_DV) ** -0.5),
        "w_branch_b": nrm(k[23], (L, MLA_HEADS * MLA_V, D), (MLA_HEADS * MLA_V) ** -0.5),
        "w_branch_c": nrm(k[24], (L, SWA_HEADS * SWA_HD, D), (SWA_HEADS * SWA_HD) ** -0.5),
        "w_out": nrm(k[25], (L, D, D), D ** -0.5),
        "ffn_w_gate": nrm(k[26], (Le, D, FFN_DENSE), D ** -0.5),
        "ffn_w_up": nrm(k[27], (Le, D, FFN_DENSE), D ** -0.5),
        "ffn_w_down": nrm(k[28], (Le, FFN_DENSE, D), FFN_DENSE ** -0.5),
        "moe_router": nrm(k[29], (Lo, D, N_EXPERTS), D ** -0.5),
        "moe_router_bias": nrm(k[30], (Lo, N_EXPERTS), 0.01),
        "moe_w_gate": nrm(k[31], (Lo, N_EXPERTS, D, FFN_EXPERT), D ** -0.5),
        "moe_w_up": nrm(k[32], (Lo, N_EXPERTS, D, FFN_EXPERT), D ** -0.5),
        "moe_w_down": nrm(k[33], (Lo, N_EXPERTS, FFN_EXPERT, D), FFN_EXPERT ** -0.5),
    }


def reference(x, c, ctx, c_ctx, w_mod, b_mod, norm_mix_g, norm_ffn_g, w_in,
              gdn_conv_w, gdn_a_log, gdn_dt_bias, gdn_norm_g,
              mla_q_norm_g, mla_kv_norm_g, mla_w_uq, mla_w_ukv, mla_qn_g, mla_kn_g,
              swa_qn_g, swa_kn_g, swa_sink,
              w_branch_a, w_branch_b, w_branch_c, w_out,
              ffn_w_gate, ffn_w_up, ffn_w_down,
              moe_router, moe_router_bias, moe_w_gate, moe_w_up, moe_w_down):

    def ffn(h, l):
        i = l // 2
        if l % 2 == 0:
            return swiglu(h, ffn_w_gate[i], ffn_w_up[i], ffn_w_down[i])
        y = moe_swiglu(h.reshape(-1, h.shape[-1]), moe_router[i], moe_router_bias[i],
                       moe_w_gate[i], moe_w_up[i], moe_w_down[i])
        return y.reshape(h.shape)

    xc = ctx
    for l in range(DEPTH):
        need_ctx = l < DEPTH - 1
        mod = jax.nn.silu(c) @ w_mod[l] + b_mod[l]
        mod_c = jax.nn.silu(c_ctx) @ w_mod[l] + b_mod[l]
        sh_m, sc_m, gt_m, sh_f, sc_f, gt_f = jnp.split(mod[:, None, :], 6, axis=-1)
        csh_m, csc_m, cgt_m, csh_f, csc_f, cgt_f = jnp.split(mod_c, 6, axis=-1)

        h = modulate(x, norm_mix_g[l], sh_m, sc_m)
        hc = modulate(xc, norm_mix_g[l], csh_m, csc_m)
        y, yc = hybrid_mixer(h, hc, w_in[l], gdn_conv_w[l], gdn_a_log[l], gdn_dt_bias[l], gdn_norm_g[l],
                             mla_q_norm_g[l], mla_kv_norm_g[l], mla_w_uq[l], mla_w_ukv[l], mla_qn_g[l], mla_kn_g[l],
                             swa_qn_g[l], swa_kn_g[l], swa_sink[l],
                             w_branch_a[l], w_branch_b[l], w_branch_c[l], w_out[l], need_ctx)
        x = x + gt_m * y
        x = x + gt_f * ffn(modulate(x, norm_ffn_g[l], sh_f, sc_f), l)
        if need_ctx:
            xc = xc + cgt_m * yc
            xc = xc + cgt_f * ffn(modulate(xc, norm_ffn_g[l], csh_f, csc_f), l)
    return x
```

```python
import functools
import math

import jax
import jax.numpy as jnp
import numpy as np
from jax import lax
from jax.experimental import pallas as pl
from jax.experimental.pallas import tpu as pltpu

F32 = jnp.float32
BF16 = jnp.bfloat16

GRID_W = 64
EPS = 1e-6
ROPE_BASE = 10000.0
N_BRANCH = 3
GDN_HEADS = 8
GDN_DK = 128
GDN_DV = 128
GDN_CONV = 5
GDN_CHUNK = 64
MLA_HEADS = 8
MLA_Q_RANK = 768
MLA_KV_RANK = 512
MLA_NOPE = 128
MLA_ROPE = 64
MLA_V = 128
MLA_QK = MLA_NOPE + MLA_ROPE
MLA_SCALE = MLA_QK ** -0.5
SWA_HEADS = 8
SWA_KV_HEADS = 2
SWA_HD = 128
SWA_WINDOW = 128
SWA_BLOCK = 128
SWA_SCALE = SWA_HD ** -0.5
N_EXPERTS = 8
TOP_K = 2

LANES = 128
VMEM_LIMIT_BYTES = 56 * 1024 * 1024
MAX_ROW_TILE = 1024
MOD_ROWS = 8
NEG_BIG = -1e30


def _params(*sem):
    return pltpu.CompilerParams(dimension_semantics=sem, vmem_limit_bytes=VMEM_LIMIT_BYTES)


def _pow2_tile(limit, *dims):
    t = 1
    while t * 2 <= limit and all(d % (t * 2) == 0 for d in dims):
        t *= 2
    return t


def _modulate_kernel(x_ref, g_ref, mod_ref, *rest, shift_row, scale_row, with_router):
    x = x_ref[...]
    y = x * lax.rsqrt(jnp.mean(x * x, axis=-1, keepdims=True) + EPS) * g_ref[...]
    mod = mod_ref[0]
    h = y * (1.0 + mod[scale_row:scale_row + 1]) + mod[shift_row:shift_row + 1]
    if with_router:
        rw_ref, rb_ref, h_ref, lg_ref = rest
        lg_ref[...] = jnp.dot(h.astype(BF16), rw_ref[...].astype(BF16), preferred_element_type=F32) + rb_ref[...]
    else:
        (h_ref,) = rest
    h_ref[...] = h.astype(h_ref.dtype)


def modulate(x, gain, modt, rows, gran, shift_row, scale_row, router=None):
    D = x.shape[1]
    tm = gran
    kern = functools.partial(_modulate_kernel, shift_row=shift_row, scale_row=scale_row,
                             with_router=router is not None)
    in_specs = [pl.BlockSpec((tm, D), lambda i: (i, 0)),
                pl.BlockSpec((1, D), lambda i: (0, 0)),
                pl.BlockSpec((1, MOD_ROWS, D), lambda i: (i, 0, 0))]
    args = [x, gain.reshape(1, D), modt]
    out_shape = [jax.ShapeDtypeStruct((rows, D), BF16)]
    out_specs = [pl.BlockSpec((tm, D), lambda i: (i, 0))]
    if router is not None:
        rw, rb = router
        in_specs += [pl.BlockSpec((D, LANES), lambda i: (0, 0)), pl.BlockSpec((1, LANES), lambda i: (0, 0))]
        args += [rw, rb]
        out_shape.append(jax.ShapeDtypeStruct((rows, LANES), F32))
        out_specs.append(pl.BlockSpec((tm, LANES), lambda i: (i, 0)))
    out = pl.pallas_call(kern, grid=(rows // tm,), in_specs=in_specs, out_specs=out_specs,
                         out_shape=out_shape, compiler_params=_params("parallel"))(*args)
    return out if router is not None else out[0]


def _mm_kernel(a_ref, w_ref, *rest, epilogue, gate_row):
    acc = jnp.dot(a_ref[...].astype(BF16), w_ref[...].astype(BF16), preferred_element_type=F32)
    if epilogue == "bias":
        b_ref, o_ref = rest
        acc = acc + b_ref[...]
    elif epilogue == "sigmoid":
        (o_ref,) = rest
        acc = jax.nn.sigmoid(acc)
    elif epilogue == "residual":
        x_ref, mod_ref, o_ref = rest
        acc = x_ref[...] + mod_ref[0][gate_row:gate_row + 1] * acc
    else:
        (o_ref,) = rest
    o_ref[...] = acc.astype(o_ref.dtype)


def matmul(a, w, *, rows=None, w_lead=None, col0=0, ncols=None, tm, tn, out_dtype, epilogue=None,
           bias=None, resid=None, modt=None, gran=None, gate_row=0):
    rows = a.shape[0] if rows is None else rows
    K = a.shape[1]
    ncols = w.shape[-1] if ncols is None else ncols
    assert rows % tm == 0 and ncols % tn == 0 and col0 % tn == 0
    cb = col0 // tn
    if w.ndim == 3:
        w_spec = pl.BlockSpec((None, K, tn), lambda i, j: (w_lead, 0, cb + j))
    else:
        w_spec = pl.BlockSpec((K, tn), lambda i, j: (0, cb + j))
    in_specs = [pl.BlockSpec((tm, K), lambda i, j: (i, 0)), w_spec]
    args = [a, w]
    if epilogue == "bias":
        in_specs.append(pl.BlockSpec((1, tn), lambda i, j: (0, j)))
        args.append(bias)
    elif epilogue == "residual":
        step = tm // gran
        in_specs += [pl.BlockSpec((tm, tn), lambda i, j: (i, j)),
                     pl.BlockSpec((1, MOD_ROWS, tn), lambda i, j: (i * step, 0, j))]
        args += [resid, modt]
    kern = functools.partial(_mm_kernel, epilogue=epilogue, gate_row=gate_row)
    return pl.pallas_call(kern, grid=(rows // tm, ncols // tn), in_specs=in_specs,
                          out_specs=pl.BlockSpec((tm, tn), lambda i, j: (i, j)),
                          out_shape=jax.ShapeDtypeStruct((rows, ncols), out_dtype),
                          compiler_params=_params("parallel", "arbitrary"))(*args)


def _merge_kernel(oa_ref, ob_ref, oc_ref, ga_ref, gb_ref, gc_ref, wa_ref, wb_ref, wc_ref, o_ref):
    acc = ga_ref[...] * jnp.dot(oa_ref[...], wa_ref[...].astype(BF16), preferred_element_type=F32)
    acc += gb_ref[...] * jnp.dot(ob_ref[...], wb_ref[...].astype(BF16), preferred_element_type=F32)
    acc += gc_ref[...] * jnp.dot(oc_ref[...], wc_ref[...].astype(BF16), preferred_element_type=F32)
    o_ref[...] = acc.astype(o_ref.dtype)


def merge_branches(oa, ob, oc, gates, wa, wb, wc, l, rows, tm, tn):
    D = wa.shape[-1]
    nj = D // tn
    o_spec = lambda a: pl.BlockSpec((tm, a.shape[1]), lambda i, j: (i, 0))
    g_spec = lambda k: pl.BlockSpec((tm, tn), lambda i, j: (i, k * nj + j))
    w_spec = lambda w: pl.BlockSpec((None, w.shape[1], tn), lambda i, j: (l, 0, j))
    return pl.pallas_call(
        _merge_kernel, grid=(rows // tm, nj),
        in_specs=[o_spec(oa), o_spec(ob), o_spec(oc), g_spec(0), g_spec(1), g_spec(2),
                  w_spec(wa), w_spec(wb), w_spec(wc)],
        out_specs=pl.BlockSpec((tm, tn), lambda i, j: (i, j)),
        out_shape=jax.ShapeDtypeStruct((rows, D), BF16),
        compiler_params=_params("parallel", "arbitrary"))(oa, ob, oc, gates, gates, gates, wa, wb, wc)


def _swiglu_kernel(be_ref, bv_ref, x_ref, wg_ref, wu_ref, wd_ref, rs_ref, o_ref):
    i, f = pl.program_id(0), pl.program_id(1)

    @pl.when(f == 0)
    def _():
        o_ref[...] = jnp.zeros_like(o_ref)

    @pl.when(bv_ref[i] > 0)
    def _():
        x = x_ref[...]
        g = jnp.dot(x, wg_ref[...].astype(BF16), preferred_element_type=F32)
        u = jnp.dot(x, wu_ref[...].astype(BF16), preferred_element_type=F32)
        h = (jax.nn.silu(g) * u).astype(BF16)
        o_ref[...] += jnp.dot(h, wd_ref[...].astype(BF16), preferred_element_type=F32)

        @pl.when(f == pl.num_programs(1) - 1)
        def _():
            o_ref[...] = o_ref[...] * rs_ref[...]


def swiglu_grouped(x, wg, wu, wd, l, block_e, block_valid, row_scale, rows, tm, tf):
    D = x.shape[1]
    F = wg.shape[-1]
    nf = F // tf
    assert rows % tm == 0 and F % tf == 0

    def f_idx(i, f, bv):
        return jnp.where(bv[i] > 0, f, nf - 1)

    grid_spec = pltpu.PrefetchScalarGridSpec(
        num_scalar_prefetch=2, grid=(rows // tm, nf),
        in_specs=[pl.BlockSpec((tm, D), lambda i, f, be, bv: (i, 0)),
                  pl.BlockSpec((None, None, D, tf), lambda i, f, be, bv: (l, be[i], 0, f_idx(i, f, bv))),
                  pl.BlockSpec((None, None, D, tf), lambda i, f, be, bv: (l, be[i], 0, f_idx(i, f, bv))),
                  pl.BlockSpec((None, None, tf, D), lambda i, f, be, bv: (l, be[i], f_idx(i, f, bv), 0)),
                  pl.BlockSpec((tm, 1), lambda i, f, be, bv: (i, 0))],
        out_specs=pl.BlockSpec((tm, D), lambda i, f, be, bv: (i, 0)))
    return pl.pallas_call(_swiglu_kernel, grid_spec=grid_spec,
                          out_shape=jax.ShapeDtypeStruct((rows, D), F32),
                          compiler_params=_params("parallel", "arbitrary"))(
                              block_e, block_valid, x, wg, wu, wd, row_scale)


def _attn_kernel(q_ref, k_ref, v_ref, o_ref, *, scale):
    s = lax.dot_general(q_ref[...], k_ref[...], (((1,), (1,)), ((), ())), preferred_element_type=F32) * scale
    m = jnp.max(s, axis=-1, keepdims=True)
    p = jnp.exp(s - m)
    p = p * (1.0 / jnp.sum(p, axis=-1, keepdims=True))
    o_ref[...] = jnp.dot(p.astype(BF16), v_ref[...], preferred_element_type=F32).astype(o_ref.dtype)


def full_attention(q, k, v, scale, tq):
    B, H, T, d = q.shape
    L, e = k.shape[2], v.shape[3]
    return pl.pallas_call(
        functools.partial(_attn_kernel, scale=scale), grid=(B, H, T // tq),
        in_specs=[pl.BlockSpec((None, None, tq, d), lambda b, h, i: (b, h, i, 0)),
                  pl.BlockSpec((None, None, L, d), lambda b, h, i: (b, h, 0, 0)),
                  pl.BlockSpec((None, None, L, e), lambda b, h, i: (b, h, 0, 0))],
        out_specs=pl.BlockSpec((None, tq, e), lambda b, h, i: (b, i, h)),
        out_shape=jax.ShapeDtypeStruct((B, T, H * e), BF16),
        compiler_params=_params("parallel", "parallel", "arbitrary"))(q, k, v)


def _swa_kernel(*refs, local, n_blocks, scale):
    if local:
        q_ref, kp_ref, kc_ref, kn_ref, vp_ref, vc_ref, vn_ref, kx_ref, vx_ref, sink_ref, o_ref = refs
    else:
        q_ref, kx_ref, vx_ref, sink_ref, o_ref = refs
    n = pl.program_id(1)
    Bk, d = SWA_BLOCK, SWA_HD
    Lc = kx_ref.shape[0]
    R = SWA_HEADS // SWA_KV_HEADS
    if local:
        iq = lax.broadcasted_iota(jnp.int32, (Bk, 3 * Bk), 0)
        jk = lax.broadcasted_iota(jnp.int32, (Bk, 3 * Bk), 1)
        valid = jnp.abs(iq + Bk - jk) <= SWA_WINDOW
        valid = valid & ((jk >= Bk) | (n > 0)) & ((jk < 2 * Bk) | (n < n_blocks - 1))
        bias = jnp.where(valid, 0.0, NEG_BIG).astype(F32)
        bias = jnp.concatenate([bias, jnp.zeros((Bk, Lc), F32)], axis=1)
    for g in range(SWA_KV_HEADS):
        cs = slice(g * d, (g + 1) * d)
        if local:
            kcat = jnp.concatenate([kp_ref[:, cs], kc_ref[:, cs], kn_ref[:, cs], kx_ref[:, cs]], axis=0)
            vcat = jnp.concatenate([vp_ref[:, cs], vc_ref[:, cs], vn_ref[:, cs], vx_ref[:, cs]], axis=0)
        else:
            kcat, vcat = kx_ref[:, cs], vx_ref[:, cs]
        for r in range(R):
            hq = g * R + r
            qh = q_ref[:, hq * d:(hq + 1) * d]
            s = lax.dot_general(qh, kcat, (((1,), (1,)), ((), ())), preferred_element_type=F32) * scale
            if local:
                s = s + bias
            sink = sink_ref[hq:hq + 1, 0:1]
            m = jnp.maximum(jnp.max(s, axis=-1, keepdims=True), sink)
            p = jnp.exp(s - m)
            p = p * (1.0 / (jnp.sum(p, axis=-1, keepdims=True) + jnp.exp(sink - m)))
            o = jnp.dot(p.astype(BF16), vcat, preferred_element_type=F32)
            o_ref[:, hq * d:(hq + 1) * d] = o.astype(o_ref.dtype)


def window_attention(q, k, v, kx, vx, sink, local):
    B, T, Q = q.shape
    Lc, KV = kx.shape[1], kx.shape[2]
    nb = T // SWA_BLOCK
    sink_b = jnp.broadcast_to(sink.astype(F32)[:, None], (SWA_HEADS, LANES))
    q_spec = pl.BlockSpec((None, SWA_BLOCK, Q), lambda b, n: (b, n, 0))
    x_spec = pl.BlockSpec((None, Lc, KV), lambda b, n: (b, 0, 0))
    s_spec = pl.BlockSpec((SWA_HEADS, LANES), lambda b, n: (0, 0))
    if local:
        prev = pl.BlockSpec((None, SWA_BLOCK, KV), lambda b, n: (b, jnp.maximum(n - 1, 0), 0))
        cur = pl.BlockSpec((None, SWA_BLOCK, KV), lambda b, n: (b, n, 0))
        nxt = pl.BlockSpec((None, SWA_BLOCK, KV), lambda b, n: (b, jnp.minimum(n + 1, nb - 1), 0))
        in_specs = [q_spec, prev, cur, nxt, prev, cur, nxt, x_spec, x_spec, s_spec]
        args = (q, k, k, k, v, v, v, kx, vx, sink_b)
    else:
        in_specs = [q_spec, x_spec, x_spec, s_spec]
        args = (q, kx, vx, sink_b)
    kern = functools.partial(_swa_kernel, local=local, n_blocks=nb, scale=SWA_SCALE)
    return pl.pallas_call(kern, grid=(B, nb), in_specs=in_specs, out_specs=q_spec,
                          out_shape=jax.ShapeDtypeStruct((B, T, Q), BF16),
                          compiler_params=_params("parallel", "arbitrary"))(*args)


def _split3(x):
    hi = x.astype(BF16)
    r1 = x - hi.astype(F32)
    mid = r1.astype(BF16)
    lo = (r1 - mid.astype(F32)).astype(BF16)
    return hi, mid, lo


def _dot_bf16(a, b):
    return jnp.dot(a.astype(BF16), b.astype(BF16), preferred_element_type=F32)


def _dot_x3(a, b):
    ah = a.astype(BF16)
    al = (a - ah.astype(F32)).astype(BF16)
    bh = b.astype(BF16)
    bl = (b - bh.astype(F32)).astype(BF16)
    return (jnp.dot(ah, bh, preferred_element_type=F32) + jnp.dot(ah, bl, preferred_element_type=F32)
            + jnp.dot(al, bh, preferred_element_type=F32))


def _unit_triangular_inverse(m):
    C = m.shape[0]
    eye = (lax.broadcasted_iota(jnp.int32, (C, C), 0) == lax.broadcasted_iota(jnp.int32, (C, C), 1)).astype(F32)
    inv = eye - m
    p = m
    k = 2
    while k < C:
        p = _dot_x3(p, p)
        inv = _dot_x3(inv, eye + p)
        k *= 2
    return inv


def _gdn_prep_kernel(q_ref, k_ref, v_ref, beta_ref, g_ref, gt_ref, wq_ref, u_ref, qk_ref, kd_ref, tot_ref):
    C, H, dk, dv = GDN_CHUNK, GDN_HEADS, GDN_DK, GDN_DV
    ii = lax.broadcasted_iota(jnp.int32, (C, C), 0)
    jj = lax.broadcasted_iota(jnp.int32, (C, C), 1)
    low = (ii >= jj).astype(BF16)
    upp = (ii <= jj).astype(BF16)
    g3 = _split3(g_ref[...])
    gt3 = _split3(gt_ref[...])
    csum = lambda tri, parts: sum(jnp.dot(tri, p, preferred_element_type=F32) for p in parts)
    rsum = lambda parts, tri: sum(jnp.dot(p, tri, preferred_element_type=F32) for p in parts)
    dcol = (csum(low, g3), csum(upp, g3))
    drow = (rsum(gt3, upp), rsum(gt3, low))
    beta = beta_ref[...]
    for d in range(2):
        causal = (ii >= jj) if d == 0 else (ii <= jj)
        strict = (ii > jj) if d == 0 else (ii < jj)
        last = C - 1 if d == 0 else 0
        for h in range(H):
            hd = d * H + h
            dc = dcol[d][:, hd:hd + 1]
            dr = drow[d][hd:hd + 1, :]
            seg = jnp.exp(jnp.where(causal, dc - dr, NEG_BIG))
            bcol = beta[:, hd:hd + 1]
            qh = q_ref[:, h * dk:(h + 1) * dk]
            kh = k_ref[:, h * dk:(h + 1) * dk]
            vh = v_ref[:, h * dv:(h + 1) * dv]
            kb = kh * bcol
            khb = kh.astype(BF16)
            kk = lax.dot_general(kb.astype(BF16), khb, (((1,), (1,)), ((), ())), preferred_element_type=F32)
            inv = _unit_triangular_inverse(jnp.where(strict, kk * seg, 0.0))
            ecol = jnp.exp(dc)
            sol = _dot_x3(inv, jnp.concatenate([kb * ecol, vh * bcol], axis=1))
            qk = lax.dot_general(qh.astype(BF16), khb, (((1,), (1,)), ((), ())), preferred_element_type=F32) * seg
            dlast = dc[last:last + 1, :]
            wq_ref[hd, :C, :] = sol[:, :dk].astype(wq_ref.dtype)
            wq_ref[hd, C:, :] = (qh * ecol).astype(wq_ref.dtype)
            u_ref[hd] = sol[:, dk:]
            qk_ref[hd] = qk.astype(qk_ref.dtype)
            kd_ref[hd] = (kh * jnp.exp(dlast - dc)).astype(kd_ref.dtype)
            tot_ref[hd] = jnp.broadcast_to(jnp.exp(dlast), (1, LANES))


def gdn_prepare_chunks(q, k, v, beta, g, gt):
    B, S, _ = q.shape
    C, H = GDN_CHUNK, GDN_HEADS
    n = S // C
    tok = lambda w: pl.BlockSpec((None, C, w), lambda b, c: (b, c, 0))
    per = lambda r, w: pl.BlockSpec((None, None, 2 * H, r, w), lambda b, c: (b, c, 0, 0, 0))
    shp = lambda r, w, dt: jax.ShapeDtypeStruct((B, n, 2 * H, r, w), dt)
    return pl.pallas_call(
        _gdn_prep_kernel, grid=(B, n),
        in_specs=[tok(H * GDN_DK), tok(H * GDN_DK), tok(H * GDN_DV), tok(LANES), tok(LANES),
                  pl.BlockSpec((None, None, 2 * H, C), lambda b, c: (b, c, 0, 0))],
        out_specs=[per(2 * C, GDN_DK), per(C, GDN_DV), per(C, C), per(C, GDN_DK), per(1, LANES)],
        out_shape=[shp(2 * C, GDN_DK, BF16), shp(C, GDN_DV, F32), shp(C, C, BF16), shp(C, GDN_DK, BF16),
                   shp(1, LANES, F32)],
        compiler_params=_params("parallel", "parallel"))(q, k, v, beta, g, gt)


def _gdn_scan_kernel(wqf, uf, qkf, kdf, totf, wqb, ub, qkb, kdb, totb, of_ref, ob_ref, s_ref):
    C, H, dv = GDN_CHUNK, GDN_HEADS, GDN_DV

    @pl.when(pl.program_id(1) == 0)
    def _():
        s_ref[...] = jnp.zeros_like(s_ref)

    for d, (wq, u, qk, kd, tot, o_ref) in enumerate(((wqf, uf, qkf, kdf, totf, of_ref),
                                                     (wqb, ub, qkb, kdb, totb, ob_ref))):
        for h in range(H):
            hd = d * H + h
            s = s_ref[hd]
            ws = jnp.dot(wq[h], s.astype(BF16), preferred_element_type=F32)
            v_new = (u[h] - ws[:C]).astype(BF16)
            o = ws[C:] + jnp.dot(qk[h], v_new, preferred_element_type=F32)
            s_ref[hd] = s * tot[h] + lax.dot_general(kd[h], v_new, (((0,), (0,)), ((), ())),
                                                     preferred_element_type=F32)
            o_ref[:, h * dv:(h + 1) * dv] = o


def gdn_scan(wq, u, qk, kd, tot, n_ctx):
    B, n = wq.shape[0], wq.shape[1]
    C, H = GDN_CHUNK, GDN_HEADS

    def bwd(s):
        return jnp.where(s < n_ctx, n_ctx - 1 - s, n - 1 - (s - n_ctx))

    fw = lambda r, w: pl.BlockSpec((None, None, H, r, w), lambda b, s: (b, s, 0, 0, 0))
    bw = lambda r, w: pl.BlockSpec((None, None, H, r, w), lambda b, s: (b, bwd(s), 1, 0, 0))
    shapes = ((2 * C, GDN_DK), (C, GDN_DV), (C, C), (C, GDN_DK), (1, LANES))
    o_shape = jax.ShapeDtypeStruct((B, n * C, H * GDN_DV), F32)
    return pl.pallas_call(
        _gdn_scan_kernel, grid=(B, n),
        in_specs=[fw(*s) for s in shapes] + [bw(*s) for s in shapes],
        out_specs=[pl.BlockSpec((None, C, H * GDN_DV), lambda b, s: (b, s, 0)),
                   pl.BlockSpec((None, C, H * GDN_DV), lambda b, s: (b, bwd(s), 0))],
        out_shape=[o_shape, o_shape],
        scratch_shapes=[pltpu.VMEM((2 * H, GDN_DK, GDN_DV), F32)],
        compiler_params=_params("parallel", "arbitrary"))(wq, u, qk, kd, tot, wq, u, qk, kd, tot)


def _rms(x, g):
    xf = x.astype(F32)
    return xf * lax.rsqrt(jnp.mean(xf * xf, axis=-1, keepdims=True) + EPS) * g.astype(F32)


def _l2(x):
    return x * lax.rsqrt(jnp.sum(x * x, axis=-1, keepdims=True) + EPS)


def _rope_tables(T, rot_dim):
    n_freq = rot_dim // 4
    t = jnp.arange(T)
    inv = jnp.power(ROPE_BASE, -jnp.arange(n_freq, dtype=F32) / n_freq)
    ar = (t // GRID_W).astype(F32)[:, None] * inv
    ac = (t % GRID_W).astype(F32)[:, None] * inv
    return jnp.cos(ar), jnp.sin(ar), jnp.cos(ac), jnp.sin(ac)


def _rope(x, tables):
    cr, sr, cc, sc = (t[None, :, None, :] for t in tables)
    n = x.shape[-1] // 4
    a1, a2, b1, b2 = x[..., :n], x[..., n:2 * n], x[..., 2 * n:3 * n], x[..., 3 * n:]
    return jnp.concatenate([a1 * cr - a2 * sr, a2 * cr + a1 * sr, b1 * cc - b2 * sc, b2 * cc + b1 * sc], axis=-1)


def _short_conv_silu(x, w):
    K = w.shape[0]
    pad = (K - 1) // 2
    T = x.shape[1]
    xp = jnp.pad(x, ((0, 0), (pad, pad), (0, 0)))
    y = sum(xp[:, j:j + T] * w[j, 0].astype(F32) for j in range(K))
    return jax.nn.silu(y)


def kernel(x, c, ctx, c_ctx, w_mod, b_mod, norm_mix_g, norm_ffn_g, w_in, gdn_conv_w, gdn_a_log, gdn_dt_bias,
           gdn_norm_g, mla_q_norm_g, mla_kv_norm_g, mla_w_uq, mla_w_ukv, mla_qn_g, mla_kn_g, swa_qn_g, swa_kn_g,
           swa_sink, w_branch_a, w_branch_b, w_branch_c, w_out, ffn_w_gate, ffn_w_up, ffn_w_down, moe_router,
           moe_router_bias, moe_w_gate, moe_w_up, moe_w_down):
    B, T, D = x.shape
    Lc = ctx.shape[1]
    depth = w_mod.shape[0]
    N, Nc = B * T, B * Lc
    M = N + Nc
    TM = _pow2_tile(MAX_ROW_TILE, T, Nc)
    gran = min(TM, 256)
    H = GDN_HEADS
    nk = H * GDN_DK

    sizes = (nk, nk, H * GDN_DV, H * GDN_DV, 2 * H, 2 * H, MLA_Q_RANK, MLA_KV_RANK, MLA_ROPE,
             SWA_HEADS * SWA_HD, SWA_KV_HEADS * SWA_HD, SWA_KV_HEADS * SWA_HD, N_BRANCH * D)
    off = np.concatenate([[0], np.cumsum(sizes)])
    n_gdn = int(off[4])
    n_mix = MLA_Q_RANK + MLA_KV_RANK + LANES + SWA_HEADS * SWA_HD + 2 * SWA_KV_HEADS * SWA_HD
    tn_gdn = _pow2_tile(512, n_gdn)
    tn_mix = 512
    n_mix_padded = -(-n_mix // tn_mix) * tn_mix
    tn_gate = _pow2_tile(512, D)

    rows_all = jnp.concatenate([x.reshape(N, D), ctx.reshape(Nc, D)], axis=0)
    tile_batch = np.minimum(np.arange(M // gran) * gran // T, B)
    tile_batch = np.where(np.arange(M // gran) * gran < N, tile_batch, B)
    cvec = jnp.concatenate([c, c_ctx[None, :], jnp.zeros((MOD_ROWS - B - 1, D), F32)], axis=0)
    cvec = jax.nn.silu(cvec)
    tabs_mla = _rope_tables(T, MLA_ROPE)
    tabs_swa = _rope_tables(T, SWA_HD)
    n_ctx_chunks = Lc // GDN_CHUNK

    for l in range(depth):
        need_ctx = l < depth - 1
        rows_out = M if need_ctx else N

        mod = matmul(cvec, w_mod, w_lead=l, tm=MOD_ROWS, tn=_pow2_tile(1024, 6 * D), out_dtype=F32,
                     epilogue="bias", bias=b_mod[l][None, :])
        mod = mod.reshape(MOD_ROWS, 6, D)[:B + 1]
        mod = jnp.pad(mod, ((0, 0), (0, MOD_ROWS - 6), (0, 0)))
        modt = mod[tile_batch]

        h = modulate(rows_all, norm_mix_g[l], modt, M, gran, shift_row=0, scale_row=1)
        w_l = w_in[l]
        pad_cols = lambda a, n: jnp.pad(a, ((0, 0), (0, n - a.shape[1])))
        w_small = pad_cols(w_l[:, off[4]:off[6]], LANES)
        w_mix = pad_cols(jnp.concatenate([w_l[:, off[6]:off[8]], pad_cols(w_l[:, off[8]:off[9]], LANES),
                                          w_l[:, off[9]:off[12]]], axis=1), n_mix_padded)
        w_gates = w_l[:, off[12]:off[13]]
        zg = matmul(h, w_in, w_lead=l, ncols=n_gdn, tm=TM, tn=tn_gdn, out_dtype=F32)
        zs = matmul(h, w_small, tm=TM, tn=LANES, out_dtype=F32)
        zm = matmul(h, w_mix, tm=TM, tn=tn_mix, out_dtype=F32)
        gates = matmul(h, w_gates, rows=rows_out, tm=TM, tn=tn_gate, out_dtype=F32, epilogue="sigmoid")

        split_rows = lambda a: (a[:N].reshape(B, T, -1), a[N:].reshape(B, Lc, -1))

        zl, zc = split_rows(zg)
        conv_w = gdn_conv_w[l]
        qkv = jnp.concatenate([_short_conv_silu(zc[..., :3 * nk], conv_w),
                               _short_conv_silu(zl[..., :3 * nk], conv_w)], axis=1)
        S = Lc + T
        gq = (_l2(qkv[..., :nk].reshape(B, S, H, GDN_DK)) * GDN_DK ** -0.5).reshape(B, S, nk)
        gk = _l2(qkv[..., nk:2 * nk].reshape(B, S, H, GDN_DK)).reshape(B, S, nk)
        gv = qkv[..., 2 * nk:]
        sl, sc = split_rows(zs)
        zsm = jnp.concatenate([sc, sl], axis=1)
        beta = jax.nn.sigmoid(zsm[..., :2 * H])
        gdec = -jnp.exp(gdn_a_log[l].astype(F32)) * jax.nn.softplus(zsm[..., 2 * H:4 * H] + gdn_dt_bias[l].astype(F32))
        lane_pad = lambda a: jnp.pad(a, ((0, 0), (0, 0), (0, LANES - a.shape[-1])))
        gt = gdec.reshape(B, S // GDN_CHUNK, GDN_CHUNK, 2 * H).swapaxes(2, 3)
        wq_c, u_c, qk_c, kd_c, tot_c = gdn_prepare_chunks(gq, gk, gv, lane_pad(beta), lane_pad(gdec), gt)
        o_f, o_b = gdn_scan(wq_c, u_c, qk_c, kd_c, tot_c, n_ctx_chunks)
        o_gdn = o_f + o_b
        o_gdn = jnp.concatenate([o_gdn[:, Lc:].reshape(N, H, GDN_DV), o_gdn[:, :Lc].reshape(Nc, H, GDN_DV)], axis=0)
        og = zg[:rows_out, 3 * nk:4 * nk].reshape(rows_out, H, GDN_DV)
        o_a = (_rms(o_gdn[:rows_out], gdn_norm_g[l]) * jax.nn.silu(og)).reshape(rows_out, H * GDN_DV).astype(BF16)

        m0 = 0
        cq = zm[:, m0:m0 + MLA_Q_RANK]
        ckv = zm[:, MLA_Q_RANK:MLA_Q_RANK + MLA_KV_RANK]
        kr = zm[:, MLA_Q_RANK + MLA_KV_RANK:MLA_Q_RANK + MLA_KV_RANK + MLA_ROPE]
        s0 = MLA_Q_RANK + MLA_KV_RANK + LANES
        cqn = _rms(cq[:rows_out], mla_q_norm_g[l]).astype(BF16)
        ckvn = _rms(ckv, mla_kv_norm_g[l]).astype(BF16)
        uq = matmul(cqn, mla_w_uq, w_lead=l, tm=TM, tn=_pow2_tile(512, MLA_HEADS * MLA_QK), out_dtype=F32)
        ukv = matmul(ckvn, mla_w_ukv, w_lead=l, tm=TM, tn=_pow2_tile(512, MLA_HEADS * (MLA_NOPE + MLA_V)),
                     out_dtype=F32)
        mq = _rms(uq.reshape(rows_out, MLA_HEADS, MLA_QK), mla_qn_g[l])
        kv = ukv.reshape(M, MLA_HEADS, MLA_NOPE + MLA_V)
        k_full = jnp.concatenate([kv[..., :MLA_NOPE], jnp.broadcast_to(kr[:, None, :], (M, MLA_HEADS, MLA_ROPE))],
                                 axis=-1)
        mk = _rms(k_full, mla_kn_g[l])
        mv = kv[..., MLA_NOPE:]

        def rope_tail(a):
            return jnp.concatenate([a[..., :MLA_NOPE], _rope(a[..., MLA_NOPE:], tabs_mla)], axis=-1)

        heads_first = lambda a: a.swapaxes(1, 2).astype(BF16)
        mq_l = heads_first(rope_tail(mq[:N].reshape(B, T, MLA_HEADS, MLA_QK)))
        mk_l = rope_tail(mk[:N].reshape(B, T, MLA_HEADS, MLA_QK))
        mk_c = mk[N:].reshape(B, Lc, MLA_HEADS, MLA_QK)
        mv_l, mv_c = mv[:N].reshape(B, T, MLA_HEADS, MLA_V), mv[N:].reshape(B, Lc, MLA_HEADS, MLA_V)
        k_all = heads_first(jnp.concatenate([mk_l, mk_c], axis=1))
        v_all = heads_first(jnp.concatenate([mv_l, mv_c], axis=1))
        tq = _pow2_tile(256, T)
        o_b_lat = full_attention(mq_l, k_all, v_all, MLA_SCALE, tq).reshape(N, MLA_HEADS * MLA_V)

        nq = SWA_HEADS * SWA_HD
        nkv = SWA_KV_HEADS * SWA_HD
        sq = _rms(zm[:rows_out, s0:s0 + nq].reshape(rows_out, SWA_HEADS, SWA_HD), swa_qn_g[l])
        sk = _rms(zm[:, s0 + nq:s0 + nq + nkv].reshape(M, SWA_KV_HEADS, SWA_HD), swa_kn_g[l])
        sv = zm[:, s0 + nq + nkv:s0 + nq + 2 * nkv].astype(BF16)
        sq_l = _rope(sq[:N].reshape(B, T, SWA_HEADS, SWA_HD), tabs_swa).reshape(B, T, nq).astype(BF16)
        sk_l = _rope(sk[:N].reshape(B, T, SWA_KV_HEADS, SWA_HD), tabs_swa).reshape(B, T, nkv).astype(BF16)
        sk_c = sk[N:].reshape(B, Lc, nkv).astype(BF16)
        sv_l, sv_c = sv[:N].reshape(B, T, nkv), sv[N:].reshape(B, Lc, nkv)
        o_c_lat = window_attention(sq_l, sk_l, sv_l, sk_c, sv_c, swa_sink[l], local=True).reshape(N, nq)

        if need_ctx:
            mq_c = heads_first(mq[N:].reshape(B, Lc, MLA_HEADS, MLA_QK))
            o_b_ctx = full_attention(mq_c, heads_first(mk_c), heads_first(mv_c), MLA_SCALE,
                                     _pow2_tile(256, Lc)).reshape(Nc, MLA_HEADS * MLA_V)
            sq_c = sq[N:].reshape(B, Lc, nq).astype(BF16)
            o_c_ctx = window_attention(sq_c, None, None, sk_c, sv_c, swa_sink[l], local=False).reshape(Nc, nq)
            o_b = jnp.concatenate([o_b_lat, o_b_ctx], axis=0)
            o_c = jnp.concatenate([o_c_lat, o_c_ctx], axis=0)
        else:
            o_b, o_c = o_b_lat, o_c_lat

        merged = merge_branches(o_a, o_b, o_c, gates, w_branch_a, w_branch_b, w_branch_c, l, rows_out, TM, tn_gate)
        rows_new = matmul(merged, w_out, w_lead=l, tm=TM, tn=tn_gate, out_dtype=F32, epilogue="residual",
                          resid=rows_all, modt=modt, gran=gran, gate_row=2)

        i = l // 2
        if l % 2 == 0:
            h2 = modulate(rows_new, norm_ffn_g[l], modt, rows_out, gran, shift_row=3, scale_row=4)
            nblk = rows_out // TM
            F = ffn_w_gate.shape[-1]
            y = swiglu_grouped(h2, ffn_w_gate[:, None], ffn_w_up[:, None], ffn_w_down[:, None], i,
                               jnp.zeros((nblk,), jnp.int32), jnp.ones((nblk,), jnp.int32),
                               jnp.ones((rows_out, 1), F32), rows_out, TM, _pow2_tile(256, F))
        else:
            E = moe_w_gate.shape[1]
            rw = jnp.pad(moe_router[i], ((0, 0), (0, LANES - E)))
            rb = jnp.pad(moe_router_bias[i].astype(F32), (0, LANES - E))[None, :]
            h2, logits = modulate(rows_new, norm_ffn_g[l], modt, rows_out, gran, shift_row=3, scale_row=4,
                                  router=(rw, rb))
            top_logit, top_idx = lax.top_k(logits[:, :E], TOP_K)
            top_w = jax.nn.softmax(top_logit, axis=-1)
            flat_e = top_idx.reshape(-1)
            onehot = (flat_e[:, None] == jnp.arange(E)[None, :]).astype(jnp.int32)
            rank = jnp.take_along_axis(jnp.cumsum(onehot, axis=0) - onehot, flat_e[:, None], axis=1)[:, 0]
            counts = jnp.sum(onehot, axis=0)
            padded = (counts + TM - 1) // TM * TM
            pstart = jnp.cumsum(padded) - padded
            dest = pstart[flat_e] + rank
            nblk = -(-(rows_out * TOP_K) // TM) + E
            slots = nblk * TM
            src = jnp.zeros((slots,), jnp.int32).at[dest].set(jnp.arange(rows_out * TOP_K, dtype=jnp.int32) // TOP_K)
            w_slot = jnp.zeros((slots,), F32).at[dest].set(top_w.reshape(-1))
            blk_start = jnp.arange(nblk, dtype=jnp.int32) * TM
            ends = jnp.cumsum(padded)
            block_e = jnp.minimum(jnp.searchsorted(ends, blk_start, side="right"), E - 1).astype(jnp.int32)
            block_valid = (blk_start < ends[-1]).astype(jnp.int32)
            last_e = block_e[jnp.maximum(jnp.sum(block_valid) - 1, 0)]
            block_e = jnp.where(block_valid > 0, block_e, last_e)
            F = moe_w_gate.shape[-1]
            ys = swiglu_grouped(h2[src], moe_w_gate, moe_w_up, moe_w_down, i, block_e, block_valid,
                                w_slot[:, None], slots, TM, _pow2_tile(256, F))
            dest2 = dest.reshape(rows_out, TOP_K)
            y = ys[dest2[:, 0]] + ys[dest2[:, 1]]
        gate_f = modt[:, 5][:rows_out // gran]
        rows_ffn = (rows_new.reshape(rows_out // gran, gran, D) + gate_f[:, None, :] * y.reshape(rows_out // gran, gran, D))
        rows_all = rows_ffn.reshape(rows_out, D)

    return rows_all[:N].reshape(B, T, D)
```

```python
import functools

import jax
import jax.numpy as jnp
import numpy as np
from jax import lax
from jax.experimental import pallas as pl
from jax.experimental.pallas import tpu as pltpu

F32 = jnp.float32
BF16 = jnp.bfloat16

GRID_W = 64
EPS = 1e-6
ROPE_BASE = 10000.0
N_BRANCH = 3
GDN_HEADS = 8
GDN_DK = 128
GDN_DV = 128
GDN_CONV = 5
GDN_CHUNK = 64
MLA_HEADS = 8
MLA_Q_RANK = 768
MLA_KV_RANK = 512
MLA_NOPE = 128
MLA_ROPE = 64
MLA_V = 128
MLA_QK = MLA_NOPE + MLA_ROPE
MLA_SCALE = MLA_QK ** -0.5
SWA_HEADS = 8
SWA_KV_HEADS = 2
SWA_HD = 128
SWA_WINDOW = 128
SWA_BLOCK = 128
SWA_SCALE = SWA_HD ** -0.5
N_EXPERTS = 8
TOP_K = 2

LANES = 128
VMEM_LIMIT_BYTES = 56 * 1024 * 1024
MAX_ROW_TILE = 1024
MOD_ROWS = 8
NEG_BIG = -1e30
MLA_QK_PAD = 2 * LANES

MIX_CQ, MIX_SK, MIX_SQ, MIX_CKV, MIX_SV, MIX_KR, MIX_WIDTH = 0, 768, 1024, 2048, 2560, 2816, 3072


def _params(*sem):
    return pltpu.CompilerParams(dimension_semantics=sem, vmem_limit_bytes=VMEM_LIMIT_BYTES)


def _pow2_tile(limit, *dims):
    t = 1
    while t * 2 <= limit and all(d % (t * 2) == 0 for d in dims):
        t *= 2
    return t


def _rms_rows(x, g):
    return x * lax.rsqrt(jnp.mean(x * x, axis=-1, keepdims=True) + EPS) * g


def _modulate_kernel(x_ref, g_ref, mod_ref, *rest, shift_row, scale_row, with_router):
    mod = mod_ref[0]
    h = _rms_rows(x_ref[...], g_ref[...]) * (1.0 + mod[scale_row:scale_row + 1]) + mod[shift_row:shift_row + 1]
    if with_router:
        rw_ref, rb_ref, h_ref, lg_ref = rest
        lg_ref[...] = jnp.dot(h.astype(BF16), rw_ref[...].astype(BF16), preferred_element_type=F32) + rb_ref[...]
    else:
        (h_ref,) = rest
    h_ref[...] = h.astype(h_ref.dtype)


def modulate(x, gain, modt, rows, gran, shift_row, scale_row, router=None):
    D = x.shape[1]
    tm = gran
    kern = functools.partial(_modulate_kernel, shift_row=shift_row, scale_row=scale_row,
                             with_router=router is not None)
    in_specs = [pl.BlockSpec((tm, D), lambda i: (i, 0)),
                pl.BlockSpec((1, D), lambda i: (0, 0)),
                pl.BlockSpec((1, MOD_ROWS, D), lambda i: (i, 0, 0))]
    args = [x, gain.reshape(1, D), modt]
    out_shape = [jax.ShapeDtypeStruct((rows, D), BF16)]
    out_specs = [pl.BlockSpec((tm, D), lambda i: (i, 0))]
    if router is not None:
        rw, rb = router
        in_specs += [pl.BlockSpec((D, LANES), lambda i: (0, 0)), pl.BlockSpec((1, LANES), lambda i: (0, 0))]
        args += [rw, rb]
        out_shape.append(jax.ShapeDtypeStruct((rows, LANES), F32))
        out_specs.append(pl.BlockSpec((tm, LANES), lambda i: (i, 0)))
    out = pl.pallas_call(kern, grid=(rows // tm,), in_specs=in_specs, out_specs=out_specs,
                         out_shape=out_shape, compiler_params=_params("parallel"))(*args)
    return out if router is not None else out[0]


def _mm_kernel(a_ref, w_ref, *rest, epilogue, gate_row):
    acc = jnp.dot(a_ref[...].astype(BF16), w_ref[...].astype(BF16), preferred_element_type=F32)
    if epilogue == "bias":
        b_ref, o_ref = rest
        acc = acc + b_ref[...]
    elif epilogue == "sigmoid":
        (o_ref,) = rest
        acc = jax.nn.sigmoid(acc)
    elif epilogue == "residual":
        x_ref, mod_ref, o_ref = rest
        acc = x_ref[...] + mod_ref[0][gate_row:gate_row + 1] * acc
    else:
        (o_ref,) = rest
    o_ref[...] = acc.astype(o_ref.dtype)


def matmul(a, w, *, rows=None, w_lead=None, col0=0, ncols=None, tm, tn, out_dtype, epilogue=None,
           bias=None, resid=None, modt=None, gran=None, gate_row=0):
    rows = a.shape[0] if rows is None else rows
    K = a.shape[1]
    ncols = w.shape[-1] if ncols is None else ncols
    assert rows % tm == 0 and ncols % tn == 0 and col0 % tn == 0
    cb = col0 // tn
    if w.ndim == 3:
        w_spec = pl.BlockSpec((None, K, tn), lambda i, j: (w_lead, 0, cb + j))
    else:
        w_spec = pl.BlockSpec((K, tn), lambda i, j: (0, cb + j))
    in_specs = [pl.BlockSpec((tm, K), lambda i, j: (i, 0)), w_spec]
    args = [a, w]
    if epilogue == "bias":
        in_specs.append(pl.BlockSpec((1, tn), lambda i, j: (0, j)))
        args.append(bias)
    elif epilogue == "residual":
        step = tm // gran
        in_specs += [pl.BlockSpec((tm, tn), lambda i, j: (i, j)),
                     pl.BlockSpec((1, MOD_ROWS, tn), lambda i, j: (i * step, 0, j))]
        args += [resid, modt]
    kern = functools.partial(_mm_kernel, epilogue=epilogue, gate_row=gate_row)
    return pl.pallas_call(kern, grid=(rows // tm, ncols // tn), in_specs=in_specs,
                          out_specs=pl.BlockSpec((tm, tn), lambda i, j: (i, j)),
                          out_shape=jax.ShapeDtypeStruct((rows, ncols), out_dtype),
                          compiler_params=_params("parallel", "arbitrary"))(*args)


def _merge_kernel(of_ref, ob2_ref, og_ref, gn_ref, ob_ref, oc_ref, ga_ref, gb_ref, gc_ref, wa_ref, wb_ref, wc_ref,
                  o_ref, oa_scr):
    @pl.when(pl.program_id(1) == 0)
    def _():
        for h in range(GDN_HEADS):
            cs = slice(h * GDN_DV, (h + 1) * GDN_DV)
            o = _rms_rows(of_ref[:, cs] + ob2_ref[:, cs], gn_ref[...])
            oa_scr[:, cs] = (o * jax.nn.silu(og_ref[:, cs])).astype(BF16)

    acc = ga_ref[...] * jnp.dot(oa_scr[...], wa_ref[...].astype(BF16), preferred_element_type=F32)
    acc += gb_ref[...] * jnp.dot(ob_ref[...], wb_ref[...].astype(BF16), preferred_element_type=F32)
    acc += gc_ref[...] * jnp.dot(oc_ref[...], wc_ref[...].astype(BF16), preferred_element_type=F32)
    o_ref[...] = acc.astype(o_ref.dtype)


def merge_branches(o_f, o_b2, zg, og_col, gdn_g, ob, oc, gates, wa, wb, wc, l, rows, tm, tn):
    D = wa.shape[-1]
    nj = D // tn
    na = GDN_HEADS * GDN_DV
    row = lambda w, cb=0: pl.BlockSpec((tm, w), lambda i, j: (i, cb))
    g_spec = lambda k: pl.BlockSpec((tm, tn), lambda i, j: (i, k * nj + j))
    w_spec = lambda w: pl.BlockSpec((None, w.shape[1], tn), lambda i, j: (l, 0, j))
    return pl.pallas_call(
        _merge_kernel, grid=(rows // tm, nj),
        in_specs=[row(na), row(na), row(na, og_col // na), pl.BlockSpec((1, GDN_DV), lambda i, j: (0, 0)),
                  row(ob.shape[1]), row(oc.shape[1]), g_spec(0), g_spec(1), g_spec(2),
                  w_spec(wa), w_spec(wb), w_spec(wc)],
        out_specs=pl.BlockSpec((tm, tn), lambda i, j: (i, j)),
        out_shape=jax.ShapeDtypeStruct((rows, D), BF16),
        scratch_shapes=[pltpu.VMEM((tm, na), BF16)],
        compiler_params=_params("parallel", "arbitrary"))(
            o_f, o_b2, zg, gdn_g.reshape(1, GDN_DV).astype(F32), ob, oc, gates, gates, gates, wa, wb, wc)


def _swiglu_kernel(be_ref, bv_ref, x_ref, wg_ref, wu_ref, wd_ref, rs_ref, o_ref):
    i, f = pl.program_id(0), pl.program_id(1)

    @pl.when(f == 0)
    def _():
        o_ref[...] = jnp.zeros_like(o_ref)

    @pl.when(bv_ref[i] > 0)
    def _():
        x = x_ref[...]
        g = jnp.dot(x, wg_ref[...].astype(BF16), preferred_element_type=F32)
        u = jnp.dot(x, wu_ref[...].astype(BF16), preferred_element_type=F32)
        h = (jax.nn.silu(g) * u).astype(BF16)
        o_ref[...] += jnp.dot(h, wd_ref[...].astype(BF16), preferred_element_type=F32)

        @pl.when(f == pl.num_programs(1) - 1)
        def _():
            o_ref[...] = o_ref[...] * rs_ref[...]


def swiglu_grouped(x, wg, wu, wd, l, block_e, block_valid, row_scale, rows, tm, tf):
    D = x.shape[1]
    F = wg.shape[-1]
    nf = F // tf
    assert rows % tm == 0 and F % tf == 0

    def f_idx(i, f, bv):
        return jnp.where(bv[i] > 0, f, nf - 1)

    grid_spec = pltpu.PrefetchScalarGridSpec(
        num_scalar_prefetch=2, grid=(rows // tm, nf),
        in_specs=[pl.BlockSpec((tm, D), lambda i, f, be, bv: (i, 0)),
                  pl.BlockSpec((None, None, D, tf), lambda i, f, be, bv: (l, be[i], 0, f_idx(i, f, bv))),
                  pl.BlockSpec((None, None, D, tf), lambda i, f, be, bv: (l, be[i], 0, f_idx(i, f, bv))),
                  pl.BlockSpec((None, None, tf, D), lambda i, f, be, bv: (l, be[i], f_idx(i, f, bv), 0)),
                  pl.BlockSpec((tm, 1), lambda i, f, be, bv: (i, 0))],
        out_specs=pl.BlockSpec((tm, D), lambda i, f, be, bv: (i, 0)))
    return pl.pallas_call(_swiglu_kernel, grid_spec=grid_spec,
                          out_shape=jax.ShapeDtypeStruct((rows, D), F32),
                          compiler_params=_params("parallel", "arbitrary"))(
                              block_e, block_valid, x, wg, wu, wd, row_scale)


def _rope_tables(T, extra, rot_dim):
    n = rot_dim // 4
    t = jnp.arange(T)
    inv = jnp.power(ROPE_BASE, -jnp.arange(n, dtype=F32) / n)
    ar = (t // GRID_W).astype(F32)[:, None] * inv
    ac = (t % GRID_W).astype(F32)[:, None] * inv
    z = jnp.zeros((T, n), F32)
    pad = lambda a, fill: jnp.concatenate([a, jnp.full((T, LANES - 4 * n), fill, F32)], axis=1)
    c = pad(jnp.concatenate([jnp.cos(ar), jnp.cos(ar), jnp.cos(ac), jnp.cos(ac)], axis=1), 1.0)
    a = pad(jnp.concatenate([-jnp.sin(ar), z, -jnp.sin(ac), z], axis=1), 0.0)
    b = pad(jnp.concatenate([z, jnp.sin(ar), z, jnp.sin(ac)], axis=1), 0.0)
    tab = jnp.concatenate([c, a, b], axis=1)
    ident = jnp.concatenate([jnp.ones((extra, LANES), F32), jnp.zeros((extra, 2 * LANES), F32)], axis=1)
    return jnp.concatenate([tab, ident], axis=0)


def _apply_rope(x, tab, half):
    return (x * tab[:, :LANES] + pltpu.roll(x, LANES - half, 1) * tab[:, LANES:2 * LANES]
            + pltpu.roll(x, half, 1) * tab[:, 2 * LANES:])


def _mla_q_kernel(c_ref, g_ref, w_ref, hg_ref, tab_ref, q_ref, a_scr):
    @pl.when(pl.program_id(1) == 0)
    def _():
        a_scr[...] = _rms_rows(c_ref[...], g_ref[...]).astype(BF16)

    acc = jnp.dot(a_scr[...], w_ref[...].astype(BF16), preferred_element_type=F32)
    y = acc * lax.rsqrt(jnp.sum(acc * acc, axis=-1, keepdims=True) * (1.0 / MLA_QK) + EPS) * hg_ref[...]
    q_ref[:, :LANES] = y[:, :LANES].astype(q_ref.dtype)
    q_ref[:, LANES:] = _apply_rope(y[:, LANES:], tab_ref[...], MLA_ROPE // 4).astype(q_ref.dtype)


def _mla_kv_kernel(c_ref, kr_ref, g_ref, w_ref, hg_ref, tab_ref, k_ref, v_ref, a_scr):
    @pl.when(pl.program_id(1) == 0)
    def _():
        a_scr[...] = _rms_rows(c_ref[...], g_ref[...]).astype(BF16)

    acc = jnp.dot(a_scr[...], w_ref[...].astype(BF16), preferred_element_type=F32)
    kn, kr = acc[:, :LANES], kr_ref[...]
    ss = jnp.sum(kn * kn, axis=-1, keepdims=True) + jnp.sum(kr * kr, axis=-1, keepdims=True)
    r = lax.rsqrt(ss * (1.0 / MLA_QK) + EPS)
    k_ref[:, :LANES] = (kn * r * hg_ref[:, :LANES]).astype(k_ref.dtype)
    k_ref[:, LANES:] = _apply_rope(kr * r * hg_ref[:, LANES:], tab_ref[...], MLA_ROPE // 4).astype(k_ref.dtype)
    v_ref[...] = acc[:, LANES:].astype(v_ref.dtype)


def _seq_pos(i, tm, T, Lc, N):
    nl, nc, nlt = T // tm, Lc // tm, N // tm
    k = i - nlt
    return jnp.where(i < nlt, i // nl, k // nc), jnp.where(i < nlt, i % nl, nl + k % nc)


def mla_project_q(zm, rows, q_norm_g, w_uq_l, qn_g, tab, B, T, Lc, tm):
    N = B * T
    Lq = T + (Lc if rows > N else 0)
    H, R = MLA_HEADS, MLA_Q_RANK
    wp = jnp.pad(w_uq_l.reshape(R, H, MLA_QK), ((0, 0), (0, 0), (0, MLA_QK_PAD - MLA_QK))).reshape(R, H * MLA_QK_PAD)
    hg = jnp.pad(qn_g.astype(F32), (0, MLA_QK_PAD - MLA_QK))[None, :]
    pos = lambda i: _seq_pos(i, tm, T, Lc, N)
    return pl.pallas_call(
        _mla_q_kernel, grid=(rows // tm, H),
        in_specs=[pl.BlockSpec((tm, R), lambda i, j: (i, MIX_CQ // R)),
                  pl.BlockSpec((1, R), lambda i, j: (0, 0)),
                  pl.BlockSpec((R, MLA_QK_PAD), lambda i, j: (0, j)),
                  pl.BlockSpec((1, MLA_QK_PAD), lambda i, j: (0, 0)),
                  pl.BlockSpec((tm, 3 * LANES), lambda i, j: (pos(i)[1], 0))],
        out_specs=pl.BlockSpec((None, None, tm, MLA_QK_PAD), lambda i, j: (pos(i)[0], j, pos(i)[1], 0)),
        out_shape=jax.ShapeDtypeStruct((B, H, Lq, MLA_QK_PAD), BF16),
        scratch_shapes=[pltpu.VMEM((tm, R), BF16)],
        compiler_params=_params("parallel", "arbitrary"))(zm, q_norm_g.reshape(1, R).astype(F32), wp, hg, tab)


def mla_project_kv(zm, kv_norm_g, w_ukv, l, kn_g, tab, B, T, Lc, tm):
    M = zm.shape[0]
    N = B * T
    H, R = MLA_HEADS, MLA_KV_RANK
    hg = jnp.pad(kn_g.astype(F32), (0, MLA_QK_PAD - MLA_QK))[None, :]
    pos = lambda i: _seq_pos(i, tm, T, Lc, N)
    o_spec = lambda w: pl.BlockSpec((None, None, tm, w), lambda i, j: (pos(i)[0], j, pos(i)[1], 0))
    return pl.pallas_call(
        _mla_kv_kernel, grid=(M // tm, H),
        in_specs=[pl.BlockSpec((tm, R), lambda i, j: (i, MIX_CKV // R)),
                  pl.BlockSpec((tm, LANES), lambda i, j: (i, MIX_KR // LANES)),
                  pl.BlockSpec((1, R), lambda i, j: (0, 0)),
                  pl.BlockSpec((None, R, MLA_NOPE + MLA_V), lambda i, j: (l, 0, j)),
                  pl.BlockSpec((1, MLA_QK_PAD), lambda i, j: (0, 0)),
                  pl.BlockSpec((tm, 3 * LANES), lambda i, j: (pos(i)[1], 0))],
        out_specs=[o_spec(MLA_QK_PAD), o_spec(MLA_V)],
        out_shape=[jax.ShapeDtypeStruct((B, H, T + Lc, MLA_QK_PAD), BF16),
                   jax.ShapeDtypeStruct((B, H, T + Lc, MLA_V), BF16)],
        scratch_shapes=[pltpu.VMEM((tm, R), BF16)],
        compiler_params=_params("parallel", "arbitrary"))(zm, zm, kv_norm_g.reshape(1, R).astype(F32), w_ukv, hg, tab)


def _attn_kernel(q_ref, k_ref, v_ref, o_ref, *, scale):
    s = lax.dot_general(q_ref[...], k_ref[...], (((1,), (1,)), ((), ())), preferred_element_type=F32) * scale
    m = jnp.max(s, axis=-1, keepdims=True)
    p = jnp.exp(s - m)
    p = p * (1.0 / jnp.sum(p, axis=-1, keepdims=True))
    o_ref[...] = jnp.dot(p.astype(BF16), v_ref[...], preferred_element_type=F32).astype(o_ref.dtype)


def full_attention(q, k, v, scale, tq, n_q, q_off, kl, k_blk):
    B, H, _, d = q.shape
    e = v.shape[3]
    nt = n_q // tq
    qo = q_off // tq
    return pl.pallas_call(
        functools.partial(_attn_kernel, scale=scale), grid=(B, H, nt),
        in_specs=[pl.BlockSpec((None, None, tq, d), lambda b, h, i: (b, h, qo + i, 0)),
                  pl.BlockSpec((None, None, kl, d), lambda b, h, i: (b, h, k_blk, 0)),
                  pl.BlockSpec((None, None, kl, e), lambda b, h, i: (b, h, k_blk, 0))],
        out_specs=pl.BlockSpec((tq, e), lambda b, h, i: (b * nt + i, h)),
        out_shape=jax.ShapeDtypeStruct((B * n_q, H * e), BF16),
        compiler_params=_params("parallel", "parallel", "arbitrary"))(q, k, v)


def _swa_kernel(*refs, local, n_blocks, scale):
    if local:
        (q_ref, kp_ref, kc_ref, kn_ref, vp_ref, vc_ref, vn_ref, kx_ref, vx_ref, tp_ref, tc_ref, tn_ref,
         qg_ref, kg_ref, sink_ref, o_ref) = refs
    else:
        q_ref, kx_ref, vx_ref, qg_ref, kg_ref, sink_ref, o_ref = refs
    n = pl.program_id(1)
    Bk, d = SWA_BLOCK, SWA_HD
    Lc = kx_ref.shape[0]
    R = SWA_HEADS // SWA_KV_HEADS
    half = SWA_HD // 4

    def prep(x, g, tab):
        y = _rms_rows(x, g)
        return y if tab is None else _apply_rope(y, tab, half)

    if local:
        iq = lax.broadcasted_iota(jnp.int32, (Bk, 3 * Bk), 0)
        jk = lax.broadcasted_iota(jnp.int32, (Bk, 3 * Bk), 1)
        valid = jnp.abs(iq + Bk - jk) <= SWA_WINDOW
        valid = valid & ((jk >= Bk) | (n > 0)) & ((jk < 2 * Bk) | (n < n_blocks - 1))
        bias = jnp.where(valid, 0.0, NEG_BIG).astype(F32)
        bias = jnp.concatenate([bias, jnp.zeros((Bk, Lc), F32)], axis=1)
        tp, tc, tn = tp_ref[...], tc_ref[...], tn_ref[...]
    else:
        tc = None
    kg, qg = kg_ref[...], qg_ref[...]
    for g in range(SWA_KV_HEADS):
        cs = slice(g * d, (g + 1) * d)
        kx = prep(kx_ref[:, cs], kg, None).astype(BF16)
        if local:
            kcat = jnp.concatenate([prep(kp_ref[:, cs], kg, tp).astype(BF16), prep(kc_ref[:, cs], kg, tc).astype(BF16),
                                    prep(kn_ref[:, cs], kg, tn).astype(BF16), kx], axis=0)
            vcat = jnp.concatenate([vp_ref[:, cs].astype(BF16), vc_ref[:, cs].astype(BF16),
                                    vn_ref[:, cs].astype(BF16), vx_ref[:, cs].astype(BF16)], axis=0)
        else:
            kcat, vcat = kx, vx_ref[:, cs].astype(BF16)
        for r in range(R):
            hq = g * R + r
            qh = prep(q_ref[:, hq * d:(hq + 1) * d], qg, tc).astype(BF16)
            s = lax.dot_general(qh, kcat, (((1,), (1,)), ((), ())), preferred_element_type=F32) * scale
            if local:
                s = s + bias
            sink = sink_ref[hq:hq + 1, 0:1]
            m = jnp.maximum(jnp.max(s, axis=-1, keepdims=True), sink)
            p = jnp.exp(s - m)
            p = p * (1.0 / (jnp.sum(p, axis=-1, keepdims=True) + jnp.exp(sink - m)))
            o = jnp.dot(p.astype(BF16), vcat, preferred_element_type=F32)
            o_ref[:, hq * d:(hq + 1) * d] = o.astype(o_ref.dtype)


def window_attention(zm, tab, qn_g, kn_g, sink, B, T, Lc, local):
    N = B * T
    Bk = SWA_BLOCK
    Q, KV = SWA_HEADS * SWA_HD, SWA_KV_HEADS * SWA_HD
    n_q = T if local else Lc
    nb = n_q // Bk
    row0 = 0 if local else N // Bk
    sink_b = jnp.broadcast_to(sink.astype(F32)[:, None], (SWA_HEADS, LANES))
    q_spec = pl.BlockSpec((Bk, Q), lambda b, n: (row0 + b * nb + n, MIX_SQ // Q))
    kx_spec = pl.BlockSpec((Lc, KV), lambda b, n: (N // Lc + b, MIX_SK // KV))
    vx_spec = pl.BlockSpec((Lc, KV), lambda b, n: (N // Lc + b, MIX_SV // KV))
    g_spec = pl.BlockSpec((1, SWA_HD), lambda b, n: (0, 0))
    s_spec = pl.BlockSpec((SWA_HEADS, LANES), lambda b, n: (0, 0))
    gains = (qn_g.reshape(1, SWA_HD).astype(F32), kn_g.reshape(1, SWA_HD).astype(F32), sink_b)
    if local:
        pv = lambda n: jnp.maximum(n - 1, 0)
        nx = lambda n: jnp.minimum(n + 1, nb - 1)
        kv_spec = lambda f, col: pl.BlockSpec((Bk, KV), lambda b, n: (b * nb + f(n), col // KV))
        t_spec = lambda f: pl.BlockSpec((Bk, 3 * LANES), lambda b, n: (f(n), 0))
        same = lambda n: n
        in_specs = [q_spec, kv_spec(pv, MIX_SK), kv_spec(same, MIX_SK), kv_spec(nx, MIX_SK),
                    kv_spec(pv, MIX_SV), kv_spec(same, MIX_SV), kv_spec(nx, MIX_SV), kx_spec, vx_spec,
                    t_spec(pv), t_spec(same), t_spec(nx), g_spec, g_spec, s_spec]
        args = (zm,) * 9 + (tab,) * 3 + gains
    else:
        in_specs = [q_spec, kx_spec, vx_spec, g_spec, g_spec, s_spec]
        args = (zm,) * 3 + gains
    kern = functools.partial(_swa_kernel, local=local, n_blocks=nb, scale=SWA_SCALE)
    return pl.pallas_call(kern, grid=(B, nb), in_specs=in_specs,
                          out_specs=pl.BlockSpec((Bk, Q), lambda b, n: (b * nb + n, 0)),
                          out_shape=jax.ShapeDtypeStruct((B * n_q, Q), BF16),
                          compiler_params=_params("parallel", "arbitrary"))(*args)


def _split3(x):
    hi = x.astype(BF16)
    r1 = x - hi.astype(F32)
    mid = r1.astype(BF16)
    lo = (r1 - mid.astype(F32)).astype(BF16)
    return hi, mid, lo


def _dot_bf16(a, b):
    return jnp.dot(a.astype(BF16), b.astype(BF16), preferred_element_type=F32)


def _dot_nt(a, b):
    return lax.dot_general(a.astype(BF16), b.astype(BF16), (((1,), (1,)), ((), ())), preferred_element_type=F32)


def _gdn_prep_kernel(q_ref, k_ref, v_ref, beta_ref, g_ref, gt_ref, wq_ref, u_ref, qk_ref, kd_ref, tot_ref):
    C, H, dk, dv = GDN_CHUNK, GDN_HEADS, GDN_DK, GDN_DV
    P = 2 * C
    ii = lax.broadcasted_iota(jnp.int32, (C, C), 0)
    jj = lax.broadcasted_iota(jnp.int32, (C, C), 1)
    low = (ii >= jj).astype(BF16)
    upp = (ii <= jj).astype(BF16)
    r = lax.broadcasted_iota(jnp.int32, (P, P), 0)
    c = lax.broadcasted_iota(jnp.int32, (P, P), 1)
    rq = jnp.where(r < C, 0, 1)
    cq = jnp.where(c < C, 0, 1)
    ahead = (r - c) * (1 - 2 * rq)
    causal = (rq == cq) & (ahead >= 0)
    strict = (rq == cq) & (ahead > 0)
    eye = (r == c).astype(F32)
    row_fwd = lax.broadcasted_iota(jnp.int32, (P, 1), 0) < C

    g3 = _split3(g_ref[...])
    gt3 = _split3(gt_ref[...])
    tri = jnp.concatenate([low, upp], axis=0)
    dcol_all = sum(jnp.dot(tri, p, preferred_element_type=F32) for p in g3)
    drow_all = jnp.concatenate([sum(jnp.dot(p[:H], upp, preferred_element_type=F32) for p in gt3),
                                sum(jnp.dot(p[H:], low, preferred_element_type=F32) for p in gt3)], axis=1)
    beta = beta_ref[...]
    beta2 = jnp.concatenate([beta, beta], axis=0)

    def pair_col(a, h):
        return jnp.where(row_fwd, a[:, h:h + 1], a[:, H + h:H + h + 1])

    stack = lambda ref, h, w: jnp.concatenate([ref[:, h * w:(h + 1) * w]] * 2, axis=0)

    group = 4
    for h0 in range(0, H, group):
        hs = range(h0, h0 + group)
        dc = [pair_col(dcol_all, h) for h in hs]
        seg = [jnp.exp(jnp.where(causal, dc[a] - drow_all[h:h + 1, :], NEG_BIG)) for a, h in enumerate(hs)]
        b2 = [pair_col(beta2, h) for h in hs]
        kk = [_dot_nt(stack(k_ref, h, dk) * b2[a], stack(k_ref, h, dk)) for a, h in enumerate(hs)]
        pw = [jnp.where(strict, kk[a] * seg[a], 0.0) for a in range(group)]
        inv = [eye - m for m in pw]
        k = 2
        while k < C:
            pw = [_dot_bf16(m, m) for m in pw]
            inv = [t + _dot_bf16(t, m) for t, m in zip(inv, pw)]
            k *= 2
        for a, h in enumerate(hs):
            k2, q2, v2 = stack(k_ref, h, dk), stack(q_ref, h, dk), stack(v_ref, h, dv)
            ecol = jnp.exp(dc[a])
            rhs = jnp.concatenate([k2 * (b2[a] * ecol), v2 * b2[a]], axis=1)
            sol = rhs + _dot_bf16(inv[a] - eye, rhs)
            qk = _dot_nt(q2, k2) * seg[a]
            dlast = jnp.where(row_fwd, dc[a][C - 1:C], dc[a][C:C + 1])
            qd = q2 * ecol
            kd = k2 * jnp.exp(dlast - dc[a])
            tot = jnp.exp(dlast)
            for d, rs in enumerate((slice(0, C), slice(C, P))):
                hd = d * H + h
                wq_ref[hd, :C, :] = sol[rs, :dk].astype(wq_ref.dtype)
                wq_ref[hd, C:, :] = qd[rs].astype(wq_ref.dtype)
                u_ref[hd] = sol[rs, dk:]
                qk_ref[hd] = qk[rs, d * C:(d + 1) * C].astype(qk_ref.dtype)
                kd_ref[hd] = kd[rs].astype(kd_ref.dtype)
                tot_ref[hd] = jnp.broadcast_to(tot[d * C:d * C + 1], (1, LANES))


def gdn_prepare_chunks(q, k, v, beta, g, gt):
    B, S, _ = q.shape
    C, H = GDN_CHUNK, GDN_HEADS
    n = S // C
    tok = lambda w: pl.BlockSpec((None, C, w), lambda b, c: (b, c, 0))
    per = lambda r, w: pl.BlockSpec((None, None, 2 * H, r, w), lambda b, c: (b, c, 0, 0, 0))
    shp = lambda r, w, dt: jax.ShapeDtypeStruct((B, n, 2 * H, r, w), dt)
    return pl.pallas_call(
        _gdn_prep_kernel, grid=(B, n),
        in_specs=[tok(H * GDN_DK), tok(H * GDN_DK), tok(H * GDN_DV), tok(LANES), tok(LANES),
                  pl.BlockSpec((None, None, 2 * H, C), lambda b, c: (b, c, 0, 0))],
        out_specs=[per(2 * C, GDN_DK), per(C, GDN_DV), per(C, C), per(C, GDN_DK), per(1, LANES)],
        out_shape=[shp(2 * C, GDN_DK, BF16), shp(C, GDN_DV, F32), shp(C, C, BF16), shp(C, GDN_DK, BF16),
                   shp(1, LANES, F32)],
        compiler_params=_params("parallel", "parallel"))(q, k, v, beta, g, gt)


def _gdn_scan_kernel(wqf, uf, qkf, kdf, totf, wqb, ub, qkb, kdb, totb, of_ref, ob_ref, s_ref):
    C, H, dv = GDN_CHUNK, GDN_HEADS, GDN_DV

    @pl.when(pl.program_id(1) == 0)
    def _():
        s_ref[...] = jnp.zeros_like(s_ref)

    for d, (wq, u, qk, kd, tot, o_ref) in enumerate(((wqf, uf, qkf, kdf, totf, of_ref),
                                                     (wqb, ub, qkb, kdb, totb, ob_ref))):
        for h in range(H):
            hd = d * H + h
            s = s_ref[hd]
            ws = jnp.dot(wq[h], s.astype(BF16), preferred_element_type=F32)
            v_new = (u[h] - ws[:C]).astype(BF16)
            o = ws[C:] + jnp.dot(qk[h], v_new, preferred_element_type=F32)
            s_ref[hd] = s * tot[h] + lax.dot_general(kd[h], v_new, (((0,), (0,)), ((), ())),
                                                     preferred_element_type=F32)
            o_ref[:, h * dv:(h + 1) * dv] = o


def gdn_scan(wq, u, qk, kd, tot, n_ctx, T, Lc):
    B, n = wq.shape[0], wq.shape[1]
    C, H = GDN_CHUNK, GDN_HEADS
    N = B * T

    def bwd(s):
        return jnp.where(s < n_ctx, n_ctx - 1 - s, n - 1 - (s - n_ctx))

    def row_block(b, c):
        return jnp.where(c < n_ctx, (N + b * Lc) // C + c, (b * T) // C + c - n_ctx)

    fw = lambda r, w: pl.BlockSpec((None, None, H, r, w), lambda b, s: (b, s, 0, 0, 0))
    bw = lambda r, w: pl.BlockSpec((None, None, H, r, w), lambda b, s: (b, bwd(s), 1, 0, 0))
    shapes = ((2 * C, GDN_DK), (C, GDN_DV), (C, C), (C, GDN_DK), (1, LANES))
    o_shape = jax.ShapeDtypeStruct((B * n * C, H * GDN_DV), F32)
    return pl.pallas_call(
        _gdn_scan_kernel, grid=(B, n),
        in_specs=[fw(*s) for s in shapes] + [bw(*s) for s in shapes],
        out_specs=[pl.BlockSpec((C, H * GDN_DV), lambda b, s: (row_block(b, s), 0)),
                   pl.BlockSpec((C, H * GDN_DV), lambda b, s: (row_block(b, bwd(s)), 0))],
        out_shape=[o_shape, o_shape],
        scratch_shapes=[pltpu.VMEM((2 * H, GDN_DK, GDN_DV), F32)],
        compiler_params=_params("parallel", "arbitrary"))(wq, u, qk, kd, tot, wq, u, qk, kd, tot)


def _l2(x):
    return x * lax.rsqrt(jnp.sum(x * x, axis=-1, keepdims=True) + EPS)


def _short_conv_silu(x, w):
    K = w.shape[0]
    pad = (K - 1) // 2
    T = x.shape[1]
    xp = jnp.pad(x, ((0, 0), (pad, pad), (0, 0)))
    y = sum(xp[:, j:j + T] * w[j, 0].astype(F32) for j in range(K))
    return jax.nn.silu(y)


def kernel(x, c, ctx, c_ctx, w_mod, b_mod, norm_mix_g, norm_ffn_g, w_in, gdn_conv_w, gdn_a_log, gdn_dt_bias,
           gdn_norm_g, mla_q_norm_g, mla_kv_norm_g, mla_w_uq, mla_w_ukv, mla_qn_g, mla_kn_g, swa_qn_g, swa_kn_g,
           swa_sink, w_branch_a, w_branch_b, w_branch_c, w_out, ffn_w_gate, ffn_w_up, ffn_w_down, moe_router,
           moe_router_bias, moe_w_gate, moe_w_up, moe_w_down):
    B, T, D = x.shape
    Lc = ctx.shape[1]
    depth = w_mod.shape[0]
    N, Nc = B * T, B * Lc
    M = N + Nc
    TM = _pow2_tile(MAX_ROW_TILE, T, Nc)
    gran = min(TM, 256)
    tseq = _pow2_tile(256, T, Lc)
    assert T % Lc == 0 and Lc % SWA_BLOCK == 0 and Lc % GDN_CHUNK == 0
    H = GDN_HEADS
    nk = H * GDN_DK

    sizes = (nk, nk, H * GDN_DV, H * GDN_DV, 2 * H, 2 * H, MLA_Q_RANK, MLA_KV_RANK, MLA_ROPE,
             SWA_HEADS * SWA_HD, SWA_KV_HEADS * SWA_HD, SWA_KV_HEADS * SWA_HD, N_BRANCH * D)
    off = np.concatenate([[0], np.cumsum(sizes)])
    n_gdn = int(off[4])
    tn_gdn = _pow2_tile(512, n_gdn)
    tn_mix = 512
    tn_gate = _pow2_tile(512, D)

    rows_all = jnp.concatenate([x.reshape(N, D), ctx.reshape(Nc, D)], axis=0)
    tile_batch = np.minimum(np.arange(M // gran) * gran // T, B)
    tile_batch = np.where(np.arange(M // gran) * gran < N, tile_batch, B)
    cvec = jnp.concatenate([c, c_ctx[None, :], jnp.zeros((MOD_ROWS - B - 1, D), F32)], axis=0)
    cvec = jax.nn.silu(cvec)
    tab_mla = _rope_tables(T, Lc, MLA_ROPE)
    tab_swa = _rope_tables(T, 0, SWA_HD)
    n_ctx_chunks = Lc // GDN_CHUNK
    S = Lc + T

    for l in range(depth):
        need_ctx = l < depth - 1
        rows_out = M if need_ctx else N

        mod = matmul(cvec, w_mod, w_lead=l, tm=MOD_ROWS, tn=_pow2_tile(1024, 6 * D), out_dtype=F32,
                     epilogue="bias", bias=b_mod[l][None, :])
        mod = mod.reshape(MOD_ROWS, 6, D)[:B + 1]
        mod = jnp.pad(mod, ((0, 0), (0, MOD_ROWS - 6), (0, 0)))
        modt = mod[tile_batch]

        h = modulate(rows_all, norm_mix_g[l], modt, M, gran, shift_row=0, scale_row=1)
        w_l = w_in[l]
        seg = lambda a, b: w_l[:, off[a]:off[b]]
        zcols = lambda n: jnp.zeros((D, n), w_l.dtype)
        w_small = jnp.concatenate([seg(4, 6), zcols(LANES - 4 * H)], axis=1)
        w_mix = jnp.concatenate([seg(6, 7), seg(10, 11), seg(9, 10), seg(7, 8), seg(11, 12), seg(8, 9),
                                 zcols(MIX_WIDTH - MIX_KR - MLA_ROPE)], axis=1)
        zg = matmul(h, w_in, w_lead=l, ncols=n_gdn, tm=TM, tn=tn_gdn, out_dtype=F32)
        zs = matmul(h, w_small, tm=TM, tn=LANES, out_dtype=F32)
        zm = matmul(h, w_mix, tm=TM, tn=tn_mix, out_dtype=F32)
        gates = matmul(h, seg(12, 13), rows=rows_out, tm=TM, tn=tn_gate, out_dtype=F32, epilogue="sigmoid")

        split_rows = lambda a: (a[:N].reshape(B, T, -1), a[N:].reshape(B, Lc, -1))

        zl, zc = split_rows(zg)
        conv_w = gdn_conv_w[l]
        qkv = jnp.concatenate([_short_conv_silu(zc[..., :3 * nk], conv_w),
                               _short_conv_silu(zl[..., :3 * nk], conv_w)], axis=1)
        gq = (_l2(qkv[..., :nk].reshape(B, S, H, GDN_DK)) * GDN_DK ** -0.5).reshape(B, S, nk)
        gk = _l2(qkv[..., nk:2 * nk].reshape(B, S, H, GDN_DK)).reshape(B, S, nk)
        gv = qkv[..., 2 * nk:]
        sl, sc = split_rows(zs)
        zsm = jnp.concatenate([sc, sl], axis=1)
        beta = jax.nn.sigmoid(zsm[..., :2 * H])
        gdec = -jnp.exp(gdn_a_log[l].astype(F32)) * jax.nn.softplus(zsm[..., 2 * H:4 * H] + gdn_dt_bias[l].astype(F32))
        lane_pad = lambda a: jnp.pad(a, ((0, 0), (0, 0), (0, LANES - a.shape[-1])))
        gt = gdec.reshape(B, S // GDN_CHUNK, GDN_CHUNK, 2 * H).swapaxes(2, 3)
        wq_c, u_c, qk_c, kd_c, tot_c = gdn_prepare_chunks(gq, gk, gv, lane_pad(beta), lane_pad(gdec), gt)
        o_f, o_b2 = gdn_scan(wq_c, u_c, qk_c, kd_c, tot_c, n_ctx_chunks, T, Lc)

        mq = mla_project_q(zm, rows_out, mla_q_norm_g[l], mla_w_uq[l], mla_qn_g[l], tab_mla, B, T, Lc, tseq)
        mk, mv = mla_project_kv(zm, mla_kv_norm_g[l], mla_w_ukv, l, mla_kn_g[l], tab_mla, B, T, Lc, tseq)
        o_b = full_attention(mq, mk, mv, MLA_SCALE, tseq, T, 0, S, 0)
        o_c = window_attention(zm, tab_swa, swa_qn_g[l], swa_kn_g[l], swa_sink[l], B, T, Lc, local=True)
        if need_ctx:
            o_b = jnp.concatenate([o_b, full_attention(mq, mk, mv, MLA_SCALE, _pow2_tile(256, Lc), Lc, T, Lc, T // Lc)],
                                  axis=0)
            o_c = jnp.concatenate([o_c, window_attention(zm, None, swa_qn_g[l], swa_kn_g[l], swa_sink[l], B, T, Lc,
                                                         local=False)], axis=0)

        merged = merge_branches(o_f, o_b2, zg, 3 * nk, gdn_norm_g[l], o_b, o_c, gates, w_branch_a, w_branch_b,
                                w_branch_c, l, rows_out, TM // 2, tn_gate)
        rows_new = matmul(merged, w_out, w_lead=l, tm=TM, tn=tn_gate, out_dtype=F32, epilogue="residual",
                          resid=rows_all, modt=modt, gran=gran, gate_row=2)

        i = l // 2
        if l % 2 == 0:
            h2 = modulate(rows_new, norm_ffn_g[l], modt, rows_out, gran, shift_row=3, scale_row=4)
            nblk = rows_out // TM
            F = ffn_w_gate.shape[-1]
            y = swiglu_grouped(h2, ffn_w_gate[:, None], ffn_w_up[:, None], ffn_w_down[:, None], i,
                               jnp.zeros((nblk,), jnp.int32), jnp.ones((nblk,), jnp.int32),
                               jnp.ones((rows_out, 1), F32), rows_out, TM, _pow2_tile(256, F))
        else:
            E = moe_w_gate.shape[1]
            rw = jnp.pad(moe_router[i], ((0, 0), (0, LANES - E)))
            rb = jnp.pad(moe_router_bias[i].astype(F32), (0, LANES - E))[None, :]
            h2, logits = modulate(rows_new, norm_ffn_g[l], modt, rows_out, gran, shift_row=3, scale_row=4,
                                  router=(rw, rb))
            logits = (h2.astype(F32) @ moe_router[i]).astype(F32) + moe_router_bias[i].astype(F32)
            top_logit, top_idx = lax.top_k(logits[:, :E], TOP_K)
            top_w = jax.nn.softmax(top_logit, axis=-1)
            flat_e = top_idx.reshape(-1)
            onehot = (flat_e[:, None] == jnp.arange(E)[None, :]).astype(jnp.int32)
            rank = jnp.take_along_axis(jnp.cumsum(onehot, axis=0) - onehot, flat_e[:, None], axis=1)[:, 0]
            counts = jnp.sum(onehot, axis=0)
            padded = (counts + TM - 1) // TM * TM
            pstart = jnp.cumsum(padded) - padded
            dest = pstart[flat_e] + rank
            nblk = -(-(rows_out * TOP_K) // TM) + E
            slots = nblk * TM
            src = jnp.zeros((slots,), jnp.int32).at[dest].set(jnp.arange(rows_out * TOP_K, dtype=jnp.int32) // TOP_K)
            w_slot = jnp.zeros((slots,), F32).at[dest].set(top_w.reshape(-1))
            blk_start = jnp.arange(nblk, dtype=jnp.int32) * TM
            ends = jnp.cumsum(padded)
            block_e = jnp.minimum(jnp.searchsorted(ends, blk_start, side="right"), E - 1).astype(jnp.int32)
            block_valid = (blk_start < ends[-1]).astype(jnp.int32)
            last_e = block_e[jnp.maximum(jnp.sum(block_valid) - 1, 0)]
            block_e = jnp.where(block_valid > 0, block_e, last_e)
            F = moe_w_gate.shape[-1]
            ys = swiglu_grouped(h2[src], moe_w_gate, moe_w_up, moe_w_down, i, block_e, block_valid,
                                w_slot[:, None], slots, TM, _pow2_tile(256, F))
            dest2 = dest.reshape(rows_out, TOP_K)
            y = ys[dest2[:, 0]] + ys[dest2[:, 1]]
        gate_f = modt[:, 5][:rows_out // gran]
        rows_ffn = (rows_new.reshape(rows_out // gran, gran, D) + gate_f[:, None, :] * y.reshape(rows_out // gran, gran, D))
        rows_all = rows_ffn.reshape(rows_out, D)

    return rows_all[:N].reshape(B, T, D)
```

```python
import functools

import jax
import jax.numpy as jnp
import numpy as np
from jax import lax
from jax.experimental import pallas as pl
from jax.experimental.pallas import tpu as pltpu

F32 = jnp.float32
BF16 = jnp.bfloat16

GRID_W = 64
EPS = 1e-6
ROPE_BASE = 10000.0
N_BRANCH = 3
GDN_HEADS = 8
GDN_DK = 128
GDN_DV = 128
GDN_CONV = 5
GDN_CHUNK = 64
MLA_HEADS = 8
MLA_Q_RANK = 768
MLA_KV_RANK = 512
MLA_NOPE = 128
MLA_ROPE = 64
MLA_V = 128
MLA_QK = MLA_NOPE + MLA_ROPE
MLA_SCALE = MLA_QK ** -0.5
SWA_HEADS = 8
SWA_KV_HEADS = 2
SWA_HD = 128
SWA_WINDOW = 128
SWA_BLOCK = 128
SWA_SCALE = SWA_HD ** -0.5
N_EXPERTS = 8
TOP_K = 2

LANES = 128
VMEM_LIMIT_BYTES = 56 * 1024 * 1024
MAX_ROW_TILE = 1024
MOD_ROWS = 8
NEG_BIG = -1e30
MLA_QK_PAD = 2 * LANES

MIX_CQ, MIX_SK, MIX_SQ, MIX_CKV, MIX_SV, MIX_KR, MIX_WIDTH = 0, 768, 1024, 2048, 2560, 2816, 3072


def _params(*sem):
    return pltpu.CompilerParams(dimension_semantics=sem, vmem_limit_bytes=VMEM_LIMIT_BYTES)


def _pow2_tile(limit, *dims):
    t = 1
    while t * 2 <= limit and all(d % (t * 2) == 0 for d in dims):
        t *= 2
    return t


def _rms_rows(x, g):
    return x * lax.rsqrt(jnp.mean(x * x, axis=-1, keepdims=True) + EPS) * g


def _modulate_kernel(x_ref, g_ref, mod_ref, *rest, shift_row, scale_row, with_router):
    mod = mod_ref[0]
    h = _rms_rows(x_ref[...], g_ref[...]) * (1.0 + mod[scale_row:scale_row + 1]) + mod[shift_row:shift_row + 1]
    if with_router:
        rw_ref, rb_ref, h_ref, lg_ref = rest
        lg_ref[...] = jnp.dot(h.astype(BF16), rw_ref[...].astype(BF16), preferred_element_type=F32) + rb_ref[...]
    else:
        (h_ref,) = rest
    h_ref[...] = h.astype(h_ref.dtype)


def modulate(x, gain, modt, rows, gran, shift_row, scale_row, router=None):
    D = x.shape[1]
    tm = gran
    kern = functools.partial(_modulate_kernel, shift_row=shift_row, scale_row=scale_row,
                             with_router=router is not None)
    in_specs = [pl.BlockSpec((tm, D), lambda i: (i, 0)),
                pl.BlockSpec((1, D), lambda i: (0, 0)),
                pl.BlockSpec((1, MOD_ROWS, D), lambda i: (i, 0, 0))]
    args = [x, gain.reshape(1, D), modt]
    out_shape = [jax.ShapeDtypeStruct((rows, D), BF16)]
    out_specs = [pl.BlockSpec((tm, D), lambda i: (i, 0))]
    if router is not None:
        rw, rb = router
        in_specs += [pl.BlockSpec((D, LANES), lambda i: (0, 0)), pl.BlockSpec((1, LANES), lambda i: (0, 0))]
        args += [rw, rb]
        out_shape.append(jax.ShapeDtypeStruct((rows, LANES), F32))
        out_specs.append(pl.BlockSpec((tm, LANES), lambda i: (i, 0)))
    out = pl.pallas_call(kern, grid=(rows // tm,), in_specs=in_specs, out_specs=out_specs,
                         out_shape=out_shape, compiler_params=_params("parallel"))(*args)
    return out if router is not None else out[0]


def _mm_kernel(a_ref, w_ref, *rest, epilogue, gate_row):
    acc = jnp.dot(a_ref[...].astype(BF16), w_ref[...].astype(BF16), preferred_element_type=F32)
    if epilogue == "bias":
        b_ref, o_ref = rest
        acc = acc + b_ref[...]
    elif epilogue == "sigmoid":
        (o_ref,) = rest
        acc = jax.nn.sigmoid(acc)
    elif epilogue == "residual":
        x_ref, mod_ref, o_ref = rest
        acc = x_ref[...] + mod_ref[0][gate_row:gate_row + 1] * acc
    else:
        (o_ref,) = rest
    o_ref[...] = acc.astype(o_ref.dtype)


def matmul(a, w, *, rows=None, w_lead=None, col0=0, ncols=None, tm, tn, out_dtype, epilogue=None,
           bias=None, resid=None, modt=None, gran=None, gate_row=0):
    rows = a.shape[0] if rows is None else rows
    K = a.shape[1]
    ncols = w.shape[-1] if ncols is None else ncols
    assert rows % tm == 0 and ncols % tn == 0 and col0 % tn == 0
    cb = col0 // tn
    if w.ndim == 3:
        w_spec = pl.BlockSpec((None, K, tn), lambda i, j: (w_lead, 0, cb + j))
    else:
        w_spec = pl.BlockSpec((K, tn), lambda i, j: (0, cb + j))
    in_specs = [pl.BlockSpec((tm, K), lambda i, j: (i, 0)), w_spec]
    args = [a, w]
    if epilogue == "bias":
        in_specs.append(pl.BlockSpec((1, tn), lambda i, j: (0, j)))
        args.append(bias)
    elif epilogue == "residual":
        step = tm // gran
        in_specs += [pl.BlockSpec((tm, tn), lambda i, j: (i, j)),
                     pl.BlockSpec((1, MOD_ROWS, tn), lambda i, j: (i * step, 0, j))]
        args += [resid, modt]
    kern = functools.partial(_mm_kernel, epilogue=epilogue, gate_row=gate_row)
    return pl.pallas_call(kern, grid=(rows // tm, ncols // tn), in_specs=in_specs,
                          out_specs=pl.BlockSpec((tm, tn), lambda i, j: (i, j)),
                          out_shape=jax.ShapeDtypeStruct((rows, ncols), out_dtype),
                          compiler_params=_params("parallel", "arbitrary"))(*args)


def _merge_kernel(of_ref, ob2_ref, og_ref, gn_ref, ob_ref, oc_ref, ga_ref, gb_ref, gc_ref, wa_ref, wb_ref, wc_ref,
                  o_ref, oa_scr):
    @pl.when(pl.program_id(1) == 0)
    def _():
        for h in range(GDN_HEADS):
            cs = slice(h * GDN_DV, (h + 1) * GDN_DV)
            o = _rms_rows(of_ref[:, cs] + ob2_ref[:, cs], gn_ref[...])
            oa_scr[:, cs] = (o * jax.nn.silu(og_ref[:, cs])).astype(BF16)

    acc = ga_ref[...] * jnp.dot(oa_scr[...], wa_ref[...].astype(BF16), preferred_element_type=F32)
    acc += gb_ref[...] * jnp.dot(ob_ref[...], wb_ref[...].astype(BF16), preferred_element_type=F32)
    acc += gc_ref[...] * jnp.dot(oc_ref[...], wc_ref[...].astype(BF16), preferred_element_type=F32)
    o_ref[...] = acc.astype(o_ref.dtype)


def merge_branches(o_f, o_b2, zg, og_col, gdn_g, ob, oc, gates, wa, wb, wc, l, rows, tm, tn):
    D = wa.shape[-1]
    nj = D // tn
    na = GDN_HEADS * GDN_DV
    row = lambda w, cb=0: pl.BlockSpec((tm, w), lambda i, j: (i, cb))
    g_spec = lambda k: pl.BlockSpec((tm, tn), lambda i, j: (i, k * nj + j))
    w_spec = lambda w: pl.BlockSpec((None, w.shape[1], tn), lambda i, j: (l, 0, j))
    return pl.pallas_call(
        _merge_kernel, grid=(rows // tm, nj),
        in_specs=[row(na), row(na), row(na, og_col // na), pl.BlockSpec((1, GDN_DV), lambda i, j: (0, 0)),
                  row(ob.shape[1]), row(oc.shape[1]), g_spec(0), g_spec(1), g_spec(2),
                  w_spec(wa), w_spec(wb), w_spec(wc)],
        out_specs=pl.BlockSpec((tm, tn), lambda i, j: (i, j)),
        out_shape=jax.ShapeDtypeStruct((rows, D), BF16),
        scratch_shapes=[pltpu.VMEM((tm, na), BF16)],
        compiler_params=_params("parallel", "arbitrary"))(
            o_f, o_b2, zg, gdn_g.reshape(1, GDN_DV).astype(F32), ob, oc, gates, gates, gates, wa, wb, wc)


def _swiglu_kernel(be_ref, bv_ref, x_ref, wg_ref, wu_ref, wd_ref, rs_ref, o_ref):
    i, f = pl.program_id(0), pl.program_id(1)

    @pl.when(f == 0)
    def _():
        o_ref[...] = jnp.zeros_like(o_ref)

    @pl.when(bv_ref[i] > 0)
    def _():
        x = x_ref[...]
        g = jnp.dot(x, wg_ref[...].astype(BF16), preferred_element_type=F32)
        u = jnp.dot(x, wu_ref[...].astype(BF16), preferred_element_type=F32)
        h = (jax.nn.silu(g) * u).astype(BF16)
        o_ref[...] += jnp.dot(h, wd_ref[...].astype(BF16), preferred_element_type=F32)

        @pl.when(f == pl.num_programs(1) - 1)
        def _():
            o_ref[...] = o_ref[...] * rs_ref[...]


def swiglu_grouped(x, wg, wu, wd, l, block_e, block_valid, row_scale, rows, tm, tf):
    D = x.shape[1]
    F = wg.shape[-1]
    nf = F // tf
    assert rows % tm == 0 and F % tf == 0

    def f_idx(i, f, bv):
        return jnp.where(bv[i] > 0, f, nf - 1)

    grid_spec = pltpu.PrefetchScalarGridSpec(
        num_scalar_prefetch=2, grid=(rows // tm, nf),
        in_specs=[pl.BlockSpec((tm, D), lambda i, f, be, bv: (i, 0)),
                  pl.BlockSpec((None, None, D, tf), lambda i, f, be, bv: (l, be[i], 0, f_idx(i, f, bv))),
                  pl.BlockSpec((None, None, D, tf), lambda i, f, be, bv: (l, be[i], 0, f_idx(i, f, bv))),
                  pl.BlockSpec((None, None, tf, D), lambda i, f, be, bv: (l, be[i], f_idx(i, f, bv), 0)),
                  pl.BlockSpec((tm, 1), lambda i, f, be, bv: (i, 0))],
        out_specs=pl.BlockSpec((tm, D), lambda i, f, be, bv: (i, 0)))
    return pl.pallas_call(_swiglu_kernel, grid_spec=grid_spec,
                          out_shape=jax.ShapeDtypeStruct((rows, D), F32),
                          compiler_params=_params("parallel", "arbitrary"))(
                              block_e, block_valid, x, wg, wu, wd, row_scale)


def _rope_tables(T, extra, rot_dim):
    n = rot_dim // 4
    t = jnp.arange(T)
    inv = jnp.power(ROPE_BASE, -jnp.arange(n, dtype=F32) / n)
    ar = (t // GRID_W).astype(F32)[:, None] * inv
    ac = (t % GRID_W).astype(F32)[:, None] * inv
    z = jnp.zeros((T, n), F32)
    pad = lambda a, fill: jnp.concatenate([a, jnp.full((T, LANES - 4 * n), fill, F32)], axis=1)
    c = pad(jnp.concatenate([jnp.cos(ar), jnp.cos(ar), jnp.cos(ac), jnp.cos(ac)], axis=1), 1.0)
    a = pad(jnp.concatenate([-jnp.sin(ar), z, -jnp.sin(ac), z], axis=1), 0.0)
    b = pad(jnp.concatenate([z, jnp.sin(ar), z, jnp.sin(ac)], axis=1), 0.0)
    tab = jnp.concatenate([c, a, b], axis=1)
    ident = jnp.concatenate([jnp.ones((extra, LANES), F32), jnp.zeros((extra, 2 * LANES), F32)], axis=1)
    return jnp.concatenate([tab, ident], axis=0)


def _apply_rope(x, tab, half):
    return (x * tab[:, :LANES] + pltpu.roll(x, LANES - half, 1) * tab[:, LANES:2 * LANES]
            + pltpu.roll(x, half, 1) * tab[:, 2 * LANES:])


def _mla_q_kernel(c_ref, g_ref, w_ref, hg_ref, tab_ref, q_ref):
    a = _rms_rows(c_ref[...], g_ref[...]).astype(BF16)
    tab, hg = tab_ref[...], hg_ref[...]
    for h in range(MLA_HEADS):
        acc = jnp.dot(a, w_ref[:, h * MLA_QK_PAD:(h + 1) * MLA_QK_PAD], preferred_element_type=F32)
        y = acc * lax.rsqrt(jnp.sum(acc * acc, axis=-1, keepdims=True) * (1.0 / MLA_QK) + EPS) * hg
        q_ref[h, :, :LANES] = y[:, :LANES].astype(q_ref.dtype)
        q_ref[h, :, LANES:] = _apply_rope(y[:, LANES:], tab, MLA_ROPE // 4).astype(q_ref.dtype)


def _mla_kv_kernel(c_ref, kr_ref, g_ref, w_ref, hg_ref, tab_ref, k_ref, v_ref):
    a = _rms_rows(c_ref[...], g_ref[...]).astype(BF16)
    tab, hg = tab_ref[...], hg_ref[...]
    kr = kr_ref[...]
    kr_ss = jnp.sum(kr * kr, axis=-1, keepdims=True)
    width = MLA_NOPE + MLA_V
    for h in range(MLA_HEADS):
        acc = jnp.dot(a, w_ref[:, h * width:(h + 1) * width], preferred_element_type=F32)
        kn = acc[:, :LANES]
        r = lax.rsqrt((jnp.sum(kn * kn, axis=-1, keepdims=True) + kr_ss) * (1.0 / MLA_QK) + EPS)
        k_ref[h, :, :LANES] = (kn * r * hg[:, :LANES]).astype(k_ref.dtype)
        k_ref[h, :, LANES:] = _apply_rope(kr * r * hg[:, LANES:], tab, MLA_ROPE // 4).astype(k_ref.dtype)
        v_ref[h] = acc[:, LANES:].astype(v_ref.dtype)


def _seq_pos(i, tm, T, Lc, N):
    nl, nc, nlt = T // tm, Lc // tm, N // tm
    k = i - nlt
    return jnp.where(i < nlt, i // nl, k // nc), jnp.where(i < nlt, i % nl, nl + k % nc)


def mla_project_q(zm, rows, q_norm_g, w_uq_l, qn_g, tab, B, T, Lc, tm):
    N = B * T
    Lq = T + (Lc if rows > N else 0)
    H, R = MLA_HEADS, MLA_Q_RANK
    wp = jnp.pad(w_uq_l.reshape(R, H, MLA_QK), ((0, 0), (0, 0), (0, MLA_QK_PAD - MLA_QK)))
    wp = wp.reshape(R, H * MLA_QK_PAD).astype(BF16)
    hg = jnp.pad(qn_g.astype(F32) * MLA_SCALE, (0, MLA_QK_PAD - MLA_QK))[None, :]
    pos = lambda i: _seq_pos(i, tm, T, Lc, N)
    return pl.pallas_call(
        _mla_q_kernel, grid=(rows // tm,),
        in_specs=[pl.BlockSpec((tm, R), lambda i: (i, MIX_CQ // R)),
                  pl.BlockSpec((1, R), lambda i: (0, 0)),
                  pl.BlockSpec((R, H * MLA_QK_PAD), lambda i: (0, 0)),
                  pl.BlockSpec((1, MLA_QK_PAD), lambda i: (0, 0)),
                  pl.BlockSpec((tm, 3 * LANES), lambda i: (pos(i)[1], 0))],
        out_specs=pl.BlockSpec((None, H, tm, MLA_QK_PAD), lambda i: (pos(i)[0], 0, pos(i)[1], 0)),
        out_shape=jax.ShapeDtypeStruct((B, H, Lq, MLA_QK_PAD), BF16),
        compiler_params=_params("parallel"))(zm, q_norm_g.reshape(1, R).astype(F32), wp, hg, tab)


def mla_project_kv(zm, kv_norm_g, w_ukv_l, kn_g, tab, B, T, Lc, tm):
    M = zm.shape[0]
    N = B * T
    H, R = MLA_HEADS, MLA_KV_RANK
    hg = jnp.pad(kn_g.astype(F32), (0, MLA_QK_PAD - MLA_QK))[None, :]
    pos = lambda i: _seq_pos(i, tm, T, Lc, N)
    o_spec = lambda w: pl.BlockSpec((None, H, tm, w), lambda i: (pos(i)[0], 0, pos(i)[1], 0))
    return pl.pallas_call(
        _mla_kv_kernel, grid=(M // tm,),
        in_specs=[pl.BlockSpec((tm, R), lambda i: (i, MIX_CKV // R)),
                  pl.BlockSpec((tm, LANES), lambda i: (i, MIX_KR // LANES)),
                  pl.BlockSpec((1, R), lambda i: (0, 0)),
                  pl.BlockSpec((R, H * (MLA_NOPE + MLA_V)), lambda i: (0, 0)),
                  pl.BlockSpec((1, MLA_QK_PAD), lambda i: (0, 0)),
                  pl.BlockSpec((tm, 3 * LANES), lambda i: (pos(i)[1], 0))],
        out_specs=[o_spec(MLA_QK_PAD), o_spec(MLA_V)],
        out_shape=[jax.ShapeDtypeStruct((B, H, T + Lc, MLA_QK_PAD), BF16),
                   jax.ShapeDtypeStruct((B, H, T + Lc, MLA_V), BF16)],
        compiler_params=_params("parallel"))(zm, zm, kv_norm_g.reshape(1, R).astype(F32), w_ukv_l.astype(BF16), hg, tab)


def _attn_kernel(q_ref, k_ref, v_ref, o_ref):
    s = lax.dot_general(q_ref[...], k_ref[...], (((1,), (1,)), ((), ())), preferred_element_type=F32)
    m = jnp.max(s, axis=-1, keepdims=True)
    p = jnp.exp(s - m)
    p = p * (1.0 / jnp.sum(p, axis=-1, keepdims=True))
    o_ref[...] = jnp.dot(p.astype(BF16), v_ref[...], preferred_element_type=F32).astype(o_ref.dtype)


def full_attention(q, k, v, tq, n_q, q_off, kl, k_blk):
    B, H, _, d = q.shape
    e = v.shape[3]
    nt = n_q // tq
    qo = q_off // tq
    return pl.pallas_call(
        _attn_kernel, grid=(B, H, nt),
        in_specs=[pl.BlockSpec((None, None, tq, d), lambda b, h, i: (b, h, qo + i, 0)),
                  pl.BlockSpec((None, None, kl, d), lambda b, h, i: (b, h, k_blk, 0)),
                  pl.BlockSpec((None, None, kl, e), lambda b, h, i: (b, h, k_blk, 0))],
        out_specs=pl.BlockSpec((tq, e), lambda b, h, i: (b * nt + i, h)),
        out_shape=jax.ShapeDtypeStruct((B * n_q, H * e), BF16),
        compiler_params=_params("parallel", "parallel", "arbitrary"))(q, k, v)


def _swa_kernel(*refs, local, n_blocks):
    if local:
        (q_ref, kp_ref, kc_ref, kn_ref, vp_ref, vc_ref, vn_ref, kx_ref, vx_ref, tp_ref, tc_ref, tn_ref,
         qg_ref, kg_ref, sink_ref, o_ref) = refs
    else:
        q_ref, kx_ref, vx_ref, qg_ref, kg_ref, sink_ref, o_ref = refs
    n = pl.program_id(1)
    Bk, d = SWA_BLOCK, SWA_HD
    Lc = kx_ref.shape[0]
    R = SWA_HEADS // SWA_KV_HEADS
    half = SWA_HD // 4

    def prep(x, g, tab):
        y = _rms_rows(x, g)
        return y if tab is None else _apply_rope(y, tab, half)

    if local:
        iq = lax.broadcasted_iota(jnp.int32, (Bk, 3 * Bk), 0)
        jk = lax.broadcasted_iota(jnp.int32, (Bk, 3 * Bk), 1)
        valid = jnp.abs(iq + Bk - jk) <= SWA_WINDOW
        valid = valid & ((jk >= Bk) | (n > 0)) & ((jk < 2 * Bk) | (n < n_blocks - 1))
        bias = jnp.where(valid, 0.0, NEG_BIG).astype(F32)
        bias = jnp.concatenate([bias, jnp.zeros((Bk, Lc), F32)], axis=1)
        tp, tc, tn = tp_ref[...], tc_ref[...], tn_ref[...]
    else:
        tc = None
    kg, qg = kg_ref[...], qg_ref[...]
    for g in range(SWA_KV_HEADS):
        cs = slice(g * d, (g + 1) * d)
        kx = prep(kx_ref[:, cs], kg, None).astype(BF16)
        if local:
            kcat = jnp.concatenate([prep(kp_ref[:, cs], kg, tp).astype(BF16), prep(kc_ref[:, cs], kg, tc).astype(BF16),
                                    prep(kn_ref[:, cs], kg, tn).astype(BF16), kx], axis=0)
            vcat = jnp.concatenate([vp_ref[:, cs].astype(BF16), vc_ref[:, cs].astype(BF16),
                                    vn_ref[:, cs].astype(BF16), vx_ref[:, cs].astype(BF16)], axis=0)
        else:
            kcat, vcat = kx, vx_ref[:, cs].astype(BF16)
        for r in range(R):
            hq = g * R + r
            qh = prep(q_ref[:, hq * d:(hq + 1) * d], qg, tc).astype(BF16)
            s = lax.dot_general(qh, kcat, (((1,), (1,)), ((), ())), preferred_element_type=F32)
            if local:
                s = s + bias
            sink = sink_ref[hq:hq + 1, 0:1]
            m = jnp.maximum(jnp.max(s, axis=-1, keepdims=True), sink)
            p = jnp.exp(s - m)
            p = p * (1.0 / (jnp.sum(p, axis=-1, keepdims=True) + jnp.exp(sink - m)))
            o = jnp.dot(p.astype(BF16), vcat, preferred_element_type=F32)
            o_ref[:, hq * d:(hq + 1) * d] = o.astype(o_ref.dtype)


def window_attention(zm, tab, qn_g, kn_g, sink, B, T, Lc, local):
    N = B * T
    Bk = SWA_BLOCK
    Q, KV = SWA_HEADS * SWA_HD, SWA_KV_HEADS * SWA_HD
    n_q = T if local else Lc
    nb = n_q // Bk
    row0 = 0 if local else N // Bk
    sink_b = jnp.broadcast_to(sink.astype(F32)[:, None], (SWA_HEADS, LANES))
    q_spec = pl.BlockSpec((Bk, Q), lambda b, n: (row0 + b * nb + n, MIX_SQ // Q))
    kx_spec = pl.BlockSpec((Lc, KV), lambda b, n: (N // Lc + b, MIX_SK // KV))
    vx_spec = pl.BlockSpec((Lc, KV), lambda b, n: (N // Lc + b, MIX_SV // KV))
    g_spec = pl.BlockSpec((1, SWA_HD), lambda b, n: (0, 0))
    s_spec = pl.BlockSpec((SWA_HEADS, LANES), lambda b, n: (0, 0))
    gains = (qn_g.reshape(1, SWA_HD).astype(F32) * SWA_SCALE, kn_g.reshape(1, SWA_HD).astype(F32), sink_b)
    if local:
        pv = lambda n: jnp.maximum(n - 1, 0)
        nx = lambda n: jnp.minimum(n + 1, nb - 1)
        kv_spec = lambda f, col: pl.BlockSpec((Bk, KV), lambda b, n: (b * nb + f(n), col // KV))
        t_spec = lambda f: pl.BlockSpec((Bk, 3 * LANES), lambda b, n: (f(n), 0))
        same = lambda n: n
        in_specs = [q_spec, kv_spec(pv, MIX_SK), kv_spec(same, MIX_SK), kv_spec(nx, MIX_SK),
                    kv_spec(pv, MIX_SV), kv_spec(same, MIX_SV), kv_spec(nx, MIX_SV), kx_spec, vx_spec,
                    t_spec(pv), t_spec(same), t_spec(nx), g_spec, g_spec, s_spec]
        args = (zm,) * 9 + (tab,) * 3 + gains
    else:
        in_specs = [q_spec, kx_spec, vx_spec, g_spec, g_spec, s_spec]
        args = (zm,) * 3 + gains
    kern = functools.partial(_swa_kernel, local=local, n_blocks=nb)
    return pl.pallas_call(kern, grid=(B, nb), in_specs=in_specs,
                          out_specs=pl.BlockSpec((Bk, Q), lambda b, n: (b * nb + n, 0)),
                          out_shape=jax.ShapeDtypeStruct((B * n_q, Q), BF16),
                          compiler_params=_params("parallel", "arbitrary"))(*args)


def _split3(x):
    hi = x.astype(BF16)
    r1 = x - hi.astype(F32)
    mid = r1.astype(BF16)
    lo = (r1 - mid.astype(F32)).astype(BF16)
    return hi, mid, lo


def _dot_bf16(a, b):
    return jnp.dot(a.astype(BF16), b.astype(BF16), preferred_element_type=F32)


def _dot_nt(a, b):
    return lax.dot_general(a.astype(BF16), b.astype(BF16), (((1,), (1,)), ((), ())), preferred_element_type=F32)


def _gdn_prep_kernel(zp_ref, zc_ref, zn_ref, zs_ref, cw_ref, ad_ref, wq_ref, u_ref, qk_ref, kd_ref, tot_ref,
                     ext_scr, qkv_scr, *, n_ctx, n_chunks):
    C, H, dk, dv = GDN_CHUNK, GDN_HEADS, GDN_DK, GDN_DV
    P = 2 * C
    nk = H * dk
    halo = 8
    pad = (GDN_CONV - 1) // 2

    ch = pl.program_id(1)
    has_prev = jnp.where((ch == 0) | (ch == n_ctx), 0.0, 1.0)
    has_next = jnp.where((ch == n_ctx - 1) | (ch == n_chunks - 1), 0.0, 1.0)
    ext_scr[0:halo, :] = zp_ref[C - halo:C, :] * has_prev
    ext_scr[halo:halo + C, :] = zc_ref[...]
    ext_scr[halo + C:, :] = zn_ref[0:halo, :] * has_next
    for part in range(3):
        cs = slice(part * nk, (part + 1) * nk)
        y = sum(ext_scr[halo - pad + j:halo - pad + j + C, cs] * cw_ref[j:j + 1, cs] for j in range(GDN_CONV))
        qkv_scr[:, cs] = y * jax.nn.sigmoid(y)
    for h in range(H):
        for part, scale in ((0, dk ** -0.5), (1, 1.0)):
            cs = slice(part * nk + h * dk, part * nk + (h + 1) * dk)
            y = qkv_scr[:, cs]
            qkv_scr[:, cs] = y * (lax.rsqrt(jnp.sum(y * y, axis=-1, keepdims=True) + EPS) * scale)
    q_ref, k_ref, v_ref = qkv_scr.at[:, 0:nk], qkv_scr.at[:, nk:2 * nk], qkv_scr.at[:, 2 * nk:3 * nk]

    zs = zs_ref[...]
    beta = jax.nn.sigmoid(zs)
    a = pltpu.roll(zs, LANES - 2 * H, 1) + ad_ref[1:2, :]
    g = -ad_ref[0:1, :] * (jnp.maximum(a, 0.0) + jnp.log(1.0 + jnp.exp(-jnp.abs(a))))

    ii = lax.broadcasted_iota(jnp.int32, (C, C), 0)
    jj = lax.broadcasted_iota(jnp.int32, (C, C), 1)
    low = (ii >= jj).astype(BF16)
    upp = (ii <= jj).astype(BF16)
    r = lax.broadcasted_iota(jnp.int32, (P, P), 0)
    c = lax.broadcasted_iota(jnp.int32, (P, P), 1)
    rq = jnp.where(r < C, 0, 1)
    cq = jnp.where(c < C, 0, 1)
    ahead = (r - c) * (1 - 2 * rq)
    causal = (rq == cq) & (ahead >= 0)
    strict = (rq == cq) & (ahead > 0)
    eye = (r == c).astype(F32)
    row_fwd = lax.broadcasted_iota(jnp.int32, (P, 1), 0) < C

    g3 = _split3(g)
    tri = jnp.concatenate([low, upp], axis=0)
    dcol_all = sum(jnp.dot(tri, p, preferred_element_type=F32) for p in g3)
    tdot = lambda p, t: lax.dot_general(p, t, (((0,), (0,)), ((), ())), preferred_element_type=F32)
    drow_all = jnp.concatenate([sum(tdot(p, upp) for p in g3)[:H], sum(tdot(p, low) for p in g3)[H:2 * H]],
                               axis=1)
    beta2 = jnp.concatenate([beta, beta], axis=0)

    def pair_col(a, h):
        return jnp.where(row_fwd, a[:, h:h + 1], a[:, H + h:H + h + 1])

    stack = lambda ref, h, w: jnp.concatenate([ref[:, h * w:(h + 1) * w]] * 2, axis=0)

    group = 4
    for h0 in range(0, H, group):
        hs = range(h0, h0 + group)
        dc = [pair_col(dcol_all, h) for h in hs]
        seg = [jnp.exp(jnp.where(causal, dc[a] - drow_all[h:h + 1, :], NEG_BIG)) for a, h in enumerate(hs)]
        b2 = [pair_col(beta2, h) for h in hs]
        kk = [_dot_nt(stack(k_ref, h, dk) * b2[a], stack(k_ref, h, dk)) for a, h in enumerate(hs)]
        pw = [jnp.where(strict, kk[a] * seg[a], 0.0) for a in range(group)]
        inv = [eye - m for m in pw]
        k = 2
        while k < C:
            pw = [_dot_bf16(m, m) for m in pw]
            inv = [t + _dot_bf16(t, m) for t, m in zip(inv, pw)]
            k *= 2
        for a, h in enumerate(hs):
            k2, q2, v2 = stack(k_ref, h, dk), stack(q_ref, h, dk), stack(v_ref, h, dv)
            ecol = jnp.exp(dc[a])
            rhs = jnp.concatenate([k2 * (b2[a] * ecol), v2 * b2[a]], axis=1)
            sol = rhs + _dot_bf16(inv[a] - eye, rhs)
            qk = _dot_nt(q2, k2) * seg[a]
            dlast = jnp.where(row_fwd, dc[a][C - 1:C], dc[a][C:C + 1])
            qd = q2 * ecol
            kd = k2 * jnp.exp(dlast - dc[a])
            tot = jnp.exp(dlast)
            for d, rs in enumerate((slice(0, C), slice(C, P))):
                hd = d * H + h
                wq_ref[hd, :C, :] = sol[rs, :dk].astype(wq_ref.dtype)
                wq_ref[hd, C:, :] = qd[rs].astype(wq_ref.dtype)
                u_ref[hd] = sol[rs, dk:]
                qk_ref[hd] = qk[rs, d * C:(d + 1) * C].astype(qk_ref.dtype)
                kd_ref[hd] = kd[rs].astype(kd_ref.dtype)
                tot_ref[hd] = jnp.broadcast_to(tot[d * C:d * C + 1], (1, LANES))


def gdn_prepare_chunks(zg, zs, conv_w, a_log, dt_bias, B, T, Lc):
    C, H = GDN_CHUNK, GDN_HEADS
    N = B * T
    n_ctx, n = Lc // C, (T + Lc) // C
    width = 3 * H * GDN_DK
    last = zg.shape[0] // C - 1

    def row_block(b, c):
        return jnp.where(c < n_ctx, (N + b * Lc) // C + c, (b * T) // C + c - n_ctx)

    z_spec = lambda d: pl.BlockSpec((C, width), lambda b, c: (jnp.clip(row_block(b, c) + d, 0, last), 0))
    lanes = lambda a: jnp.pad(a.astype(F32), (0, LANES - 2 * H))
    ad = jnp.zeros((8, LANES), F32).at[0].set(lanes(jnp.exp(a_log.astype(F32)))).at[1].set(lanes(dt_bias))
    per = lambda r, w: pl.BlockSpec((None, None, 2 * H, r, w), lambda b, c: (b, c, 0, 0, 0))
    shp = lambda r, w, dt: jax.ShapeDtypeStruct((B, n, 2 * H, r, w), dt)
    return pl.pallas_call(
        functools.partial(_gdn_prep_kernel, n_ctx=n_ctx, n_chunks=n), grid=(B, n),
        in_specs=[z_spec(-1), z_spec(0), z_spec(1),
                  pl.BlockSpec((C, LANES), lambda b, c: (row_block(b, c), 0)),
                  pl.BlockSpec((GDN_CONV, width), lambda b, c: (0, 0)),
                  pl.BlockSpec((8, LANES), lambda b, c: (0, 0))],
        out_specs=[per(2 * C, GDN_DK), per(C, GDN_DV), per(C, C), per(C, GDN_DK), per(1, LANES)],
        out_shape=[shp(2 * C, GDN_DK, BF16), shp(C, GDN_DV, F32), shp(C, C, BF16), shp(C, GDN_DK, BF16),
                   shp(1, LANES, F32)],
        scratch_shapes=[pltpu.VMEM((C + 16, width), F32), pltpu.VMEM((C, width), F32)],
        compiler_params=_params("parallel", "parallel"))(
            zg, zg, zg, zs, conv_w.reshape(GDN_CONV, width).astype(F32), ad)


def _gdn_scan_kernel(wqf, uf, qkf, kdf, totf, wqb, ub, qkb, kdb, totb, of_ref, ob_ref, s_ref):
    C, H, dv = GDN_CHUNK, GDN_HEADS, GDN_DV

    @pl.when(pl.program_id(1) == 0)
    def _():
        s_ref[...] = jnp.zeros_like(s_ref)

    for d, (wq, u, qk, kd, tot, o_ref) in enumerate(((wqf, uf, qkf, kdf, totf, of_ref),
                                                     (wqb, ub, qkb, kdb, totb, ob_ref))):
        hs = range(H)
        ws = [jnp.dot(wq[h], s_ref[d * H + h].astype(BF16), preferred_element_type=F32) for h in hs]
        v_new = [(u[h] - ws[h][:C]).astype(BF16) for h in hs]
        for h in hs:
            o_ref[:, h * dv:(h + 1) * dv] = ws[h][C:] + jnp.dot(qk[h], v_new[h], preferred_element_type=F32)
        upd = [lax.dot_general(kd[h], v_new[h], (((0,), (0,)), ((), ())), preferred_element_type=F32) for h in hs]
        for h in hs:
            s_ref[d * H + h] = s_ref[d * H + h] * tot[h] + upd[h]


def gdn_scan(wq, u, qk, kd, tot, n_ctx, T, Lc):
    B, n = wq.shape[0], wq.shape[1]
    C, H = GDN_CHUNK, GDN_HEADS
    N = B * T

    def bwd(s):
        return jnp.where(s < n_ctx, n_ctx - 1 - s, n - 1 - (s - n_ctx))

    def row_block(b, c):
        return jnp.where(c < n_ctx, (N + b * Lc) // C + c, (b * T) // C + c - n_ctx)

    fw = lambda r, w: pl.BlockSpec((None, None, H, r, w), lambda b, s: (b, s, 0, 0, 0))
    bw = lambda r, w: pl.BlockSpec((None, None, H, r, w), lambda b, s: (b, bwd(s), 1, 0, 0))
    shapes = ((2 * C, GDN_DK), (C, GDN_DV), (C, C), (C, GDN_DK), (1, LANES))
    o_shape = jax.ShapeDtypeStruct((B * n * C, H * GDN_DV), F32)
    return pl.pallas_call(
        _gdn_scan_kernel, grid=(B, n),
        in_specs=[fw(*s) for s in shapes] + [bw(*s) for s in shapes],
        out_specs=[pl.BlockSpec((C, H * GDN_DV), lambda b, s: (row_block(b, s), 0)),
                   pl.BlockSpec((C, H * GDN_DV), lambda b, s: (row_block(b, bwd(s)), 0))],
        out_shape=[o_shape, o_shape],
        scratch_shapes=[pltpu.VMEM((2 * H, GDN_DK, GDN_DV), F32)],
        compiler_params=_params("parallel", "arbitrary"))(wq, u, qk, kd, tot, wq, u, qk, kd, tot)


def kernel(x, c, ctx, c_ctx, w_mod, b_mod, norm_mix_g, norm_ffn_g, w_in, gdn_conv_w, gdn_a_log, gdn_dt_bias,
           gdn_norm_g, mla_q_norm_g, mla_kv_norm_g, mla_w_uq, mla_w_ukv, mla_qn_g, mla_kn_g, swa_qn_g, swa_kn_g,
           swa_sink, w_branch_a, w_branch_b, w_branch_c, w_out, ffn_w_gate, ffn_w_up, ffn_w_down, moe_router,
           moe_router_bias, moe_w_gate, moe_w_up, moe_w_down):
    B, T, D = x.shape
    Lc = ctx.shape[1]
    depth = w_mod.shape[0]
    N, Nc = B * T, B * Lc
    M = N + Nc
    TM = _pow2_tile(MAX_ROW_TILE, T, Nc)
    gran = min(TM, 256)
    tseq = _pow2_tile(256, T, Lc)
    assert T % Lc == 0 and Lc % SWA_BLOCK == 0 and Lc % GDN_CHUNK == 0
    H = GDN_HEADS
    nk = H * GDN_DK

    sizes = (nk, nk, H * GDN_DV, H * GDN_DV, 2 * H, 2 * H, MLA_Q_RANK, MLA_KV_RANK, MLA_ROPE,
             SWA_HEADS * SWA_HD, SWA_KV_HEADS * SWA_HD, SWA_KV_HEADS * SWA_HD, N_BRANCH * D)
    off = np.concatenate([[0], np.cumsum(sizes)])
    n_gdn = int(off[4])
    tn_gdn = _pow2_tile(512, n_gdn)
    tn_mix = 512
    tn_gate = _pow2_tile(512, D)

    rows_all = jnp.concatenate([x.reshape(N, D), ctx.reshape(Nc, D)], axis=0)
    tile_batch = np.minimum(np.arange(M // gran) * gran // T, B)
    tile_batch = np.where(np.arange(M // gran) * gran < N, tile_batch, B)
    cvec = jnp.concatenate([c, c_ctx[None, :], jnp.zeros((MOD_ROWS - B - 1, D), F32)], axis=0)
    cvec = jax.nn.silu(cvec)
    tab_mla = _rope_tables(T, Lc, MLA_ROPE)
    tab_swa = _rope_tables(T, 0, SWA_HD)
    n_ctx_chunks = Lc // GDN_CHUNK
    S = Lc + T

    for l in range(depth):
        need_ctx = l < depth - 1
        rows_out = M if need_ctx else N

        mod = matmul(cvec, w_mod, w_lead=l, tm=MOD_ROWS, tn=_pow2_tile(1024, 6 * D), out_dtype=F32,
                     epilogue="bias", bias=b_mod[l][None, :])
        mod = mod.reshape(MOD_ROWS, 6, D)[:B + 1]
        mod = jnp.pad(mod, ((0, 0), (0, MOD_ROWS - 6), (0, 0)))
        modt = mod[tile_batch]

        h = modulate(rows_all, norm_mix_g[l], modt, M, gran, shift_row=0, scale_row=1)
        w_l = w_in[l]
        seg = lambda a, b: w_l[:, off[a]:off[b]]
        zcols = lambda n: jnp.zeros((D, n), w_l.dtype)
        w_small = jnp.concatenate([seg(4, 6), zcols(LANES - 4 * H)], axis=1)
        w_mix = jnp.concatenate([seg(6, 7), seg(10, 11), seg(9, 10), seg(7, 8), seg(11, 12), seg(8, 9),
                                 zcols(MIX_WIDTH - MIX_KR - MLA_ROPE)], axis=1)
        zg = matmul(h, w_in, w_lead=l, ncols=n_gdn, tm=TM, tn=tn_gdn, out_dtype=F32)
        zs = matmul(h, w_small, tm=TM, tn=LANES, out_dtype=F32)
        zm = matmul(h, w_mix, tm=TM, tn=tn_mix, out_dtype=F32)
        gates = matmul(h, seg(12, 13), rows=rows_out, tm=TM, tn=tn_gate, out_dtype=F32, epilogue="sigmoid")

        wq_c, u_c, qk_c, kd_c, tot_c = gdn_prepare_chunks(zg, zs, gdn_conv_w[l], gdn_a_log[l], gdn_dt_bias[l], B, T, Lc)
        o_f, o_b2 = gdn_scan(wq_c, u_c, qk_c, kd_c, tot_c, n_ctx_chunks, T, Lc)

        mq = mla_project_q(zm, rows_out, mla_q_norm_g[l], mla_w_uq[l], mla_qn_g[l], tab_mla, B, T, Lc, tseq)
        mk, mv = mla_project_kv(zm, mla_kv_norm_g[l], mla_w_ukv[l], mla_kn_g[l], tab_mla, B, T, Lc, tseq)
        o_b = full_attention(mq, mk, mv, tseq, T, 0, S, 0)
        o_c = window_attention(zm, tab_swa, swa_qn_g[l], swa_kn_g[l], swa_sink[l], B, T, Lc, local=True)
        if need_ctx:
            o_b = jnp.concatenate([o_b, full_attention(mq, mk, mv, _pow2_tile(256, Lc), Lc, T, Lc, T // Lc)], axis=0)
            o_c = jnp.concatenate([o_c, window_attention(zm, None, swa_qn_g[l], swa_kn_g[l], swa_sink[l], B, T, Lc,
                                                         local=False)], axis=0)

        merged = merge_branches(o_f, o_b2, zg, 3 * nk, gdn_norm_g[l], o_b, o_c, gates, w_branch_a, w_branch_b,
                                w_branch_c, l, rows_out, TM // 2, tn_gate)
        rows_new = matmul(merged, w_out, w_lead=l, tm=TM, tn=tn_gate, out_dtype=F32, epilogue="residual",
                          resid=rows_all, modt=modt, gran=gran, gate_row=2)

        i = l // 2
        if l % 2 == 0:
            h2 = modulate(rows_new, norm_ffn_g[l], modt, rows_out, gran, shift_row=3, scale_row=4)
            nblk = rows_out // TM
            F = ffn_w_gate.shape[-1]
            y = swiglu_grouped(h2, ffn_w_gate[:, None], ffn_w_up[:, None], ffn_w_down[:, None], i,
                               jnp.zeros((nblk,), jnp.int32), jnp.ones((nblk,), jnp.int32),
                               jnp.ones((rows_out, 1), F32), rows_out, TM, _pow2_tile(256, F))
        else:
            E = moe_w_gate.shape[1]
            rw = jnp.pad(moe_router[i], ((0, 0), (0, LANES - E)))
            rb = jnp.pad(moe_router_bias[i].astype(F32), (0, LANES - E))[None, :]
            h2, logits = modulate(rows_new, norm_ffn_g[l], modt, rows_out, gran, shift_row=3, scale_row=4,
                                  router=(rw, rb))
            top_logit, top_idx = lax.top_k(logits[:, :E], TOP_K)
            top_w = jax.nn.softmax(top_logit, axis=-1)
            flat_e = top_idx.reshape(-1)
            onehot = (flat_e[:, None] == jnp.arange(E)[None, :]).astype(jnp.int32)
            rank = jnp.take_along_axis(jnp.cumsum(onehot, axis=0) - onehot, flat_e[:, None], axis=1)[:, 0]
            counts = jnp.sum(onehot, axis=0)
            padded = (counts + TM - 1) // TM * TM
            pstart = jnp.cumsum(padded) - padded
            dest = pstart[flat_e] + rank
            nblk = -(-(rows_out * TOP_K) // TM) + E
            slots = nblk * TM
            src = jnp.zeros((slots,), jnp.int32).at[dest].set(jnp.arange(rows_out * TOP_K, dtype=jnp.int32) // TOP_K)
            w_slot = jnp.zeros((slots,), F32).at[dest].set(top_w.reshape(-1))
            blk_start = jnp.arange(nblk, dtype=jnp.int32) * TM
            ends = jnp.cumsum(padded)
            block_e = jnp.minimum(jnp.searchsorted(ends, blk_start, side="right"), E - 1).astype(jnp.int32)
            block_valid = (blk_start < ends[-1]).astype(jnp.int32)
            last_e = block_e[jnp.maximum(jnp.sum(block_valid) - 1, 0)]
            block_e = jnp.where(block_valid > 0, block_e, last_e)
            F = moe_w_gate.shape[-1]
            ys = swiglu_grouped(h2[src], moe_w_gate, moe_w_up, moe_w_down, i, block_e, block_valid,
                                w_slot[:, None], slots, TM, _pow2_tile(256, F))
            dest2 = dest.reshape(rows_out, TOP_K)
            y = ys[dest2[:, 0]] + ys[dest2[:, 1]]
        gate_f = modt[:, 5][:rows_out // gran]
        rows_ffn = (rows_new.reshape(rows_out // gran, gran, D) + gate_f[:, None, :] * y.reshape(rows_out // gran, gran, D))
        rows_all = rows_ffn.reshape(rows_out, D)

    return rows_all[:N].reshape(B, T, D)
```

```python
import functools

import jax
import jax.numpy as jnp
import numpy as np
from jax import lax
from jax.experimental import pallas as pl
from jax.experimental.pallas import tpu as pltpu

F32 = jnp.float32
BF16 = jnp.bfloat16

GRID_W = 64
EPS = 1e-6
ROPE_BASE = 10000.0
N_BRANCH = 3
GDN_HEADS = 8
GDN_DK = 128
GDN_DV = 128
GDN_CONV = 5
GDN_CHUNK = 64
MLA_HEADS = 8
MLA_Q_RANK = 768
MLA_KV_RANK = 512
MLA_NOPE = 128
MLA_ROPE = 64
MLA_V = 128
MLA_QK = MLA_NOPE + MLA_ROPE
MLA_SCALE = MLA_QK ** -0.5
SWA_HEADS = 8
SWA_KV_HEADS = 2
SWA_HD = 128
SWA_WINDOW = 128
SWA_BLOCK = 128
SWA_SCALE = SWA_HD ** -0.5
N_EXPERTS = 8
TOP_K = 2

LANES = 128
VMEM_LIMIT_BYTES = 56 * 1024 * 1024
MAX_ROW_TILE = 1024
MOD_ROWS = 8
NEG_BIG = -1e30
MLA_QK_PAD = 2 * LANES

MIX_CQ, MIX_SK, MIX_SQ, MIX_CKV, MIX_SV, MIX_KR, MIX_WIDTH = 0, 768, 1024, 2048, 2560, 2816, 3072


def _params(*sem):
    return pltpu.CompilerParams(dimension_semantics=sem, vmem_limit_bytes=VMEM_LIMIT_BYTES)


def _pow2_tile(limit, *dims):
    t = 1
    while t * 2 <= limit and all(d % (t * 2) == 0 for d in dims):
        t *= 2
    return t


def _rms_rows(x, g):
    return x * lax.rsqrt(jnp.mean(x * x, axis=-1, keepdims=True) + EPS) * g


def _modulate_kernel(x_ref, g_ref, mod_ref, *rest, shift_row, scale_row, with_router):
    mod = mod_ref[0]
    h = _rms_rows(x_ref[...], g_ref[...]) * (1.0 + mod[scale_row:scale_row + 1]) + mod[shift_row:shift_row + 1]
    if with_router:
        rw_ref, rb_ref, h_ref, lg_ref = rest
        lg_ref[...] = jnp.dot(h.astype(BF16), rw_ref[...].astype(BF16), preferred_element_type=F32) + rb_ref[...]
    else:
        (h_ref,) = rest
    h_ref[...] = h.astype(h_ref.dtype)


def modulate(x, gain, modt, rows, gran, shift_row, scale_row, router=None):
    D = x.shape[1]
    tm = gran
    kern = functools.partial(_modulate_kernel, shift_row=shift_row, scale_row=scale_row,
                             with_router=router is not None)
    in_specs = [pl.BlockSpec((tm, D), lambda i: (i, 0)),
                pl.BlockSpec((1, D), lambda i: (0, 0)),
                pl.BlockSpec((1, MOD_ROWS, D), lambda i: (i, 0, 0))]
    args = [x, gain.reshape(1, D), modt]
    out_shape = [jax.ShapeDtypeStruct((rows, D), BF16)]
    out_specs = [pl.BlockSpec((tm, D), lambda i: (i, 0))]
    if router is not None:
        rw, rb = router
        in_specs += [pl.BlockSpec((D, LANES), lambda i: (0, 0)), pl.BlockSpec((1, LANES), lambda i: (0, 0))]
        args += [rw, rb]
        out_shape.append(jax.ShapeDtypeStruct((rows, LANES), F32))
        out_specs.append(pl.BlockSpec((tm, LANES), lambda i: (i, 0)))
    out = pl.pallas_call(kern, grid=(rows // tm,), in_specs=in_specs, out_specs=out_specs,
                         out_shape=out_shape, compiler_params=_params("parallel"))(*args)
    return out if router is not None else out[0]


def _mm_kernel(a_ref, w_ref, *rest, epilogue, gate_row):
    acc = jnp.dot(a_ref[...].astype(BF16), w_ref[...].astype(BF16), preferred_element_type=F32)
    if epilogue == "bias":
        b_ref, o_ref = rest
        acc = acc + b_ref[...]
    elif epilogue == "sigmoid":
        (o_ref,) = rest
        acc = jax.nn.sigmoid(acc)
    elif epilogue == "residual":
        x_ref, mod_ref, o_ref = rest
        acc = x_ref[...] + mod_ref[0][gate_row:gate_row + 1] * acc
    else:
        (o_ref,) = rest
    o_ref[...] = acc.astype(o_ref.dtype)


def matmul(a, w, *, rows=None, w_lead=None, col0=0, ncols=None, tm, tn, out_dtype, epilogue=None,
           bias=None, resid=None, modt=None, gran=None, gate_row=0):
    rows = a.shape[0] if rows is None else rows
    K = a.shape[1]
    ncols = w.shape[-1] if ncols is None else ncols
    assert rows % tm == 0 and ncols % tn == 0 and col0 % tn == 0
    cb = col0 // tn
    if w.ndim == 3:
        w_spec = pl.BlockSpec((None, K, tn), lambda i, j: (w_lead, 0, cb + j))
    else:
        w_spec = pl.BlockSpec((K, tn), lambda i, j: (0, cb + j))
    in_specs = [pl.BlockSpec((tm, K), lambda i, j: (i, 0)), w_spec]
    args = [a, w]
    if epilogue == "bias":
        in_specs.append(pl.BlockSpec((1, tn), lambda i, j: (0, j)))
        args.append(bias)
    elif epilogue == "residual":
        step = tm // gran
        in_specs += [pl.BlockSpec((tm, tn), lambda i, j: (i, j)),
                     pl.BlockSpec((1, MOD_ROWS, tn), lambda i, j: (i * step, 0, j))]
        args += [resid, modt]
    kern = functools.partial(_mm_kernel, epilogue=epilogue, gate_row=gate_row)
    return pl.pallas_call(kern, grid=(rows // tm, ncols // tn), in_specs=in_specs,
                          out_specs=pl.BlockSpec((tm, tn), lambda i, j: (i, j)),
                          out_shape=jax.ShapeDtypeStruct((rows, ncols), out_dtype),
                          compiler_params=_params("parallel", "arbitrary"))(*args)


def _merge_kernel(h_ref, of_ref, ob2_ref, og_ref, gn_ref, ob_ref, oc_ref, wga_ref, wgb_ref, wgc_ref,
                  wa_ref, wb_ref, wc_ref, o_ref, oa_scr):
    @pl.when(pl.program_id(1) == 0)
    def _():
        for h in range(GDN_HEADS):
            cs = slice(h * GDN_DV, (h + 1) * GDN_DV)
            o = _rms_rows(of_ref[:, cs] + ob2_ref[:, cs], gn_ref[...])
            oa_scr[:, cs] = (o * jax.nn.silu(og_ref[:, cs])).astype(BF16)

    hm = h_ref[...]
    gate = lambda w_ref: jax.nn.sigmoid(jnp.dot(hm, w_ref[...], preferred_element_type=F32))
    acc = gate(wga_ref) * jnp.dot(oa_scr[...], wa_ref[...], preferred_element_type=F32)
    acc += gate(wgb_ref) * jnp.dot(ob_ref[...], wb_ref[...], preferred_element_type=F32)
    acc += gate(wgc_ref) * jnp.dot(oc_ref[...], wc_ref[...], preferred_element_type=F32)
    o_ref[...] = acc.astype(o_ref.dtype)


def merge_branches(h, w_gates, o_f, o_b2, zg, og_col, gdn_g, ob, oc, wa, wb, wc, l, rows, tm, tn):
    D = wa.shape[-1]
    nj = D // tn
    na = GDN_HEADS * GDN_DV
    row = lambda w, cb=0: pl.BlockSpec((tm, w), lambda i, j: (i, cb))
    g_spec = lambda k: pl.BlockSpec((D, tn), lambda i, j: (0, k * nj + j))
    wa, wb, wc = (w[l].astype(BF16) for w in (wa, wb, wc))
    w_gates = w_gates.astype(BF16)
    w_spec = lambda w: pl.BlockSpec((w.shape[0], tn), lambda i, j: (0, j))
    return pl.pallas_call(
        _merge_kernel, grid=(rows // tm, nj),
        in_specs=[row(D), row(na), row(na), row(na, og_col // na), pl.BlockSpec((1, GDN_DV), lambda i, j: (0, 0)),
                  row(ob.shape[1]), row(oc.shape[1]), g_spec(0), g_spec(1), g_spec(2),
                  w_spec(wa), w_spec(wb), w_spec(wc)],
        out_specs=pl.BlockSpec((tm, tn), lambda i, j: (i, j)),
        out_shape=jax.ShapeDtypeStruct((rows, D), BF16),
        scratch_shapes=[pltpu.VMEM((tm, na), BF16)],
        compiler_params=_params("parallel", "arbitrary"))(
            h, o_f, o_b2, zg, gdn_g.reshape(1, GDN_DV).astype(F32), ob, oc, w_gates, w_gates, w_gates, wa, wb, wc)


def _swiglu_kernel(be_ref, bv_ref, x_ref, wg_ref, wu_ref, wd_ref, rs_ref, o_ref):
    i, f = pl.program_id(0), pl.program_id(1)

    @pl.when(f == 0)
    def _():
        o_ref[...] = jnp.zeros_like(o_ref)

    def accumulate(n):
        x = x_ref[:n, :]
        g = jnp.dot(x, wg_ref[...].astype(BF16), preferred_element_type=F32)
        u = jnp.dot(x, wu_ref[...].astype(BF16), preferred_element_type=F32)
        h = (jax.nn.silu(g) * u).astype(BF16)
        o_ref[:n, :] += jnp.dot(h, wd_ref[...].astype(BF16), preferred_element_type=F32)

        @pl.when(f == pl.num_programs(1) - 1)
        def _():
            o_ref[:n, :] = o_ref[:n, :] * rs_ref[:n, :]

    tm = x_ref.shape[0]
    pl.when(bv_ref[i] == 2)(functools.partial(accumulate, tm))
    pl.when(bv_ref[i] == 1)(functools.partial(accumulate, tm // 2))


def swiglu_grouped(x, wg, wu, wd, l, block_e, block_valid, row_scale, rows, tm, tf):
    D = x.shape[1]
    F = wg.shape[-1]
    nf = F // tf
    assert rows % tm == 0 and F % tf == 0

    def f_idx(i, f, bv):
        return jnp.where(bv[i] > 0, f, nf - 1)

    grid_spec = pltpu.PrefetchScalarGridSpec(
        num_scalar_prefetch=2, grid=(rows // tm, nf),
        in_specs=[pl.BlockSpec((tm, D), lambda i, f, be, bv: (i, 0)),
                  pl.BlockSpec((None, None, D, tf), lambda i, f, be, bv: (l, be[i], 0, f_idx(i, f, bv))),
                  pl.BlockSpec((None, None, D, tf), lambda i, f, be, bv: (l, be[i], 0, f_idx(i, f, bv))),
                  pl.BlockSpec((None, None, tf, D), lambda i, f, be, bv: (l, be[i], f_idx(i, f, bv), 0)),
                  pl.BlockSpec((tm, 1), lambda i, f, be, bv: (i, 0))],
        out_specs=pl.BlockSpec((tm, D), lambda i, f, be, bv: (i, 0)))
    return pl.pallas_call(_swiglu_kernel, grid_spec=grid_spec,
                          out_shape=jax.ShapeDtypeStruct((rows, D), F32),
                          compiler_params=_params("parallel", "arbitrary"))(
                              block_e, block_valid, x, wg, wu, wd, row_scale)


def _rope_tables(T, extra, rot_dim):
    n = rot_dim // 4
    t = jnp.arange(T)
    inv = jnp.power(ROPE_BASE, -jnp.arange(n, dtype=F32) / n)
    ar = (t // GRID_W).astype(F32)[:, None] * inv
    ac = (t % GRID_W).astype(F32)[:, None] * inv
    z = jnp.zeros((T, n), F32)
    pad = lambda a, fill: jnp.concatenate([a, jnp.full((T, LANES - 4 * n), fill, F32)], axis=1)
    c = pad(jnp.concatenate([jnp.cos(ar), jnp.cos(ar), jnp.cos(ac), jnp.cos(ac)], axis=1), 1.0)
    a = pad(jnp.concatenate([-jnp.sin(ar), z, -jnp.sin(ac), z], axis=1), 0.0)
    b = pad(jnp.concatenate([z, jnp.sin(ar), z, jnp.sin(ac)], axis=1), 0.0)
    tab = jnp.concatenate([c, a, b], axis=1)
    ident = jnp.concatenate([jnp.ones((extra, LANES), F32), jnp.zeros((extra, 2 * LANES), F32)], axis=1)
    return jnp.concatenate([tab, ident], axis=0)


def _apply_rope(x, tab, half):
    return (x * tab[:, :LANES] + pltpu.roll(x, LANES - half, 1) * tab[:, LANES:2 * LANES]
            + pltpu.roll(x, half, 1) * tab[:, 2 * LANES:])


def _mla_q_kernel(c_ref, g_ref, w_ref, hg_ref, tab_ref, q_ref):
    a = _rms_rows(c_ref[...], g_ref[...]).astype(BF16)
    tab, hg = tab_ref[...], hg_ref[...]
    for h in range(MLA_HEADS):
        acc = jnp.dot(a, w_ref[:, h * MLA_QK_PAD:(h + 1) * MLA_QK_PAD], preferred_element_type=F32)
        y = acc * lax.rsqrt(jnp.sum(acc * acc, axis=-1, keepdims=True) * (1.0 / MLA_QK) + EPS) * hg
        q_ref[h, :, :LANES] = y[:, :LANES].astype(q_ref.dtype)
        q_ref[h, :, LANES:] = _apply_rope(y[:, LANES:], tab, MLA_ROPE // 4).astype(q_ref.dtype)


def _mla_kv_kernel(c_ref, kr_ref, g_ref, w_ref, hg_ref, tab_ref, k_ref, v_ref):
    a = _rms_rows(c_ref[...], g_ref[...]).astype(BF16)
    tab, hg = tab_ref[...], hg_ref[...]
    kr = kr_ref[...]
    kr_ss = jnp.sum(kr * kr, axis=-1, keepdims=True)
    width = MLA_NOPE + MLA_V
    for h in range(MLA_HEADS):
        acc = jnp.dot(a, w_ref[:, h * width:(h + 1) * width], preferred_element_type=F32)
        kn = acc[:, :LANES]
        r = lax.rsqrt((jnp.sum(kn * kn, axis=-1, keepdims=True) + kr_ss) * (1.0 / MLA_QK) + EPS)
        k_ref[h, :, :LANES] = (kn * r * hg[:, :LANES]).astype(k_ref.dtype)
        k_ref[h, :, LANES:] = _apply_rope(kr * r * hg[:, LANES:], tab, MLA_ROPE // 4).astype(k_ref.dtype)
        v_ref[h] = acc[:, LANES:].astype(v_ref.dtype)


def _seq_pos(i, tm, T, Lc, N):
    nl, nc, nlt = T // tm, Lc // tm, N // tm
    k = i - nlt
    return jnp.where(i < nlt, i // nl, k // nc), jnp.where(i < nlt, i % nl, nl + k % nc)


def mla_project_q(zm, rows, q_norm_g, w_uq_l, qn_g, tab, B, T, Lc, tm):
    N = B * T
    Lq = T + (Lc if rows > N else 0)
    H, R = MLA_HEADS, MLA_Q_RANK
    wp = jnp.pad(w_uq_l.reshape(R, H, MLA_QK), ((0, 0), (0, 0), (0, MLA_QK_PAD - MLA_QK)))
    wp = wp.reshape(R, H * MLA_QK_PAD).astype(BF16)
    hg = jnp.pad(qn_g.astype(F32) * MLA_SCALE, (0, MLA_QK_PAD - MLA_QK))[None, :]
    pos = lambda i: _seq_pos(i, tm, T, Lc, N)
    return pl.pallas_call(
        _mla_q_kernel, grid=(rows // tm,),
        in_specs=[pl.BlockSpec((tm, R), lambda i: (i, MIX_CQ // R)),
                  pl.BlockSpec((1, R), lambda i: (0, 0)),
                  pl.BlockSpec((R, H * MLA_QK_PAD), lambda i: (0, 0)),
                  pl.BlockSpec((1, MLA_QK_PAD), lambda i: (0, 0)),
                  pl.BlockSpec((tm, 3 * LANES), lambda i: (pos(i)[1], 0))],
        out_specs=pl.BlockSpec((None, H, tm, MLA_QK_PAD), lambda i: (pos(i)[0], 0, pos(i)[1], 0)),
        out_shape=jax.ShapeDtypeStruct((B, H, Lq, MLA_QK_PAD), BF16),
        compiler_params=_params("parallel"))(zm, q_norm_g.reshape(1, R).astype(F32), wp, hg, tab)


def mla_project_kv(zm, kv_norm_g, w_ukv_l, kn_g, tab, B, T, Lc, tm):
    M = zm.shape[0]
    N = B * T
    H, R = MLA_HEADS, MLA_KV_RANK
    hg = jnp.pad(kn_g.astype(F32), (0, MLA_QK_PAD - MLA_QK))[None, :]
    pos = lambda i: _seq_pos(i, tm, T, Lc, N)
    o_spec = lambda w: pl.BlockSpec((None, H, tm, w), lambda i: (pos(i)[0], 0, pos(i)[1], 0))
    return pl.pallas_call(
        _mla_kv_kernel, grid=(M // tm,),
        in_specs=[pl.BlockSpec((tm, R), lambda i: (i, MIX_CKV // R)),
                  pl.BlockSpec((tm, LANES), lambda i: (i, MIX_KR // LANES)),
                  pl.BlockSpec((1, R), lambda i: (0, 0)),
                  pl.BlockSpec((R, H * (MLA_NOPE + MLA_V)), lambda i: (0, 0)),
                  pl.BlockSpec((1, MLA_QK_PAD), lambda i: (0, 0)),
                  pl.BlockSpec((tm, 3 * LANES), lambda i: (pos(i)[1], 0))],
        out_specs=[o_spec(MLA_QK_PAD), o_spec(MLA_V)],
        out_shape=[jax.ShapeDtypeStruct((B, H, T + Lc, MLA_QK_PAD), BF16),
                   jax.ShapeDtypeStruct((B, H, T + Lc, MLA_V), BF16)],
        compiler_params=_params("parallel"))(zm, zm, kv_norm_g.reshape(1, R).astype(F32), w_ukv_l.astype(BF16), hg, tab)


def _attn_kernel(q_ref, k_ref, v_ref, o_ref):
    s = lax.dot_general(q_ref[...], k_ref[...], (((1,), (1,)), ((), ())), preferred_element_type=F32)
    m = jnp.max(s, axis=-1, keepdims=True)
    p = jnp.exp(s - m)
    p = p * (1.0 / jnp.sum(p, axis=-1, keepdims=True))
    o_ref[...] = jnp.dot(p.astype(BF16), v_ref[...], preferred_element_type=F32).astype(o_ref.dtype)


def full_attention(q, k, v, tq, n_q, q_off, kl, k_blk):
    B, H, _, d = q.shape
    e = v.shape[3]
    nt = n_q // tq
    qo = q_off // tq
    return pl.pallas_call(
        _attn_kernel, grid=(B, H, nt),
        in_specs=[pl.BlockSpec((None, None, tq, d), lambda b, h, i: (b, h, qo + i, 0)),
                  pl.BlockSpec((None, None, kl, d), lambda b, h, i: (b, h, k_blk, 0)),
                  pl.BlockSpec((None, None, kl, e), lambda b, h, i: (b, h, k_blk, 0))],
        out_specs=pl.BlockSpec((tq, e), lambda b, h, i: (b * nt + i, h)),
        out_shape=jax.ShapeDtypeStruct((B * n_q, H * e), BF16),
        compiler_params=_params("parallel", "parallel", "arbitrary"))(q, k, v)


def _swa_kernel(*refs, local, n_blocks):
    if local:
        (q_ref, kp_ref, kc_ref, kn_ref, vp_ref, vc_ref, vn_ref, kx_ref, vx_ref, tp_ref, tc_ref, tn_ref,
         qg_ref, kg_ref, sink_ref, o_ref) = refs
    else:
        q_ref, kx_ref, vx_ref, qg_ref, kg_ref, sink_ref, o_ref = refs
    n = pl.program_id(1)
    Bk, d = SWA_BLOCK, SWA_HD
    Lc = kx_ref.shape[0]
    R = SWA_HEADS // SWA_KV_HEADS
    half = SWA_HD // 4

    def prep(x, g, tab):
        y = _rms_rows(x, g)
        return y if tab is None else _apply_rope(y, tab, half)

    if local:
        iq = lax.broadcasted_iota(jnp.int32, (Bk, 3 * Bk), 0)
        jk = lax.broadcasted_iota(jnp.int32, (Bk, 3 * Bk), 1)
        valid = jnp.abs(iq + Bk - jk) <= SWA_WINDOW
        valid = valid & ((jk >= Bk) | (n > 0)) & ((jk < 2 * Bk) | (n < n_blocks - 1))
        bias = jnp.where(valid, 0.0, NEG_BIG).astype(F32)
        bias = jnp.concatenate([bias, jnp.zeros((Bk, Lc), F32)], axis=1)
        tp, tc, tn = tp_ref[...], tc_ref[...], tn_ref[...]
    else:
        tc = None
    kg, qg = kg_ref[...], qg_ref[...]
    kcat, vcat = [], []
    for g in range(SWA_KV_HEADS):
        cs = slice(g * d, (g + 1) * d)
        kx = prep(kx_ref[:, cs], kg, None).astype(BF16)
        if local:
            kcat.append(jnp.concatenate([prep(kp_ref[:, cs], kg, tp).astype(BF16), prep(kc_ref[:, cs], kg, tc).astype(BF16),
                                         prep(kn_ref[:, cs], kg, tn).astype(BF16), kx], axis=0))
            vcat.append(jnp.concatenate([vp_ref[:, cs].astype(BF16), vc_ref[:, cs].astype(BF16),
                                         vn_ref[:, cs].astype(BF16), vx_ref[:, cs].astype(BF16)], axis=0))
        else:
            kcat.append(kx)
            vcat.append(vx_ref[:, cs].astype(BF16))
    hs = range(SWA_HEADS)
    qh = [prep(q_ref[:, h * d:(h + 1) * d], qg, tc).astype(BF16) for h in hs]
    s = [lax.dot_general(qh[h], kcat[h // R], (((1,), (1,)), ((), ())), preferred_element_type=F32) for h in hs]
    if local:
        s = [sh + bias for sh in s]
    sink = [sink_ref[h:h + 1, 0:1] for h in hs]
    m = [jnp.maximum(jnp.max(s[h], axis=-1, keepdims=True), sink[h]) for h in hs]
    p = [jnp.exp(s[h] - m[h]) for h in hs]
    inv = [1.0 / (jnp.sum(p[h], axis=-1, keepdims=True) + jnp.exp(sink[h] - m[h])) for h in hs]
    for h in hs:
        o = jnp.dot((p[h] * inv[h]).astype(BF16), vcat[h // R], preferred_element_type=F32)
        o_ref[:, h * d:(h + 1) * d] = o.astype(o_ref.dtype)


def window_attention(zm, tab, qn_g, kn_g, sink, B, T, Lc, local):
    N = B * T
    Bk = SWA_BLOCK
    Q, KV = SWA_HEADS * SWA_HD, SWA_KV_HEADS * SWA_HD
    n_q = T if local else Lc
    nb = n_q // Bk
    row0 = 0 if local else N // Bk
    sink_b = jnp.broadcast_to(sink.astype(F32)[:, None], (SWA_HEADS, LANES))
    q_spec = pl.BlockSpec((Bk, Q), lambda b, n: (row0 + b * nb + n, MIX_SQ // Q))
    kx_spec = pl.BlockSpec((Lc, KV), lambda b, n: (N // Lc + b, MIX_SK // KV))
    vx_spec = pl.BlockSpec((Lc, KV), lambda b, n: (N // Lc + b, MIX_SV // KV))
    g_spec = pl.BlockSpec((1, SWA_HD), lambda b, n: (0, 0))
    s_spec = pl.BlockSpec((SWA_HEADS, LANES), lambda b, n: (0, 0))
    gains = (qn_g.reshape(1, SWA_HD).astype(F32) * SWA_SCALE, kn_g.reshape(1, SWA_HD).astype(F32), sink_b)
    if local:
        pv = lambda n: jnp.maximum(n - 1, 0)
        nx = lambda n: jnp.minimum(n + 1, nb - 1)
        kv_spec = lambda f, col: pl.BlockSpec((Bk, KV), lambda b, n: (b * nb + f(n), col // KV))
        t_spec = lambda f: pl.BlockSpec((Bk, 3 * LANES), lambda b, n: (f(n), 0))
        same = lambda n: n
        in_specs = [q_spec, kv_spec(pv, MIX_SK), kv_spec(same, MIX_SK), kv_spec(nx, MIX_SK),
                    kv_spec(pv, MIX_SV), kv_spec(same, MIX_SV), kv_spec(nx, MIX_SV), kx_spec, vx_spec,
                    t_spec(pv), t_spec(same), t_spec(nx), g_spec, g_spec, s_spec]
        args = (zm,) * 9 + (tab,) * 3 + gains
    else:
        in_specs = [q_spec, kx_spec, vx_spec, g_spec, g_spec, s_spec]
        args = (zm,) * 3 + gains
    kern = functools.partial(_swa_kernel, local=local, n_blocks=nb)
    return pl.pallas_call(kern, grid=(B, nb), in_specs=in_specs,
                          out_specs=pl.BlockSpec((Bk, Q), lambda b, n: (b * nb + n, 0)),
                          out_shape=jax.ShapeDtypeStruct((B * n_q, Q), BF16),
                          compiler_params=_params("parallel", "arbitrary"))(*args)


def _split3(x):
    hi = x.astype(BF16)
    r1 = x - hi.astype(F32)
    mid = r1.astype(BF16)
    lo = (r1 - mid.astype(F32)).astype(BF16)
    return hi, mid, lo


def _dot_bf16(a, b):
    return jnp.dot(a.astype(BF16), b.astype(BF16), preferred_element_type=F32)


def _dot_nt(a, b):
    return lax.dot_general(a.astype(BF16), b.astype(BF16), (((1,), (1,)), ((), ())), preferred_element_type=F32)


def _gdn_prep_kernel(zp_ref, zc_ref, zn_ref, zs_ref, cw_ref, ad_ref, wq_ref, u_ref, qk_ref, kd_ref, tot_ref,
                     ext_scr, qkv_scr, *, n_ctx, n_chunks):
    C, H, dk, dv = GDN_CHUNK, GDN_HEADS, GDN_DK, GDN_DV
    P = 2 * C
    nk = H * dk
    halo = 8
    pad = (GDN_CONV - 1) // 2

    ch = pl.program_id(1)
    has_prev = jnp.where((ch == 0) | (ch == n_ctx), 0.0, 1.0)
    has_next = jnp.where((ch == n_ctx - 1) | (ch == n_chunks - 1), 0.0, 1.0)
    ext_scr[0:halo, :] = zp_ref[C - halo:C, :] * has_prev
    ext_scr[halo:halo + C, :] = zc_ref[...]
    ext_scr[halo + C:, :] = zn_ref[0:halo, :] * has_next
    for part in range(3):
        cs = slice(part * nk, (part + 1) * nk)
        y = sum(ext_scr[halo - pad + j:halo - pad + j + C, cs] * cw_ref[j:j + 1, cs] for j in range(GDN_CONV))
        qkv_scr[:, cs] = y * jax.nn.sigmoid(y)
    for h in range(H):
        for part, scale in ((0, dk ** -0.5), (1, 1.0)):
            cs = slice(part * nk + h * dk, part * nk + (h + 1) * dk)
            y = qkv_scr[:, cs]
            qkv_scr[:, cs] = y * (lax.rsqrt(jnp.sum(y * y, axis=-1, keepdims=True) + EPS) * scale)
    q_ref, k_ref, v_ref = qkv_scr.at[:, 0:nk], qkv_scr.at[:, nk:2 * nk], qkv_scr.at[:, 2 * nk:3 * nk]

    zs = zs_ref[...]
    beta = jax.nn.sigmoid(zs)
    a = pltpu.roll(zs, LANES - 2 * H, 1) + ad_ref[1:2, :]
    g = -ad_ref[0:1, :] * (jnp.maximum(a, 0.0) + jnp.log(1.0 + jnp.exp(-jnp.abs(a))))

    ii = lax.broadcasted_iota(jnp.int32, (C, C), 0)
    jj = lax.broadcasted_iota(jnp.int32, (C, C), 1)
    low = (ii >= jj).astype(BF16)
    upp = (ii <= jj).astype(BF16)
    r = lax.broadcasted_iota(jnp.int32, (P, P), 0)
    c = lax.broadcasted_iota(jnp.int32, (P, P), 1)
    rq = jnp.where(r < C, 0, 1)
    cq = jnp.where(c < C, 0, 1)
    ahead = (r - c) * (1 - 2 * rq)
    causal = (rq == cq) & (ahead >= 0)
    strict = (rq == cq) & (ahead > 0)
    eye = (r == c).astype(F32)
    row_fwd = lax.broadcasted_iota(jnp.int32, (P, 1), 0) < C

    g3 = _split3(g)
    tri = jnp.concatenate([low, upp], axis=0)
    dcol_all = sum(jnp.dot(tri, p, preferred_element_type=F32) for p in g3)
    tdot = lambda p, t: lax.dot_general(p, t, (((0,), (0,)), ((), ())), preferred_element_type=F32)
    drow_all = jnp.concatenate([sum(tdot(p, upp) for p in g3)[:H], sum(tdot(p, low) for p in g3)[H:2 * H]],
                               axis=1)
    beta2 = jnp.concatenate([beta, beta], axis=0)

    def pair_col(a, h):
        return jnp.where(row_fwd, a[:, h:h + 1], a[:, H + h:H + h + 1])

    stack = lambda ref, h, w: jnp.concatenate([ref[:, h * w:(h + 1) * w]] * 2, axis=0)

    group = 4
    for h0 in range(0, H, group):
        hs = range(h0, h0 + group)
        dc = [pair_col(dcol_all, h) for h in hs]
        seg = [jnp.exp(jnp.where(causal, dc[a] - drow_all[h:h + 1, :], NEG_BIG)) for a, h in enumerate(hs)]
        b2 = [pair_col(beta2, h) for h in hs]
        kk = [_dot_nt(stack(k_ref, h, dk) * b2[a], stack(k_ref, h, dk)) for a, h in enumerate(hs)]
        pw = [jnp.where(strict, kk[a] * seg[a], 0.0) for a in range(group)]
        inv = [eye - m for m in pw]
        k = 2
        while k < C:
            pw = [_dot_bf16(m, m) for m in pw]
            inv = [t + _dot_bf16(t, m) for t, m in zip(inv, pw)]
            k *= 2
        for a, h in enumerate(hs):
            k2, q2, v2 = stack(k_ref, h, dk), stack(q_ref, h, dk), stack(v_ref, h, dv)
            ecol = jnp.exp(dc[a])
            rhs = jnp.concatenate([k2 * (b2[a] * ecol), v2 * b2[a]], axis=1)
            sol = rhs + _dot_bf16(inv[a] - eye, rhs)
            qk = _dot_nt(q2, k2) * seg[a]
            dlast = jnp.where(row_fwd, dc[a][C - 1:C], dc[a][C:C + 1])
            qd = q2 * ecol
            kd = k2 * jnp.exp(dlast - dc[a])
            tot = jnp.exp(dlast)
            for d, rs in enumerate((slice(0, C), slice(C, P))):
                hd = d * H + h
                wq_ref[hd, :C, :] = sol[rs, :dk].astype(wq_ref.dtype)
                wq_ref[hd, C:, :] = qd[rs].astype(wq_ref.dtype)
                u_ref[hd] = sol[rs, dk:]
                qk_ref[hd] = qk[rs, d * C:(d + 1) * C].astype(qk_ref.dtype)
                kd_ref[hd] = kd[rs].astype(kd_ref.dtype)
                tot_ref[hd] = jnp.broadcast_to(tot[d * C:d * C + 1], (1, LANES))


def gdn_prepare_chunks(zg, zs, conv_w, a_log, dt_bias, B, T, Lc):
    C, H = GDN_CHUNK, GDN_HEADS
    N = B * T
    n_ctx, n = Lc // C, (T + Lc) // C
    width = 3 * H * GDN_DK
    last = zg.shape[0] // C - 1

    def row_block(b, c):
        return jnp.where(c < n_ctx, (N + b * Lc) // C + c, (b * T) // C + c - n_ctx)

    z_spec = lambda d: pl.BlockSpec((C, width), lambda b, c: (jnp.clip(row_block(b, c) + d, 0, last), 0))
    lanes = lambda a: jnp.pad(a.astype(F32), (0, LANES - 2 * H))
    ad = jnp.zeros((8, LANES), F32).at[0].set(lanes(jnp.exp(a_log.astype(F32)))).at[1].set(lanes(dt_bias))
    per = lambda r, w: pl.BlockSpec((None, None, 2 * H, r, w), lambda b, c: (b, c, 0, 0, 0))
    shp = lambda r, w, dt: jax.ShapeDtypeStruct((B, n, 2 * H, r, w), dt)
    return pl.pallas_call(
        functools.partial(_gdn_prep_kernel, n_ctx=n_ctx, n_chunks=n), grid=(B, n),
        in_specs=[z_spec(-1), z_spec(0), z_spec(1),
                  pl.BlockSpec((C, LANES), lambda b, c: (row_block(b, c), 0)),
                  pl.BlockSpec((GDN_CONV, width), lambda b, c: (0, 0)),
                  pl.BlockSpec((8, LANES), lambda b, c: (0, 0))],
        out_specs=[per(2 * C, GDN_DK), per(C, GDN_DV), per(C, C), per(C, GDN_DK), per(1, LANES)],
        out_shape=[shp(2 * C, GDN_DK, BF16), shp(C, GDN_DV, F32), shp(C, C, BF16), shp(C, GDN_DK, BF16),
                   shp(1, LANES, F32)],
        scratch_shapes=[pltpu.VMEM((C + 16, width), F32), pltpu.VMEM((C, width), F32)],
        compiler_params=_params("parallel", "parallel"))(
            zg, zg, zg, zs, conv_w.reshape(GDN_CONV, width).astype(F32), ad)


def _gdn_scan_kernel(wqf, uf, qkf, kdf, totf, wqb, ub, qkb, kdb, totb, of_ref, ob_ref, s_ref):
    C, H, dv = GDN_CHUNK, GDN_HEADS, GDN_DV

    @pl.when(pl.program_id(1) == 0)
    def _():
        s_ref[...] = jnp.zeros_like(s_ref)

    for d, (wq, u, qk, kd, tot, o_ref) in enumerate(((wqf, uf, qkf, kdf, totf, of_ref),
                                                     (wqb, ub, qkb, kdb, totb, ob_ref))):
        hs = range(H)
        ws = [jnp.dot(wq[h], s_ref[d * H + h].astype(BF16), preferred_element_type=F32) for h in hs]
        v_new = [(u[h] - ws[h][:C]).astype(BF16) for h in hs]
        for h in hs:
            o_ref[:, h * dv:(h + 1) * dv] = ws[h][C:] + jnp.dot(qk[h], v_new[h], preferred_element_type=F32)
        upd = [lax.dot_general(kd[h], v_new[h], (((0,), (0,)), ((), ())), preferred_element_type=F32) for h in hs]
        for h in hs:
            s_ref[d * H + h] = s_ref[d * H + h] * tot[h] + upd[h]


def gdn_scan(wq, u, qk, kd, tot, n_ctx, T, Lc):
    B, n = wq.shape[0], wq.shape[1]
    C, H = GDN_CHUNK, GDN_HEADS
    N = B * T

    def bwd(s):
        return jnp.where(s < n_ctx, n_ctx - 1 - s, n - 1 - (s - n_ctx))

    def row_block(b, c):
        return jnp.where(c < n_ctx, (N + b * Lc) // C + c, (b * T) // C + c - n_ctx)

    fw = lambda r, w: pl.BlockSpec((None, None, H, r, w), lambda b, s: (b, s, 0, 0, 0))
    bw = lambda r, w: pl.BlockSpec((None, None, H, r, w), lambda b, s: (b, bwd(s), 1, 0, 0))
    shapes = ((2 * C, GDN_DK), (C, GDN_DV), (C, C), (C, GDN_DK), (1, LANES))
    o_shape = jax.ShapeDtypeStruct((B * n * C, H * GDN_DV), F32)
    return pl.pallas_call(
        _gdn_scan_kernel, grid=(B, n),
        in_specs=[fw(*s) for s in shapes] + [bw(*s) for s in shapes],
        out_specs=[pl.BlockSpec((C, H * GDN_DV), lambda b, s: (row_block(b, s), 0)),
                   pl.BlockSpec((C, H * GDN_DV), lambda b, s: (row_block(b, bwd(s)), 0))],
        out_shape=[o_shape, o_shape],
        scratch_shapes=[pltpu.VMEM((2 * H, GDN_DK, GDN_DV), F32)],
        compiler_params=_params("parallel", "arbitrary"))(wq, u, qk, kd, tot, wq, u, qk, kd, tot)


def kernel(x, c, ctx, c_ctx, w_mod, b_mod, norm_mix_g, norm_ffn_g, w_in, gdn_conv_w, gdn_a_log, gdn_dt_bias,
           gdn_norm_g, mla_q_norm_g, mla_kv_norm_g, mla_w_uq, mla_w_ukv, mla_qn_g, mla_kn_g, swa_qn_g, swa_kn_g,
           swa_sink, w_branch_a, w_branch_b, w_branch_c, w_out, ffn_w_gate, ffn_w_up, ffn_w_down, moe_router,
           moe_router_bias, moe_w_gate, moe_w_up, moe_w_down):
    B, T, D = x.shape
    Lc = ctx.shape[1]
    depth = w_mod.shape[0]
    N, Nc = B * T, B * Lc
    M = N + Nc
    TM = _pow2_tile(MAX_ROW_TILE, T, Nc)
    gran = min(TM, 256)
    tseq = _pow2_tile(256, T, Lc)
    assert T % Lc == 0 and Lc % SWA_BLOCK == 0 and Lc % GDN_CHUNK == 0
    H = GDN_HEADS
    nk = H * GDN_DK

    sizes = (nk, nk, H * GDN_DV, H * GDN_DV, 2 * H, 2 * H, MLA_Q_RANK, MLA_KV_RANK, MLA_ROPE,
             SWA_HEADS * SWA_HD, SWA_KV_HEADS * SWA_HD, SWA_KV_HEADS * SWA_HD, N_BRANCH * D)
    off = np.concatenate([[0], np.cumsum(sizes)])
    n_gdn = int(off[4])
    tn_gdn = _pow2_tile(512, n_gdn)
    tn_mix = 512
    tn_gate = _pow2_tile(512, D)

    rows_all = jnp.concatenate([x.reshape(N, D), ctx.reshape(Nc, D)], axis=0)
    tile_batch = np.minimum(np.arange(M // gran) * gran // T, B)
    tile_batch = np.where(np.arange(M // gran) * gran < N, tile_batch, B)
    cvec = jnp.concatenate([c, c_ctx[None, :], jnp.zeros((MOD_ROWS - B - 1, D), F32)], axis=0)
    cvec = jax.nn.silu(cvec)
    tab_mla = _rope_tables(T, Lc, MLA_ROPE)
    tab_swa = _rope_tables(T, 0, SWA_HD)
    n_ctx_chunks = Lc // GDN_CHUNK
    S = Lc + T

    for l in range(depth):
        need_ctx = l < depth - 1
        rows_out = M if need_ctx else N

        mod = matmul(cvec, w_mod, w_lead=l, tm=MOD_ROWS, tn=_pow2_tile(1024, 6 * D), out_dtype=F32,
                     epilogue="bias", bias=b_mod[l][None, :])
        mod = mod.reshape(MOD_ROWS, 6, D)[:B + 1]
        mod = jnp.pad(mod, ((0, 0), (0, MOD_ROWS - 6), (0, 0)))
        modt = mod[tile_batch]

        h = modulate(rows_all, norm_mix_g[l], modt, M, gran, shift_row=0, scale_row=1)
        w_l = w_in[l]
        seg = lambda a, b: w_l[:, off[a]:off[b]]
        zcols = lambda n: jnp.zeros((D, n), w_l.dtype)
        w_small = jnp.concatenate([seg(4, 6), zcols(LANES - 4 * H)], axis=1)
        w_mix = jnp.concatenate([seg(6, 7), seg(10, 11), seg(9, 10), seg(7, 8), seg(11, 12), seg(8, 9),
                                 zcols(MIX_WIDTH - MIX_KR - MLA_ROPE)], axis=1)
        zg = matmul(h, w_in, w_lead=l, ncols=n_gdn, tm=TM, tn=tn_gdn, out_dtype=F32)
        zs = matmul(h, w_small, tm=TM, tn=LANES, out_dtype=F32)
        zm = matmul(h, w_mix, tm=TM, tn=tn_mix, out_dtype=F32)

        wq_c, u_c, qk_c, kd_c, tot_c = gdn_prepare_chunks(zg, zs, gdn_conv_w[l], gdn_a_log[l], gdn_dt_bias[l], B, T, Lc)
        o_f, o_b2 = gdn_scan(wq_c, u_c, qk_c, kd_c, tot_c, n_ctx_chunks, T, Lc)

        mq = mla_project_q(zm, rows_out, mla_q_norm_g[l], mla_w_uq[l], mla_qn_g[l], tab_mla, B, T, Lc, tseq)
        mk, mv = mla_project_kv(zm, mla_kv_norm_g[l], mla_w_ukv[l], mla_kn_g[l], tab_mla, B, T, Lc, tseq)
        o_b = full_attention(mq, mk, mv, tseq, T, 0, S, 0)
        o_c = window_attention(zm, tab_swa, swa_qn_g[l], swa_kn_g[l], swa_sink[l], B, T, Lc, local=True)
        if need_ctx:
            o_b = jnp.concatenate([o_b, full_attention(mq, mk, mv, _pow2_tile(256, Lc), Lc, T, Lc, T // Lc)], axis=0)
            o_c = jnp.concatenate([o_c, window_attention(zm, None, swa_qn_g[l], swa_kn_g[l], swa_sink[l], B, T, Lc,
                                                         local=False)], axis=0)

        merged = merge_branches(h, seg(12, 13), o_f, o_b2, zg, 3 * nk, gdn_norm_g[l], o_b, o_c, w_branch_a, w_branch_b,
                                w_branch_c, l, rows_out, TM // 2, tn_gate)
        rows_new = matmul(merged, w_out, w_lead=l, tm=TM, tn=tn_gate, out_dtype=F32, epilogue="residual",
                          resid=rows_all, modt=modt, gran=gran, gate_row=2)

        i = l // 2
        if l % 2 == 0:
            h2 = modulate(rows_new, norm_ffn_g[l], modt, rows_out, gran, shift_row=3, scale_row=4)
            nblk = rows_out // TM
            F = ffn_w_gate.shape[-1]
            y = swiglu_grouped(h2, ffn_w_gate[:, None], ffn_w_up[:, None], ffn_w_down[:, None], i,
                               jnp.zeros((nblk,), jnp.int32), jnp.full((nblk,), 2, jnp.int32),
                               jnp.ones((rows_out, 1), F32), rows_out, TM, _pow2_tile(256, F))
        else:
            E = moe_w_gate.shape[1]
            rw = jnp.pad(moe_router[i], ((0, 0), (0, LANES - E)))
            rb = jnp.pad(moe_router_bias[i].astype(F32), (0, LANES - E))[None, :]
            h2, logits = modulate(rows_new, norm_ffn_g[l], modt, rows_out, gran, shift_row=3, scale_row=4,
                                  router=(rw, rb))
            top_logit, top_idx = lax.top_k(logits[:, :E], TOP_K)
            top_w = jax.nn.softmax(top_logit, axis=-1)
            flat_e = top_idx.reshape(-1)
            onehot = (flat_e[:, None] == jnp.arange(E)[None, :]).astype(jnp.int32)
            rank = jnp.take_along_axis(jnp.cumsum(onehot, axis=0) - onehot, flat_e[:, None], axis=1)[:, 0]
            counts = jnp.sum(onehot, axis=0)
            padded = (counts + TM - 1) // TM * TM
            pstart = jnp.cumsum(padded) - padded
            dest = pstart[flat_e] + rank
            nblk = -(-(rows_out * TOP_K) // TM) + E
            slots = nblk * TM
            src = jnp.zeros((slots,), jnp.int32).at[dest].set(jnp.arange(rows_out * TOP_K, dtype=jnp.int32) // TOP_K)
            w_slot = jnp.zeros((slots,), F32).at[dest].set(top_w.reshape(-1))
            blk_start = jnp.arange(nblk, dtype=jnp.int32) * TM
            ends = jnp.cumsum(padded)
            block_e = jnp.minimum(jnp.searchsorted(ends, blk_start, side="right"), E - 1).astype(jnp.int32)
            used = blk_start < ends[-1]
            last_e = block_e[jnp.maximum(jnp.sum(used.astype(jnp.int32)) - 1, 0)]
            block_e = jnp.where(used, block_e, last_e)
            rows_in_block = jnp.clip((pstart + counts)[block_e] - blk_start, 0, TM)
            half = TM // 2
            block_valid = jnp.where(used, (rows_in_block + half - 1) // half, 0).astype(jnp.int32)
            F = moe_w_gate.shape[-1]
            ys = swiglu_grouped(h2[src], moe_w_gate, moe_w_up, moe_w_down, i, block_e, block_valid,
                                w_slot[:, None], slots, TM, _pow2_tile(256, F))
            dest2 = dest.reshape(rows_out, TOP_K)
            y = ys[dest2[:, 0]] + ys[dest2[:, 1]]
        gate_f = modt[:, 5][:rows_out // gran]
        rows_ffn = (rows_new.reshape(rows_out // gran, gran, D) + gate_f[:, None, :] * y.reshape(rows_out // gran, gran, D))
        rows_all = rows_ffn.reshape(rows_out, D)

    return rows_all[:N].reshape(B, T, D)
```

```python
import functools

import jax
import jax.numpy as jnp
import numpy as np
from jax import lax
from jax.experimental import pallas as pl
from jax.experimental.pallas import tpu as pltpu

F32 = jnp.float32
BF16 = jnp.bfloat16

GRID_W = 64
EPS = 1e-6
ROPE_BASE = 10000.0
N_BRANCH = 3
GDN_HEADS = 8
GDN_DK = 128
GDN_DV = 128
GDN_CONV = 5
GDN_CHUNK = 64
MLA_HEADS = 8
MLA_Q_RANK = 768
MLA_KV_RANK = 512
MLA_NOPE = 128
MLA_ROPE = 64
MLA_V = 128
MLA_QK = MLA_NOPE + MLA_ROPE
MLA_SCALE = MLA_QK ** -0.5
SWA_HEADS = 8
SWA_KV_HEADS = 2
SWA_HD = 128
SWA_WINDOW = 128
SWA_BLOCK = 128
SWA_SCALE = SWA_HD ** -0.5
N_EXPERTS = 8
TOP_K = 2

LANES = 128
VMEM_LIMIT_BYTES = 56 * 1024 * 1024
MAX_ROW_TILE = 1024
MOD_ROWS = 8
NEG_BIG = -1e30
LOG2E = 1.4426950408889634
ATTN_ROW_GROUP = 256
MLA_QK_PAD = 2 * LANES

MIX_CQ, MIX_SK, MIX_SQ, MIX_CKV, MIX_SV, MIX_KR, MIX_WIDTH = 0, 768, 1024, 2048, 2560, 2816, 3072


def _params(*sem):
    return pltpu.CompilerParams(dimension_semantics=sem, vmem_limit_bytes=VMEM_LIMIT_BYTES)


def _pow2_tile(limit, *dims):
    t = 1
    while t * 2 <= limit and all(d % (t * 2) == 0 for d in dims):
        t *= 2
    return t


def _rms_rows(x, g):
    return x * lax.rsqrt(jnp.mean(x * x, axis=-1, keepdims=True) + EPS) * g


def _modulate_kernel(x_ref, g_ref, mod_ref, *rest, shift_row, scale_row, with_router):
    mod = mod_ref[0]
    h = _rms_rows(x_ref[...], g_ref[...]) * (1.0 + mod[scale_row:scale_row + 1]) + mod[shift_row:shift_row + 1]
    if with_router:
        rw_ref, rb_ref, h_ref, lg_ref = rest
        lg_ref[...] = jnp.dot(h.astype(BF16), rw_ref[...].astype(BF16), preferred_element_type=F32) + rb_ref[...]
    else:
        (h_ref,) = rest
    h_ref[...] = h.astype(h_ref.dtype)


def modulate(x, gain, modt, rows, gran, shift_row, scale_row, router=None):
    D = x.shape[1]
    tm = gran
    kern = functools.partial(_modulate_kernel, shift_row=shift_row, scale_row=scale_row,
                             with_router=router is not None)
    in_specs = [pl.BlockSpec((tm, D), lambda i: (i, 0)),
                pl.BlockSpec((1, D), lambda i: (0, 0)),
                pl.BlockSpec((1, MOD_ROWS, D), lambda i: (i, 0, 0))]
    args = [x, gain.reshape(1, D), modt]
    out_shape = [jax.ShapeDtypeStruct((rows, D), BF16)]
    out_specs = [pl.BlockSpec((tm, D), lambda i: (i, 0))]
    if router is not None:
        rw, rb = router
        in_specs += [pl.BlockSpec((D, LANES), lambda i: (0, 0)), pl.BlockSpec((1, LANES), lambda i: (0, 0))]
        args += [rw, rb]
        out_shape.append(jax.ShapeDtypeStruct((rows, LANES), F32))
        out_specs.append(pl.BlockSpec((tm, LANES), lambda i: (i, 0)))
    out = pl.pallas_call(kern, grid=(rows // tm,), in_specs=in_specs, out_specs=out_specs,
                         out_shape=out_shape, compiler_params=_params("parallel"))(*args)
    return out if router is not None else out[0]


def _mm_kernel(a_ref, w_ref, *rest, epilogue, gate_row):
    acc = jnp.dot(a_ref[...].astype(BF16), w_ref[...].astype(BF16), preferred_element_type=F32)
    if epilogue == "bias":
        b_ref, o_ref = rest
        acc = acc + b_ref[...]
    elif epilogue == "sigmoid":
        (o_ref,) = rest
        acc = jax.nn.sigmoid(acc)
    elif epilogue == "residual":
        x_ref, mod_ref, o_ref = rest
        acc = x_ref[...] + mod_ref[0][gate_row:gate_row + 1] * acc
    else:
        (o_ref,) = rest
    o_ref[...] = acc.astype(o_ref.dtype)


def matmul(a, w, *, rows=None, w_lead=None, col0=0, ncols=None, tm, tn, out_dtype, epilogue=None,
           bias=None, resid=None, modt=None, gran=None, gate_row=0):
    rows = a.shape[0] if rows is None else rows
    K = a.shape[1]
    ncols = w.shape[-1] if ncols is None else ncols
    assert rows % tm == 0 and ncols % tn == 0 and col0 % tn == 0
    cb = col0 // tn
    if w.ndim == 3:
        w_spec = pl.BlockSpec((None, K, tn), lambda i, j: (w_lead, 0, cb + j))
    else:
        w_spec = pl.BlockSpec((K, tn), lambda i, j: (0, cb + j))
    in_specs = [pl.BlockSpec((tm, K), lambda i, j: (i, 0)), w_spec]
    args = [a, w]
    if epilogue == "bias":
        in_specs.append(pl.BlockSpec((1, tn), lambda i, j: (0, j)))
        args.append(bias)
    elif epilogue == "residual":
        step = tm // gran
        in_specs += [pl.BlockSpec((tm, tn), lambda i, j: (i, j)),
                     pl.BlockSpec((1, MOD_ROWS, tn), lambda i, j: (i * step, 0, j))]
        args += [resid, modt]
    kern = functools.partial(_mm_kernel, epilogue=epilogue, gate_row=gate_row)
    return pl.pallas_call(kern, grid=(rows // tm, ncols // tn), in_specs=in_specs,
                          out_specs=pl.BlockSpec((tm, tn), lambda i, j: (i, j)),
                          out_shape=jax.ShapeDtypeStruct((rows, ncols), out_dtype),
                          compiler_params=_params("parallel", "arbitrary"))(*args)


def _merge_kernel(h_ref, of_ref, ob2_ref, og_ref, gn_ref, ob_ref, oc_ref, wga_ref, wgb_ref, wgc_ref,
                  wa_ref, wb_ref, wc_ref, o_ref, oa_scr):
    @pl.when(pl.program_id(1) == 0)
    def _():
        for h in range(GDN_HEADS):
            cs = slice(h * GDN_DV, (h + 1) * GDN_DV)
            o = _rms_rows(of_ref[:, cs] + ob2_ref[:, cs], gn_ref[...])
            oa_scr[:, cs] = (o * jax.nn.silu(og_ref[:, cs])).astype(BF16)

    hm = h_ref[...]
    gate = lambda w_ref: jax.nn.sigmoid(jnp.dot(hm, w_ref[...], preferred_element_type=F32))
    acc = gate(wga_ref) * jnp.dot(oa_scr[...], wa_ref[...], preferred_element_type=F32)
    acc += gate(wgb_ref) * jnp.dot(ob_ref[...], wb_ref[...], preferred_element_type=F32)
    acc += gate(wgc_ref) * jnp.dot(oc_ref[...], wc_ref[...], preferred_element_type=F32)
    o_ref[...] = acc.astype(o_ref.dtype)


def merge_branches(h, w_gates, o_f, o_b2, zg, og_col, gdn_g, ob, oc, wa, wb, wc, l, rows, tm, tn):
    D = wa.shape[-1]
    nj = D // tn
    na = GDN_HEADS * GDN_DV
    row = lambda w, cb=0: pl.BlockSpec((tm, w), lambda i, j: (i, cb))
    g_spec = lambda k: pl.BlockSpec((D, tn), lambda i, j: (0, k * nj + j))
    wa, wb, wc = (w[l].astype(BF16) for w in (wa, wb, wc))
    w_gates = w_gates.astype(BF16)
    w_spec = lambda w: pl.BlockSpec((w.shape[0], tn), lambda i, j: (0, j))
    return pl.pallas_call(
        _merge_kernel, grid=(rows // tm, nj),
        in_specs=[row(D), row(na), row(na), row(na, og_col // na), pl.BlockSpec((1, GDN_DV), lambda i, j: (0, 0)),
                  row(ob.shape[1]), row(oc.shape[1]), g_spec(0), g_spec(1), g_spec(2),
                  w_spec(wa), w_spec(wb), w_spec(wc)],
        out_specs=pl.BlockSpec((tm, tn), lambda i, j: (i, j)),
        out_shape=jax.ShapeDtypeStruct((rows, D), BF16),
        scratch_shapes=[pltpu.VMEM((tm, na), BF16)],
        compiler_params=_params("parallel", "arbitrary"))(
            h, o_f, o_b2, zg, gdn_g.reshape(1, GDN_DV).astype(F32), ob, oc, w_gates, w_gates, w_gates, wa, wb, wc)


def _swiglu_kernel(be_ref, bv_ref, x_ref, wg_ref, wu_ref, wd_ref, rs_ref, o_ref):
    i, f = pl.program_id(0), pl.program_id(1)

    @pl.when(f == 0)
    def _():
        o_ref[...] = jnp.zeros_like(o_ref)

    def accumulate(n):
        x = x_ref[:n, :]
        g = jnp.dot(x, wg_ref[...].astype(BF16), preferred_element_type=F32)
        u = jnp.dot(x, wu_ref[...].astype(BF16), preferred_element_type=F32)
        h = (jax.nn.silu(g) * u).astype(BF16)
        o_ref[:n, :] += jnp.dot(h, wd_ref[...].astype(BF16), preferred_element_type=F32)

        @pl.when(f == pl.num_programs(1) - 1)
        def _():
            o_ref[:n, :] = o_ref[:n, :] * rs_ref[:n, :]

    tm = x_ref.shape[0]
    pl.when(bv_ref[i] == 2)(functools.partial(accumulate, tm))
    pl.when(bv_ref[i] == 1)(functools.partial(accumulate, tm // 2))


def swiglu_grouped(x, wg, wu, wd, l, block_e, block_valid, row_scale, rows, tm, tf):
    D = x.shape[1]
    F = wg.shape[-1]
    nf = F // tf
    assert rows % tm == 0 and F % tf == 0

    def f_idx(i, f, bv):
        return jnp.where(bv[i] > 0, f, nf - 1)

    grid_spec = pltpu.PrefetchScalarGridSpec(
        num_scalar_prefetch=2, grid=(rows // tm, nf),
        in_specs=[pl.BlockSpec((tm, D), lambda i, f, be, bv: (i, 0)),
                  pl.BlockSpec((None, None, D, tf), lambda i, f, be, bv: (l, be[i], 0, f_idx(i, f, bv))),
                  pl.BlockSpec((None, None, D, tf), lambda i, f, be, bv: (l, be[i], 0, f_idx(i, f, bv))),
                  pl.BlockSpec((None, None, tf, D), lambda i, f, be, bv: (l, be[i], f_idx(i, f, bv), 0)),
                  pl.BlockSpec((tm, 1), lambda i, f, be, bv: (i, 0))],
        out_specs=pl.BlockSpec((tm, D), lambda i, f, be, bv: (i, 0)))
    return pl.pallas_call(_swiglu_kernel, grid_spec=grid_spec,
                          out_shape=jax.ShapeDtypeStruct((rows, D), F32),
                          compiler_params=_params("parallel", "arbitrary"))(
                              block_e, block_valid, x, wg, wu, wd, row_scale)


def _rope_tables(T, extra, rot_dim):
    n = rot_dim // 4
    t = jnp.arange(T)
    inv = jnp.power(ROPE_BASE, -jnp.arange(n, dtype=F32) / n)
    ar = (t // GRID_W).astype(F32)[:, None] * inv
    ac = (t % GRID_W).astype(F32)[:, None] * inv
    z = jnp.zeros((T, n), F32)
    pad = lambda a, fill: jnp.concatenate([a, jnp.full((T, LANES - 4 * n), fill, F32)], axis=1)
    c = pad(jnp.concatenate([jnp.cos(ar), jnp.cos(ar), jnp.cos(ac), jnp.cos(ac)], axis=1), 1.0)
    a = pad(jnp.concatenate([-jnp.sin(ar), z, -jnp.sin(ac), z], axis=1), 0.0)
    b = pad(jnp.concatenate([z, jnp.sin(ar), z, jnp.sin(ac)], axis=1), 0.0)
    tab = jnp.concatenate([c, a, b], axis=1)
    ident = jnp.concatenate([jnp.ones((extra, LANES), F32), jnp.zeros((extra, 2 * LANES), F32)], axis=1)
    return jnp.concatenate([tab, ident], axis=0)


def _apply_rope(x, tab, half):
    return (x * tab[:, :LANES] + pltpu.roll(x, LANES - half, 1) * tab[:, LANES:2 * LANES]
            + pltpu.roll(x, half, 1) * tab[:, 2 * LANES:])


def _mla_q_kernel(c_ref, g_ref, w_ref, hg_ref, tab_ref, q_ref):
    a = _rms_rows(c_ref[...], g_ref[...]).astype(BF16)
    tab, hg = tab_ref[...], hg_ref[...]
    for h in range(MLA_HEADS):
        acc = jnp.dot(a, w_ref[:, h * MLA_QK_PAD:(h + 1) * MLA_QK_PAD], preferred_element_type=F32)
        y = acc * lax.rsqrt(jnp.sum(acc * acc, axis=-1, keepdims=True) * (1.0 / MLA_QK) + EPS) * hg
        q_ref[h, :, :LANES] = y[:, :LANES].astype(q_ref.dtype)
        q_ref[h, :, LANES:] = _apply_rope(y[:, LANES:], tab, MLA_ROPE // 4).astype(q_ref.dtype)


def _mla_kv_kernel(c_ref, kr_ref, g_ref, w_ref, hg_ref, tab_ref, k_ref, v_ref):
    a = _rms_rows(c_ref[...], g_ref[...]).astype(BF16)
    tab, hg = tab_ref[...], hg_ref[...]
    kr = kr_ref[...]
    kr_ss = jnp.sum(kr * kr, axis=-1, keepdims=True)
    width = MLA_NOPE + MLA_V
    for h in range(MLA_HEADS):
        acc = jnp.dot(a, w_ref[:, h * width:(h + 1) * width], preferred_element_type=F32)
        kn = acc[:, :LANES]
        r = lax.rsqrt((jnp.sum(kn * kn, axis=-1, keepdims=True) + kr_ss) * (1.0 / MLA_QK) + EPS)
        k_ref[h, :, :LANES] = (kn * r * hg[:, :LANES]).astype(k_ref.dtype)
        k_ref[h, :, LANES:] = _apply_rope(kr * r * hg[:, LANES:], tab, MLA_ROPE // 4).astype(k_ref.dtype)
        v_ref[h] = acc[:, LANES:].astype(v_ref.dtype)


def _seq_pos(i, tm, T, Lc, N):
    nl, nc, nlt = T // tm, Lc // tm, N // tm
    k = i - nlt
    return jnp.where(i < nlt, i // nl, k // nc), jnp.where(i < nlt, i % nl, nl + k % nc)


def mla_project_q(zm, rows, q_norm_g, w_uq_l, qn_g, tab, B, T, Lc, tm):
    N = B * T
    Lq = T + (Lc if rows > N else 0)
    H, R = MLA_HEADS, MLA_Q_RANK
    wp = jnp.pad(w_uq_l.reshape(R, H, MLA_QK), ((0, 0), (0, 0), (0, MLA_QK_PAD - MLA_QK)))
    wp = wp.reshape(R, H * MLA_QK_PAD).astype(BF16)
    hg = jnp.pad(qn_g.astype(F32) * (MLA_SCALE * LOG2E), (0, MLA_QK_PAD - MLA_QK))[None, :]
    pos = lambda i: _seq_pos(i, tm, T, Lc, N)
    return pl.pallas_call(
        _mla_q_kernel, grid=(rows // tm,),
        in_specs=[pl.BlockSpec((tm, R), lambda i: (i, MIX_CQ // R)),
                  pl.BlockSpec((1, R), lambda i: (0, 0)),
                  pl.BlockSpec((R, H * MLA_QK_PAD), lambda i: (0, 0)),
                  pl.BlockSpec((1, MLA_QK_PAD), lambda i: (0, 0)),
                  pl.BlockSpec((tm, 3 * LANES), lambda i: (pos(i)[1], 0))],
        out_specs=pl.BlockSpec((None, H, tm, MLA_QK_PAD), lambda i: (pos(i)[0], 0, pos(i)[1], 0)),
        out_shape=jax.ShapeDtypeStruct((B, H, Lq, MLA_QK_PAD), BF16),
        compiler_params=_params("parallel"))(zm, q_norm_g.reshape(1, R).astype(F32), wp, hg, tab)


def mla_project_kv(zm, kv_norm_g, w_ukv_l, kn_g, tab, B, T, Lc, tm):
    M = zm.shape[0]
    N = B * T
    H, R = MLA_HEADS, MLA_KV_RANK
    hg = jnp.pad(kn_g.astype(F32), (0, MLA_QK_PAD - MLA_QK))[None, :]
    pos = lambda i: _seq_pos(i, tm, T, Lc, N)
    o_spec = lambda w: pl.BlockSpec((None, H, tm, w), lambda i: (pos(i)[0], 0, pos(i)[1], 0))
    return pl.pallas_call(
        _mla_kv_kernel, grid=(M // tm,),
        in_specs=[pl.BlockSpec((tm, R), lambda i: (i, MIX_CKV // R)),
                  pl.BlockSpec((tm, LANES), lambda i: (i, MIX_KR // LANES)),
                  pl.BlockSpec((1, R), lambda i: (0, 0)),
                  pl.BlockSpec((R, H * (MLA_NOPE + MLA_V)), lambda i: (0, 0)),
                  pl.BlockSpec((1, MLA_QK_PAD), lambda i: (0, 0)),
                  pl.BlockSpec((tm, 3 * LANES), lambda i: (pos(i)[1], 0))],
        out_specs=[o_spec(MLA_QK_PAD), o_spec(MLA_V)],
        out_shape=[jax.ShapeDtypeStruct((B, H, T + Lc, MLA_QK_PAD), BF16),
                   jax.ShapeDtypeStruct((B, H, T + Lc, MLA_V), BF16)],
        compiler_params=_params("parallel"))(zm, zm, kv_norm_g.reshape(1, R).astype(F32), w_ukv_l.astype(BF16), hg, tab)


def _attn_kernel(q_ref, k_ref, v_ref, o_ref, *, sub):
    for r0 in range(0, q_ref.shape[0], sub):
        rows = slice(r0, r0 + sub)
        s = lax.dot_general(q_ref[rows, :], k_ref[...], (((1,), (1,)), ((), ())), preferred_element_type=F32)
        p = jnp.exp2(s - jnp.max(s, axis=-1, keepdims=True))
        den = jnp.sum(p, axis=-1, keepdims=True)
        o = jnp.dot(p.astype(BF16), v_ref[...], preferred_element_type=F32)
        o_ref[rows, :] = (o * (1.0 / den)).astype(o_ref.dtype)


def full_attention(q, k, v, tq, n_q, q_off, kl, k_blk):
    B, H, _, d = q.shape
    e = v.shape[3]
    nt = n_q // tq
    qo = q_off // tq
    return pl.pallas_call(
        functools.partial(_attn_kernel, sub=min(tq, ATTN_ROW_GROUP)), grid=(B, H, nt),
        in_specs=[pl.BlockSpec((None, None, tq, d), lambda b, h, i: (b, h, qo + i, 0)),
                  pl.BlockSpec((None, None, kl, d), lambda b, h, i: (b, h, k_blk, 0)),
                  pl.BlockSpec((None, None, kl, e), lambda b, h, i: (b, h, k_blk, 0))],
        out_specs=pl.BlockSpec((tq, e), lambda b, h, i: (b * nt + i, h)),
        out_shape=jax.ShapeDtypeStruct((B * n_q, H * e), BF16),
        compiler_params=_params("parallel", "parallel", "arbitrary"))(q, k, v)


def _swa_kernel(*refs, local, n_blocks):
    if local:
        (q_ref, kp_ref, kc_ref, kn_ref, vp_ref, vc_ref, vn_ref, kx_ref, vx_ref, tp_ref, tc_ref, tn_ref,
         qg_ref, kg_ref, sink_ref, o_ref) = refs
    else:
        q_ref, kx_ref, vx_ref, qg_ref, kg_ref, sink_ref, o_ref = refs
    n = pl.program_id(1)
    Bk, d = SWA_BLOCK, SWA_HD
    Lc = kx_ref.shape[0]
    R = SWA_HEADS // SWA_KV_HEADS
    half = SWA_HD // 4

    def prep(x, g, tab):
        y = _rms_rows(x, g)
        return y if tab is None else _apply_rope(y, tab, half)

    if local:
        iq = lax.broadcasted_iota(jnp.int32, (Bk, 3 * Bk), 0)
        jk = lax.broadcasted_iota(jnp.int32, (Bk, 3 * Bk), 1)
        valid = jnp.abs(iq + Bk - jk) <= SWA_WINDOW
        valid = valid & ((jk >= Bk) | (n > 0)) & ((jk < 2 * Bk) | (n < n_blocks - 1))
        bias = jnp.where(valid, 0.0, NEG_BIG).astype(F32)
        bias = jnp.concatenate([bias, jnp.zeros((Bk, Lc), F32)], axis=1)
        tp, tc, tn = tp_ref[...], tc_ref[...], tn_ref[...]
    else:
        tc = None
    kg, qg = kg_ref[...], qg_ref[...]
    kcat, vcat = [], []
    for g in range(SWA_KV_HEADS):
        cs = slice(g * d, (g + 1) * d)
        kx = prep(kx_ref[:, cs], kg, None).astype(BF16)
        if local:
            kcat.append(jnp.concatenate([prep(kp_ref[:, cs], kg, tp).astype(BF16), prep(kc_ref[:, cs], kg, tc).astype(BF16),
                                         prep(kn_ref[:, cs], kg, tn).astype(BF16), kx], axis=0))
            vcat.append(jnp.concatenate([vp_ref[:, cs].astype(BF16), vc_ref[:, cs].astype(BF16),
                                         vn_ref[:, cs].astype(BF16), vx_ref[:, cs].astype(BF16)], axis=0))
        else:
            kcat.append(kx)
            vcat.append(vx_ref[:, cs].astype(BF16))
    hs = range(SWA_HEADS)
    qh = [prep(q_ref[:, h * d:(h + 1) * d], qg, tc).astype(BF16) for h in hs]
    s = [lax.dot_general(qh[h], kcat[h // R], (((1,), (1,)), ((), ())), preferred_element_type=F32) for h in hs]
    if local:
        s = [sh + bias for sh in s]
    sink = [sink_ref[h:h + 1, 0:1] for h in hs]
    m = [jnp.maximum(jnp.max(s[h], axis=-1, keepdims=True), sink[h]) for h in hs]
    p = [jnp.exp2(s[h] - m[h]).astype(BF16) for h in hs]
    vone = [jnp.concatenate([v, jnp.ones_like(v)], axis=1) for v in vcat]
    for h in hs:
        o = jnp.dot(p[h], vone[h // R], preferred_element_type=F32)
        den = o[:, d:] + jnp.exp2(sink[h] - m[h])
        o_ref[:, h * d:(h + 1) * d] = (o[:, :d] * (1.0 / den)).astype(o_ref.dtype)


def window_attention(zm, tab, qn_g, kn_g, sink, B, T, Lc, local):
    N = B * T
    Bk = SWA_BLOCK
    Q, KV = SWA_HEADS * SWA_HD, SWA_KV_HEADS * SWA_HD
    n_q = T if local else Lc
    nb = n_q // Bk
    row0 = 0 if local else N // Bk
    sink_b = jnp.broadcast_to(sink.astype(F32)[:, None], (SWA_HEADS, LANES))
    q_spec = pl.BlockSpec((Bk, Q), lambda b, n: (row0 + b * nb + n, MIX_SQ // Q))
    kx_spec = pl.BlockSpec((Lc, KV), lambda b, n: (N // Lc + b, MIX_SK // KV))
    vx_spec = pl.BlockSpec((Lc, KV), lambda b, n: (N // Lc + b, MIX_SV // KV))
    g_spec = pl.BlockSpec((1, SWA_HD), lambda b, n: (0, 0))
    s_spec = pl.BlockSpec((SWA_HEADS, LANES), lambda b, n: (0, 0))
    gains = (qn_g.reshape(1, SWA_HD).astype(F32) * (SWA_SCALE * LOG2E), kn_g.reshape(1, SWA_HD).astype(F32),
             sink_b * LOG2E)
    if local:
        pv = lambda n: jnp.maximum(n - 1, 0)
        nx = lambda n: jnp.minimum(n + 1, nb - 1)
        kv_spec = lambda f, col: pl.BlockSpec((Bk, KV), lambda b, n: (b * nb + f(n), col // KV))
        t_spec = lambda f: pl.BlockSpec((Bk, 3 * LANES), lambda b, n: (f(n), 0))
        same = lambda n: n
        in_specs = [q_spec, kv_spec(pv, MIX_SK), kv_spec(same, MIX_SK), kv_spec(nx, MIX_SK),
                    kv_spec(pv, MIX_SV), kv_spec(same, MIX_SV), kv_spec(nx, MIX_SV), kx_spec, vx_spec,
                    t_spec(pv), t_spec(same), t_spec(nx), g_spec, g_spec, s_spec]
        args = (zm,) * 9 + (tab,) * 3 + gains
    else:
        in_specs = [q_spec, kx_spec, vx_spec, g_spec, g_spec, s_spec]
        args = (zm,) * 3 + gains
    kern = functools.partial(_swa_kernel, local=local, n_blocks=nb)
    return pl.pallas_call(kern, grid=(B, nb), in_specs=in_specs,
                          out_specs=pl.BlockSpec((Bk, Q), lambda b, n: (b * nb + n, 0)),
                          out_shape=jax.ShapeDtypeStruct((B * n_q, Q), BF16),
                          compiler_params=_params("parallel", "arbitrary"))(*args)


def _split3(x):
    hi = x.astype(BF16)
    r1 = x - hi.astype(F32)
    mid = r1.astype(BF16)
    lo = (r1 - mid.astype(F32)).astype(BF16)
    return hi, mid, lo


def _dot_bf16(a, b):
    return jnp.dot(a.astype(BF16), b.astype(BF16), preferred_element_type=F32)


def _dot_nt(a, b):
    return lax.dot_general(a.astype(BF16), b.astype(BF16), (((1,), (1,)), ((), ())), preferred_element_type=F32)


def _gdn_prep_kernel(zp_ref, zc_ref, zn_ref, zs_ref, cw_ref, ad_ref, wq_ref, u_ref, qk_ref, kd_ref, tot_ref,
                     ext_scr, qkv_scr, *, n_ctx, n_chunks):
    C, H, dk, dv = GDN_CHUNK, GDN_HEADS, GDN_DK, GDN_DV
    P = 2 * C
    nk = H * dk
    halo = 8
    pad = (GDN_CONV - 1) // 2

    ch = pl.program_id(1)
    has_prev = jnp.where((ch == 0) | (ch == n_ctx), 0.0, 1.0)
    has_next = jnp.where((ch == n_ctx - 1) | (ch == n_chunks - 1), 0.0, 1.0)
    ext_scr[0:halo, :] = zp_ref[C - halo:C, :] * has_prev
    ext_scr[halo:halo + C, :] = zc_ref[...]
    ext_scr[halo + C:, :] = zn_ref[0:halo, :] * has_next
    for part in range(3):
        cs = slice(part * nk, (part + 1) * nk)
        y = sum(ext_scr[halo - pad + j:halo - pad + j + C, cs] * cw_ref[j:j + 1, cs] for j in range(GDN_CONV))
        qkv_scr[:, cs] = y * jax.nn.sigmoid(y)
    for h in range(H):
        for part, scale in ((0, dk ** -0.5), (1, 1.0)):
            cs = slice(part * nk + h * dk, part * nk + (h + 1) * dk)
            y = qkv_scr[:, cs]
            qkv_scr[:, cs] = y * (lax.rsqrt(jnp.sum(y * y, axis=-1, keepdims=True) + EPS) * scale)
    q_ref, k_ref, v_ref = qkv_scr.at[:, 0:nk], qkv_scr.at[:, nk:2 * nk], qkv_scr.at[:, 2 * nk:3 * nk]

    zs = zs_ref[...]
    beta = jax.nn.sigmoid(zs)
    a = pltpu.roll(zs, LANES - 2 * H, 1) + ad_ref[1:2, :]
    g = -ad_ref[0:1, :] * (jnp.maximum(a, 0.0) + jnp.log(1.0 + jnp.exp(-jnp.abs(a))))

    ii = lax.broadcasted_iota(jnp.int32, (C, C), 0)
    jj = lax.broadcasted_iota(jnp.int32, (C, C), 1)
    low = (ii >= jj).astype(BF16)
    upp = (ii <= jj).astype(BF16)
    r = lax.broadcasted_iota(jnp.int32, (P, P), 0)
    c = lax.broadcasted_iota(jnp.int32, (P, P), 1)
    rq = jnp.where(r < C, 0, 1)
    cq = jnp.where(c < C, 0, 1)
    ahead = (r - c) * (1 - 2 * rq)
    causal = (rq == cq) & (ahead >= 0)
    strict = (rq == cq) & (ahead > 0)
    eye = (r == c).astype(F32)
    row_fwd = lax.broadcasted_iota(jnp.int32, (P, 1), 0) < C

    g3 = _split3(g)
    tri = jnp.concatenate([low, upp], axis=0)
    dcol_all = sum(jnp.dot(tri, p, preferred_element_type=F32) for p in g3)
    tdot = lambda p, t: lax.dot_general(p, t, (((0,), (0,)), ((), ())), preferred_element_type=F32)
    drow_all = jnp.concatenate([sum(tdot(p, upp) for p in g3)[:H], sum(tdot(p, low) for p in g3)[H:2 * H]],
                               axis=1)
    beta2 = jnp.concatenate([beta, beta], axis=0)

    def pair_col(a, h):
        return jnp.where(row_fwd, a[:, h:h + 1], a[:, H + h:H + h + 1])

    stack = lambda ref, h, w: jnp.concatenate([ref[:, h * w:(h + 1) * w]] * 2, axis=0)

    group = 4
    for h0 in range(0, H, group):
        hs = range(h0, h0 + group)
        dc = [pair_col(dcol_all, h) for h in hs]
        seg = [jnp.exp(jnp.where(causal, dc[a] - drow_all[h:h + 1, :], NEG_BIG)) for a, h in enumerate(hs)]
        b2 = [pair_col(beta2, h) for h in hs]
        kk = [_dot_nt(stack(k_ref, h, dk) * b2[a], stack(k_ref, h, dk)) for a, h in enumerate(hs)]
        pw = [jnp.where(strict, kk[a] * seg[a], 0.0) for a in range(group)]
        inv = [eye - m for m in pw]
        k = 2
        while k < C:
            pw = [_dot_bf16(m, m) for m in pw]
            inv = [t + _dot_bf16(t, m) for t, m in zip(inv, pw)]
            k *= 2
        for a, h in enumerate(hs):
            k2, q2, v2 = stack(k_ref, h, dk), stack(q_ref, h, dk), stack(v_ref, h, dv)
            ecol = jnp.exp(dc[a])
            rhs = jnp.concatenate([k2 * (b2[a] * ecol), v2 * b2[a]], axis=1)
            sol = rhs + _dot_bf16(inv[a] - eye, rhs)
            qk = _dot_nt(q2, k2) * seg[a]
            dlast = jnp.where(row_fwd, dc[a][C - 1:C], dc[a][C:C + 1])
            qd = q2 * ecol
            kd = k2 * jnp.exp(dlast - dc[a])
            tot = jnp.exp(dlast)
            for d, rs in enumerate((slice(0, C), slice(C, P))):
                hd = d * H + h
                wq_ref[hd, :C, :] = sol[rs, :dk].astype(wq_ref.dtype)
                wq_ref[hd, C:, :] = qd[rs].astype(wq_ref.dtype)
                u_ref[hd] = sol[rs, dk:]
                qk_ref[hd] = qk[rs, d * C:(d + 1) * C].astype(qk_ref.dtype)
                kd_ref[hd] = kd[rs].astype(kd_ref.dtype)
                tot_ref[hd] = jnp.broadcast_to(tot[d * C:d * C + 1], (1, LANES))


def gdn_prepare_chunks(zg, zs, conv_w, a_log, dt_bias, B, T, Lc):
    C, H = GDN_CHUNK, GDN_HEADS
    N = B * T
    n_ctx, n = Lc // C, (T + Lc) // C
    width = 3 * H * GDN_DK
    last = zg.shape[0] // C - 1

    def row_block(b, c):
        return jnp.where(c < n_ctx, (N + b * Lc) // C + c, (b * T) // C + c - n_ctx)

    z_spec = lambda d: pl.BlockSpec((C, width), lambda b, c: (jnp.clip(row_block(b, c) + d, 0, last), 0))
    lanes = lambda a: jnp.pad(a.astype(F32), (0, LANES - 2 * H))
    ad = jnp.zeros((8, LANES), F32).at[0].set(lanes(jnp.exp(a_log.astype(F32)))).at[1].set(lanes(dt_bias))
    per = lambda r, w: pl.BlockSpec((None, None, 2 * H, r, w), lambda b, c: (b, c, 0, 0, 0))
    shp = lambda r, w, dt: jax.ShapeDtypeStruct((B, n, 2 * H, r, w), dt)
    return pl.pallas_call(
        functools.partial(_gdn_prep_kernel, n_ctx=n_ctx, n_chunks=n), grid=(B, n),
        in_specs=[z_spec(-1), z_spec(0), z_spec(1),
                  pl.BlockSpec((C, LANES), lambda b, c: (row_block(b, c), 0)),
                  pl.BlockSpec((GDN_CONV, width), lambda b, c: (0, 0)),
                  pl.BlockSpec((8, LANES), lambda b, c: (0, 0))],
        out_specs=[per(2 * C, GDN_DK), per(C, GDN_DV), per(C, C), per(C, GDN_DK), per(1, LANES)],
        out_shape=[shp(2 * C, GDN_DK, BF16), shp(C, GDN_DV, F32), shp(C, C, BF16), shp(C, GDN_DK, BF16),
                   shp(1, LANES, F32)],
        scratch_shapes=[pltpu.VMEM((C + 16, width), F32), pltpu.VMEM((C, width), F32)],
        compiler_params=_params("parallel", "parallel"))(
            zg, zg, zg, zs, conv_w.reshape(GDN_CONV, width).astype(F32), ad)


def _gdn_scan_kernel(wqf, uf, qkf, kdf, totf, wqb, ub, qkb, kdb, totb, of_ref, ob_ref, s_ref):
    C, H, dv = GDN_CHUNK, GDN_HEADS, GDN_DV

    @pl.when(pl.program_id(1) == 0)
    def _():
        s_ref[...] = jnp.zeros_like(s_ref)

    for d, (wq, u, qk, kd, tot, o_ref) in enumerate(((wqf, uf, qkf, kdf, totf, of_ref),
                                                     (wqb, ub, qkb, kdb, totb, ob_ref))):
        hs = range(H)
        ws = [jnp.dot(wq[h], s_ref[d * H + h].astype(BF16), preferred_element_type=F32) for h in hs]
        v_new = [(u[h] - ws[h][:C]).astype(BF16) for h in hs]
        for h in hs:
            o_ref[:, h * dv:(h + 1) * dv] = ws[h][C:] + jnp.dot(qk[h], v_new[h], preferred_element_type=F32)
        upd = [lax.dot_general(kd[h], v_new[h], (((0,), (0,)), ((), ())), preferred_element_type=F32) for h in hs]
        for h in hs:
            s_ref[d * H + h] = s_ref[d * H + h] * tot[h] + upd[h]


def gdn_scan(wq, u, qk, kd, tot, n_ctx, T, Lc):
    B, n = wq.shape[0], wq.shape[1]
    C, H = GDN_CHUNK, GDN_HEADS
    N = B * T

    def bwd(s):
        return jnp.where(s < n_ctx, n_ctx - 1 - s, n - 1 - (s - n_ctx))

    def row_block(b, c):
        return jnp.where(c < n_ctx, (N + b * Lc) // C + c, (b * T) // C + c - n_ctx)

    fw = lambda r, w: pl.BlockSpec((None, None, H, r, w), lambda b, s: (b, s, 0, 0, 0))
    bw = lambda r, w: pl.BlockSpec((None, None, H, r, w), lambda b, s: (b, bwd(s), 1, 0, 0))
    shapes = ((2 * C, GDN_DK), (C, GDN_DV), (C, C), (C, GDN_DK), (1, LANES))
    o_shape = jax.ShapeDtypeStruct((B * n * C, H * GDN_DV), F32)
    return pl.pallas_call(
        _gdn_scan_kernel, grid=(B, n),
        in_specs=[fw(*s) for s in shapes] + [bw(*s) for s in shapes],
        out_specs=[pl.BlockSpec((C, H * GDN_DV), lambda b, s: (row_block(b, s), 0)),
                   pl.BlockSpec((C, H * GDN_DV), lambda b, s: (row_block(b, bwd(s)), 0))],
        out_shape=[o_shape, o_shape],
        scratch_shapes=[pltpu.VMEM((2 * H, GDN_DK, GDN_DV), F32)],
        compiler_params=_params("parallel", "arbitrary"))(wq, u, qk, kd, tot, wq, u, qk, kd, tot)


def kernel(x, c, ctx, c_ctx, w_mod, b_mod, norm_mix_g, norm_ffn_g, w_in, gdn_conv_w, gdn_a_log, gdn_dt_bias,
           gdn_norm_g, mla_q_norm_g, mla_kv_norm_g, mla_w_uq, mla_w_ukv, mla_qn_g, mla_kn_g, swa_qn_g, swa_kn_g,
           swa_sink, w_branch_a, w_branch_b, w_branch_c, w_out, ffn_w_gate, ffn_w_up, ffn_w_down, moe_router,
           moe_router_bias, moe_w_gate, moe_w_up, moe_w_down):
    B, T, D = x.shape
    Lc = ctx.shape[1]
    depth = w_mod.shape[0]
    N, Nc = B * T, B * Lc
    M = N + Nc
    TM = _pow2_tile(MAX_ROW_TILE, T, Nc)
    gran = min(TM, 256)
    tseq = _pow2_tile(256, T, Lc)
    assert T % Lc == 0 and Lc % SWA_BLOCK == 0 and Lc % GDN_CHUNK == 0
    H = GDN_HEADS
    nk = H * GDN_DK

    sizes = (nk, nk, H * GDN_DV, H * GDN_DV, 2 * H, 2 * H, MLA_Q_RANK, MLA_KV_RANK, MLA_ROPE,
             SWA_HEADS * SWA_HD, SWA_KV_HEADS * SWA_HD, SWA_KV_HEADS * SWA_HD, N_BRANCH * D)
    off = np.concatenate([[0], np.cumsum(sizes)])
    n_gdn = int(off[4])
    tn_gdn = _pow2_tile(512, n_gdn)
    tn_mix = 512
    tn_gate = _pow2_tile(512, D)

    rows_all = jnp.concatenate([x.reshape(N, D), ctx.reshape(Nc, D)], axis=0)
    tile_batch = np.minimum(np.arange(M // gran) * gran // T, B)
    tile_batch = np.where(np.arange(M // gran) * gran < N, tile_batch, B)
    cvec = jnp.concatenate([c, c_ctx[None, :], jnp.zeros((MOD_ROWS - B - 1, D), F32)], axis=0)
    cvec = jax.nn.silu(cvec)
    tab_mla = _rope_tables(T, Lc, MLA_ROPE)
    tab_swa = _rope_tables(T, 0, SWA_HD)
    n_ctx_chunks = Lc // GDN_CHUNK
    S = Lc + T

    for l in range(depth):
        need_ctx = l < depth - 1
        rows_out = M if need_ctx else N

        mod = matmul(cvec, w_mod, w_lead=l, tm=MOD_ROWS, tn=_pow2_tile(1024, 6 * D), out_dtype=F32,
                     epilogue="bias", bias=b_mod[l][None, :])
        mod = mod.reshape(MOD_ROWS, 6, D)[:B + 1]
        mod = jnp.pad(mod, ((0, 0), (0, MOD_ROWS - 6), (0, 0)))
        modt = mod[tile_batch]

        h = modulate(rows_all, norm_mix_g[l], modt, M, gran, shift_row=0, scale_row=1)
        w_l = w_in[l]
        seg = lambda a, b: w_l[:, off[a]:off[b]]
        zcols = lambda n: jnp.zeros((D, n), w_l.dtype)
        w_small = jnp.concatenate([seg(4, 6), zcols(LANES - 4 * H)], axis=1)
        w_mix = jnp.concatenate([seg(6, 7), seg(10, 11), seg(9, 10), seg(7, 8), seg(11, 12), seg(8, 9),
                                 zcols(MIX_WIDTH - MIX_KR - MLA_ROPE)], axis=1)
        zg = matmul(h, w_in, w_lead=l, ncols=n_gdn, tm=TM, tn=tn_gdn, out_dtype=F32)
        zs = matmul(h, w_small, tm=TM, tn=LANES, out_dtype=F32)
        zm = matmul(h, w_mix, tm=TM, tn=tn_mix, out_dtype=F32)

        wq_c, u_c, qk_c, kd_c, tot_c = gdn_prepare_chunks(zg, zs, gdn_conv_w[l], gdn_a_log[l], gdn_dt_bias[l], B, T, Lc)
        o_f, o_b2 = gdn_scan(wq_c, u_c, qk_c, kd_c, tot_c, n_ctx_chunks, T, Lc)

        mq = mla_project_q(zm, rows_out, mla_q_norm_g[l], mla_w_uq[l], mla_qn_g[l], tab_mla, B, T, Lc, tseq)
        mk, mv = mla_project_kv(zm, mla_kv_norm_g[l], mla_w_ukv[l], mla_kn_g[l], tab_mla, B, T, Lc, tseq)
        o_b = full_attention(mq, mk, mv, _pow2_tile(MAX_ROW_TILE, T), T, 0, S, 0)
        o_c = window_attention(zm, tab_swa, swa_qn_g[l], swa_kn_g[l], swa_sink[l], B, T, Lc, local=True)
        if need_ctx:
            o_b = jnp.concatenate([o_b, full_attention(mq, mk, mv, _pow2_tile(256, Lc), Lc, T, Lc, T // Lc)], axis=0)
            o_c = jnp.concatenate([o_c, window_attention(zm, None, swa_qn_g[l], swa_kn_g[l], swa_sink[l], B, T, Lc,
                                                         local=False)], axis=0)

        merged = merge_branches(h, seg(12, 13), o_f, o_b2, zg, 3 * nk, gdn_norm_g[l], o_b, o_c, w_branch_a, w_branch_b,
                                w_branch_c, l, rows_out, TM // 2, tn_gate)
        rows_new = matmul(merged, w_out, w_lead=l, tm=TM, tn=tn_gate, out_dtype=F32, epilogue="residual",
                          resid=rows_all, modt=modt, gran=gran, gate_row=2)

        i = l // 2
        if l % 2 == 0:
            h2 = modulate(rows_new, norm_ffn_g[l], modt, rows_out, gran, shift_row=3, scale_row=4)
            nblk = rows_out // TM
            F = ffn_w_gate.shape[-1]
            y = swiglu_grouped(h2, ffn_w_gate[:, None], ffn_w_up[:, None], ffn_w_down[:, None], i,
                               jnp.zeros((nblk,), jnp.int32), jnp.full((nblk,), 2, jnp.int32),
                               jnp.ones((rows_out, 1), F32), rows_out, TM, _pow2_tile(256, F))
        else:
            E = moe_w_gate.shape[1]
            rw = jnp.pad(moe_router[i], ((0, 0), (0, LANES - E)))
            rb = jnp.pad(moe_router_bias[i].astype(F32), (0, LANES - E))[None, :]
            h2, logits = modulate(rows_new, norm_ffn_g[l], modt, rows_out, gran, shift_row=3, scale_row=4,
                                  router=(rw, rb))
            top_logit, top_idx = lax.top_k(logits[:, :E], TOP_K)
            top_w = jax.nn.softmax(top_logit, axis=-1)
            flat_e = top_idx.reshape(-1)
            onehot = (flat_e[:, None] == jnp.arange(E)[None, :]).astype(jnp.int32)
            rank = jnp.take_along_axis(jnp.cumsum(onehot, axis=0) - onehot, flat_e[:, None], axis=1)[:, 0]
            counts = jnp.sum(onehot, axis=0)
            padded = (counts + TM - 1) // TM * TM
            pstart = jnp.cumsum(padded) - padded
            dest = pstart[flat_e] + rank
            nblk = -(-(rows_out * TOP_K) // TM) + E
            slots = nblk * TM
            src = jnp.zeros((slots,), jnp.int32).at[dest].set(jnp.arange(rows_out * TOP_K, dtype=jnp.int32) // TOP_K)
            w_slot = jnp.zeros((slots,), F32).at[dest].set(top_w.reshape(-1))
            blk_start = jnp.arange(nblk, dtype=jnp.int32) * TM
            ends = jnp.cumsum(padded)
            block_e = jnp.minimum(jnp.searchsorted(ends, blk_start, side="right"), E - 1).astype(jnp.int32)
            used = blk_start < ends[-1]
            last_e = block_e[jnp.maximum(jnp.sum(used.astype(jnp.int32)) - 1, 0)]
            block_e = jnp.where(used, block_e, last_e)
            rows_in_block = jnp.clip((pstart + counts)[block_e] - blk_start, 0, TM)
            half = TM // 2
            block_valid = jnp.where(used, (rows_in_block + half - 1) // half, 0).astype(jnp.int32)
            F = moe_w_gate.shape[-1]
            ys = swiglu_grouped(h2[src], moe_w_gate, moe_w_up, moe_w_down, i, block_e, block_valid,
                                w_slot[:, None], slots, TM, _pow2_tile(256, F))
            dest2 = dest.reshape(rows_out, TOP_K)
            y = ys[dest2[:, 0]] + ys[dest2[:, 1]]
        gate_f = modt[:, 5][:rows_out // gran]
        rows_ffn = (rows_new.reshape(rows_out // gran, gran, D) + gate_f[:, None, :] * y.reshape(rows_out // gran, gran, D))
        rows_all = rows_ffn.reshape(rows_out, D)

    return rows_all[:N].reshape(B, T, D)
```

```python
import functools

import jax
import jax.numpy as jnp
import numpy as np
from jax import lax
from jax.experimental import pallas as pl
from jax.experimental.pallas import tpu as pltpu

F32 = jnp.float32
BF16 = jnp.bfloat16

GRID_W = 64
EPS = 1e-6
ROPE_BASE = 10000.0
N_BRANCH = 3
GDN_HEADS = 8
GDN_DK = 128
GDN_DV = 128
GDN_CONV = 5
GDN_CHUNK = 64
MLA_HEADS = 8
MLA_Q_RANK = 768
MLA_KV_RANK = 512
MLA_NOPE = 128
MLA_ROPE = 64
MLA_V = 128
MLA_QK = MLA_NOPE + MLA_ROPE
MLA_SCALE = MLA_QK ** -0.5
SWA_HEADS = 8
SWA_KV_HEADS = 2
SWA_HD = 128
SWA_WINDOW = 128
SWA_BLOCK = 128
SWA_SCALE = SWA_HD ** -0.5
N_EXPERTS = 8
TOP_K = 2

LANES = 128
VMEM_LIMIT_BYTES = 56 * 1024 * 1024
MAX_ROW_TILE = 1024
MOD_ROWS = 8
NEG_BIG = -1e30
LOG2E = 1.4426950408889634
ATTN_ROW_GROUP = 256
MLA_QK_PAD = 2 * LANES

MIX_CQ, MIX_SK, MIX_SQ, MIX_CKV, MIX_SV, MIX_KR, MIX_WIDTH = 0, 768, 1024, 2048, 2560, 2816, 3072


def _params(*sem):
    return pltpu.CompilerParams(dimension_semantics=sem, vmem_limit_bytes=VMEM_LIMIT_BYTES)


def _pow2_tile(limit, *dims):
    t = 1
    while t * 2 <= limit and all(d % (t * 2) == 0 for d in dims):
        t *= 2
    return t


def _rms_rows(x, g):
    return x * lax.rsqrt(jnp.mean(x * x, axis=-1, keepdims=True) + EPS) * g


def _modulate_kernel(x_ref, g_ref, mod_ref, *rest, shift_row, scale_row, with_router):
    mod = mod_ref[0]
    h = _rms_rows(x_ref[...], g_ref[...]) * (1.0 + mod[scale_row:scale_row + 1]) + mod[shift_row:shift_row + 1]
    if with_router:
        rw_ref, rb_ref, h_ref, lg_ref = rest
        lg_ref[...] = jnp.dot(h.astype(BF16), rw_ref[...].astype(BF16), preferred_element_type=F32) + rb_ref[...]
    else:
        (h_ref,) = rest
    h_ref[...] = h.astype(h_ref.dtype)


def modulate(x, gain, modt, rows, gran, shift_row, scale_row, router=None):
    D = x.shape[1]
    tm = gran
    kern = functools.partial(_modulate_kernel, shift_row=shift_row, scale_row=scale_row,
                             with_router=router is not None)
    in_specs = [pl.BlockSpec((tm, D), lambda i: (i, 0)),
                pl.BlockSpec((1, D), lambda i: (0, 0)),
                pl.BlockSpec((1, MOD_ROWS, D), lambda i: (i, 0, 0))]
    args = [x, gain.reshape(1, D), modt]
    out_shape = [jax.ShapeDtypeStruct((rows, D), BF16)]
    out_specs = [pl.BlockSpec((tm, D), lambda i: (i, 0))]
    if router is not None:
        rw, rb = router
        in_specs += [pl.BlockSpec((D, LANES), lambda i: (0, 0)), pl.BlockSpec((1, LANES), lambda i: (0, 0))]
        args += [rw, rb]
        out_shape.append(jax.ShapeDtypeStruct((rows, LANES), F32))
        out_specs.append(pl.BlockSpec((tm, LANES), lambda i: (i, 0)))
    out = pl.pallas_call(kern, grid=(rows // tm,), in_specs=in_specs, out_specs=out_specs,
                         out_shape=out_shape, compiler_params=_params("parallel"))(*args)
    return out if router is not None else out[0]


def _mm_kernel(a_ref, w_ref, *rest, epilogue, gate_row):
    acc = jnp.dot(a_ref[...].astype(BF16), w_ref[...].astype(BF16), preferred_element_type=F32)
    if epilogue == "bias":
        b_ref, o_ref = rest
        acc = acc + b_ref[...]
    elif epilogue == "sigmoid":
        (o_ref,) = rest
        acc = jax.nn.sigmoid(acc)
    elif epilogue == "residual":
        x_ref, mod_ref, o_ref = rest
        acc = x_ref[...] + mod_ref[0][gate_row:gate_row + 1] * acc
    else:
        (o_ref,) = rest
    o_ref[...] = acc.astype(o_ref.dtype)


def matmul(a, w, *, rows=None, w_lead=None, col0=0, ncols=None, tm, tn, out_dtype, epilogue=None,
           bias=None, resid=None, modt=None, gran=None, gate_row=0):
    rows = a.shape[0] if rows is None else rows
    K = a.shape[1]
    ncols = w.shape[-1] if ncols is None else ncols
    assert rows % tm == 0 and ncols % tn == 0 and col0 % tn == 0
    cb = col0 // tn
    if w.ndim == 3:
        w_spec = pl.BlockSpec((None, K, tn), lambda i, j: (w_lead, 0, cb + j))
    else:
        w_spec = pl.BlockSpec((K, tn), lambda i, j: (0, cb + j))
    in_specs = [pl.BlockSpec((tm, K), lambda i, j: (i, 0)), w_spec]
    args = [a, w]
    if epilogue == "bias":
        in_specs.append(pl.BlockSpec((1, tn), lambda i, j: (0, j)))
        args.append(bias)
    elif epilogue == "residual":
        step = tm // gran
        in_specs += [pl.BlockSpec((tm, tn), lambda i, j: (i, j)),
                     pl.BlockSpec((1, MOD_ROWS, tn), lambda i, j: (i * step, 0, j))]
        args += [resid, modt]
    kern = functools.partial(_mm_kernel, epilogue=epilogue, gate_row=gate_row)
    return pl.pallas_call(kern, grid=(rows // tm, ncols // tn), in_specs=in_specs,
                          out_specs=pl.BlockSpec((tm, tn), lambda i, j: (i, j)),
                          out_shape=jax.ShapeDtypeStruct((rows, ncols), out_dtype),
                          compiler_params=_params("parallel", "arbitrary"))(*args)


def _merge_kernel(h_ref, of_ref, ob2_ref, og_ref, gn_ref, ob_ref, oc_ref, wga_ref, wgb_ref, wgc_ref,
                  wa_ref, wb_ref, wc_ref, o_ref, oa_scr):
    @pl.when(pl.program_id(1) == 0)
    def _():
        for h in range(GDN_HEADS):
            cs = slice(h * GDN_DV, (h + 1) * GDN_DV)
            o = _rms_rows(of_ref[:, cs] + ob2_ref[:, cs], gn_ref[...])
            oa_scr[:, cs] = (o * jax.nn.silu(og_ref[:, cs])).astype(BF16)

    hm = h_ref[...]
    gate = lambda w_ref: jax.nn.sigmoid(jnp.dot(hm, w_ref[...], preferred_element_type=F32))
    acc = gate(wga_ref) * jnp.dot(oa_scr[...], wa_ref[...], preferred_element_type=F32)
    acc += gate(wgb_ref) * jnp.dot(ob_ref[...], wb_ref[...], preferred_element_type=F32)
    acc += gate(wgc_ref) * jnp.dot(oc_ref[...], wc_ref[...], preferred_element_type=F32)
    o_ref[...] = acc.astype(o_ref.dtype)


def merge_branches(h, w_gates, o_f, o_b2, zg, og_col, gdn_g, ob, oc, wa, wb, wc, l, rows, tm, tn):
    D = wa.shape[-1]
    nj = D // tn
    na = GDN_HEADS * GDN_DV
    row = lambda w, cb=0: pl.BlockSpec((tm, w), lambda i, j: (i, cb))
    g_spec = lambda k: pl.BlockSpec((D, tn), lambda i, j: (0, k * nj + j))
    wa, wb, wc = (w[l].astype(BF16) for w in (wa, wb, wc))
    w_gates = w_gates.astype(BF16)
    w_spec = lambda w: pl.BlockSpec((w.shape[0], tn), lambda i, j: (0, j))
    return pl.pallas_call(
        _merge_kernel, grid=(rows // tm, nj),
        in_specs=[row(D), row(na), row(na), row(na, og_col // na), pl.BlockSpec((1, GDN_DV), lambda i, j: (0, 0)),
                  row(ob.shape[1]), row(oc.shape[1]), g_spec(0), g_spec(1), g_spec(2),
                  w_spec(wa), w_spec(wb), w_spec(wc)],
        out_specs=pl.BlockSpec((tm, tn), lambda i, j: (i, j)),
        out_shape=jax.ShapeDtypeStruct((rows, D), BF16),
        scratch_shapes=[pltpu.VMEM((tm, na), BF16)],
        compiler_params=_params("parallel", "arbitrary"))(
            h, o_f, o_b2, zg, gdn_g.reshape(1, GDN_DV).astype(F32), ob, oc, w_gates, w_gates, w_gates, wa, wb, wc)


def _swiglu_kernel(be_ref, bv_ref, x_ref, wg_ref, wu_ref, wd_ref, rs_ref, o_ref):
    i, f = pl.program_id(0), pl.program_id(1)

    @pl.when(f == 0)
    def _():
        o_ref[...] = jnp.zeros_like(o_ref)

    def accumulate(n):
        x = x_ref[:n, :]
        g = jnp.dot(x, wg_ref[...].astype(BF16), preferred_element_type=F32)
        u = jnp.dot(x, wu_ref[...].astype(BF16), preferred_element_type=F32)
        h = (jax.nn.silu(g) * u).astype(BF16)
        o_ref[:n, :] += jnp.dot(h, wd_ref[...].astype(BF16), preferred_element_type=F32)

        @pl.when(f == pl.num_programs(1) - 1)
        def _():
            o_ref[:n, :] = o_ref[:n, :] * rs_ref[:n, :]

    tm = x_ref.shape[0]
    pl.when(bv_ref[i] == 2)(functools.partial(accumulate, tm))
    pl.when(bv_ref[i] == 1)(functools.partial(accumulate, tm // 2))


def swiglu_grouped(x, wg, wu, wd, l, block_e, block_valid, row_scale, rows, tm, tf):
    D = x.shape[1]
    F = wg.shape[-1]
    nf = F // tf
    assert rows % tm == 0 and F % tf == 0

    def f_idx(i, f, bv):
        return jnp.where(bv[i] > 0, f, nf - 1)

    grid_spec = pltpu.PrefetchScalarGridSpec(
        num_scalar_prefetch=2, grid=(rows // tm, nf),
        in_specs=[pl.BlockSpec((tm, D), lambda i, f, be, bv: (i, 0)),
                  pl.BlockSpec((None, None, D, tf), lambda i, f, be, bv: (l, be[i], 0, f_idx(i, f, bv))),
                  pl.BlockSpec((None, None, D, tf), lambda i, f, be, bv: (l, be[i], 0, f_idx(i, f, bv))),
                  pl.BlockSpec((None, None, tf, D), lambda i, f, be, bv: (l, be[i], f_idx(i, f, bv), 0)),
                  pl.BlockSpec((tm, 1), lambda i, f, be, bv: (i, 0))],
        out_specs=pl.BlockSpec((tm, D), lambda i, f, be, bv: (i, 0)))
    return pl.pallas_call(_swiglu_kernel, grid_spec=grid_spec,
                          out_shape=jax.ShapeDtypeStruct((rows, D), F32),
                          compiler_params=_params("parallel", "arbitrary"))(
                              block_e, block_valid, x, wg, wu, wd, row_scale)


def _rope_tables(T, extra, rot_dim):
    n = rot_dim // 4
    t = jnp.arange(T)
    inv = jnp.power(ROPE_BASE, -jnp.arange(n, dtype=F32) / n)
    ar = (t // GRID_W).astype(F32)[:, None] * inv
    ac = (t % GRID_W).astype(F32)[:, None] * inv
    z = jnp.zeros((T, n), F32)
    pad = lambda a, fill: jnp.concatenate([a, jnp.full((T, LANES - 4 * n), fill, F32)], axis=1)
    c = pad(jnp.concatenate([jnp.cos(ar), jnp.cos(ar), jnp.cos(ac), jnp.cos(ac)], axis=1), 1.0)
    a = pad(jnp.concatenate([-jnp.sin(ar), z, -jnp.sin(ac), z], axis=1), 0.0)
    b = pad(jnp.concatenate([z, jnp.sin(ar), z, jnp.sin(ac)], axis=1), 0.0)
    tab = jnp.concatenate([c, a, b], axis=1)
    ident = jnp.concatenate([jnp.ones((extra, LANES), F32), jnp.zeros((extra, 2 * LANES), F32)], axis=1)
    return jnp.concatenate([tab, ident], axis=0)


def _apply_rope(x, tab, half):
    return (x * tab[:, :LANES] + pltpu.roll(x, LANES - half, 1) * tab[:, LANES:2 * LANES]
            + pltpu.roll(x, half, 1) * tab[:, 2 * LANES:])


def _mla_q_kernel(c_ref, g_ref, w_ref, hg_ref, tab_ref, q_ref):
    a = _rms_rows(c_ref[...], g_ref[...]).astype(BF16)
    tab, hg = tab_ref[...], hg_ref[...]
    for h in range(MLA_HEADS):
        acc = jnp.dot(a, w_ref[:, h * MLA_QK_PAD:(h + 1) * MLA_QK_PAD], preferred_element_type=F32)
        y = acc * lax.rsqrt(jnp.sum(acc * acc, axis=-1, keepdims=True) * (1.0 / MLA_QK) + EPS) * hg
        q_ref[h, :, :LANES] = y[:, :LANES].astype(q_ref.dtype)
        q_ref[h, :, LANES:] = _apply_rope(y[:, LANES:], tab, MLA_ROPE // 4).astype(q_ref.dtype)


def _mla_kv_kernel(c_ref, kr_ref, g_ref, w_ref, hg_ref, tab_ref, k_ref, v_ref):
    a = _rms_rows(c_ref[...], g_ref[...]).astype(BF16)
    tab, hg = tab_ref[...], hg_ref[...]
    kr = kr_ref[...]
    kr_ss = jnp.sum(kr * kr, axis=-1, keepdims=True)
    width = MLA_NOPE + MLA_V
    for h in range(MLA_HEADS):
        acc = jnp.dot(a, w_ref[:, h * width:(h + 1) * width], preferred_element_type=F32)
        kn = acc[:, :LANES]
        r = lax.rsqrt((jnp.sum(kn * kn, axis=-1, keepdims=True) + kr_ss) * (1.0 / MLA_QK) + EPS)
        k_ref[h, :, :LANES] = (kn * r * hg[:, :LANES]).astype(k_ref.dtype)
        k_ref[h, :, LANES:] = _apply_rope(kr * r * hg[:, LANES:], tab, MLA_ROPE // 4).astype(k_ref.dtype)
        v_ref[h] = acc[:, LANES:].astype(v_ref.dtype)


def _seq_pos(i, tm, T, Lc, N):
    nl, nc, nlt = T // tm, Lc // tm, N // tm
    k = i - nlt
    return jnp.where(i < nlt, i // nl, k // nc), jnp.where(i < nlt, i % nl, nl + k % nc)


def mla_project_q(zm, rows, q_norm_g, w_uq_l, qn_g, tab, B, T, Lc, tm):
    N = B * T
    Lq = T + (Lc if rows > N else 0)
    H, R = MLA_HEADS, MLA_Q_RANK
    wp = jnp.pad(w_uq_l.reshape(R, H, MLA_QK), ((0, 0), (0, 0), (0, MLA_QK_PAD - MLA_QK)))
    wp = wp.reshape(R, H * MLA_QK_PAD).astype(BF16)
    hg = jnp.pad(qn_g.astype(F32) * (MLA_SCALE * LOG2E), (0, MLA_QK_PAD - MLA_QK))[None, :]
    pos = lambda i: _seq_pos(i, tm, T, Lc, N)
    return pl.pallas_call(
        _mla_q_kernel, grid=(rows // tm,),
        in_specs=[pl.BlockSpec((tm, R), lambda i: (i, MIX_CQ // R)),
                  pl.BlockSpec((1, R), lambda i: (0, 0)),
                  pl.BlockSpec((R, H * MLA_QK_PAD), lambda i: (0, 0)),
                  pl.BlockSpec((1, MLA_QK_PAD), lambda i: (0, 0)),
                  pl.BlockSpec((tm, 3 * LANES), lambda i: (pos(i)[1], 0))],
        out_specs=pl.BlockSpec((None, H, tm, MLA_QK_PAD), lambda i: (pos(i)[0], 0, pos(i)[1], 0)),
        out_shape=jax.ShapeDtypeStruct((B, H, Lq, MLA_QK_PAD), BF16),
        compiler_params=_params("parallel"))(zm, q_norm_g.reshape(1, R).astype(F32), wp, hg, tab)


def mla_project_kv(zm, kv_norm_g, w_ukv_l, kn_g, tab, B, T, Lc, tm):
    M = zm.shape[0]
    N = B * T
    H, R = MLA_HEADS, MLA_KV_RANK
    hg = jnp.pad(kn_g.astype(F32), (0, MLA_QK_PAD - MLA_QK))[None, :]
    pos = lambda i: _seq_pos(i, tm, T, Lc, N)
    o_spec = lambda w: pl.BlockSpec((None, H, tm, w), lambda i: (pos(i)[0], 0, pos(i)[1], 0))
    return pl.pallas_call(
        _mla_kv_kernel, grid=(M // tm,),
        in_specs=[pl.BlockSpec((tm, R), lambda i: (i, MIX_CKV // R)),
                  pl.BlockSpec((tm, LANES), lambda i: (i, MIX_KR // LANES)),
                  pl.BlockSpec((1, R), lambda i: (0, 0)),
                  pl.BlockSpec((R, H * (MLA_NOPE + MLA_V)), lambda i: (0, 0)),
                  pl.BlockSpec((1, MLA_QK_PAD), lambda i: (0, 0)),
                  pl.BlockSpec((tm, 3 * LANES), lambda i: (pos(i)[1], 0))],
        out_specs=[o_spec(MLA_QK_PAD), o_spec(MLA_V)],
        out_shape=[jax.ShapeDtypeStruct((B, H, T + Lc, MLA_QK_PAD), BF16),
                   jax.ShapeDtypeStruct((B, H, T + Lc, MLA_V), BF16)],
        compiler_params=_params("parallel"))(zm, zm, kv_norm_g.reshape(1, R).astype(F32), w_ukv_l.astype(BF16), hg, tab)


def _attn_kernel(q_ref, k_ref, v_ref, o_ref, *, sub):
    for r0 in range(0, q_ref.shape[0], sub):
        rows = slice(r0, r0 + sub)
        s = lax.dot_general(q_ref[rows, :], k_ref[...], (((1,), (1,)), ((), ())), preferred_element_type=F32)
        p = jnp.exp2(s - jnp.max(s, axis=-1, keepdims=True))
        den = jnp.sum(p, axis=-1, keepdims=True)
        o = jnp.dot(p.astype(BF16), v_ref[...], preferred_element_type=F32)
        o_ref[rows, :] = (o * (1.0 / den)).astype(o_ref.dtype)


def full_attention(q, k, v, tq, n_q, q_off, kl, k_blk):
    B, H, _, d = q.shape
    e = v.shape[3]
    nt = n_q // tq
    qo = q_off // tq
    return pl.pallas_call(
        functools.partial(_attn_kernel, sub=min(tq, ATTN_ROW_GROUP)), grid=(B, H, nt),
        in_specs=[pl.BlockSpec((None, None, tq, d), lambda b, h, i: (b, h, qo + i, 0)),
                  pl.BlockSpec((None, None, kl, d), lambda b, h, i: (b, h, k_blk, 0)),
                  pl.BlockSpec((None, None, kl, e), lambda b, h, i: (b, h, k_blk, 0))],
        out_specs=pl.BlockSpec((tq, e), lambda b, h, i: (b * nt + i, h)),
        out_shape=jax.ShapeDtypeStruct((B * n_q, H * e), BF16),
        compiler_params=_params("parallel", "parallel", "arbitrary"))(q, k, v)


def _swa_kernel(*refs, local, n_blocks):
    if local:
        (q_ref, kp_ref, kc_ref, kn_ref, vp_ref, vc_ref, vn_ref, kx_ref, vx_ref, tp_ref, tc_ref, tn_ref,
         qg_ref, kg_ref, sink_ref, o_ref) = refs
    else:
        q_ref, kx_ref, vx_ref, qg_ref, kg_ref, sink_ref, o_ref = refs
    n = pl.program_id(1)
    Bk, d = SWA_BLOCK, SWA_HD
    Lc = kx_ref.shape[0]
    R = SWA_HEADS // SWA_KV_HEADS
    half = SWA_HD // 4

    def prep(x, g, tab):
        y = _rms_rows(x, g)
        return y if tab is None else _apply_rope(y, tab, half)

    if local:
        iq = lax.broadcasted_iota(jnp.int32, (Bk, 3 * Bk), 0)
        jk = lax.broadcasted_iota(jnp.int32, (Bk, 3 * Bk), 1)
        valid = jnp.abs(iq + Bk - jk) <= SWA_WINDOW
        valid = valid & ((jk >= Bk) | (n > 0)) & ((jk < 2 * Bk) | (n < n_blocks - 1))
        bias = jnp.where(valid, 0.0, NEG_BIG).astype(F32)
        bias = jnp.concatenate([bias, jnp.zeros((Bk, Lc), F32)], axis=1)
        tp, tc, tn = tp_ref[...], tc_ref[...], tn_ref[...]
    else:
        tc = None
    kg, qg = kg_ref[...], qg_ref[...]
    kcat, vcat = [], []
    for g in range(SWA_KV_HEADS):
        cs = slice(g * d, (g + 1) * d)
        kx = prep(kx_ref[:, cs], kg, None).astype(BF16)
        if local:
            kcat.append(jnp.concatenate([prep(kp_ref[:, cs], kg, tp).astype(BF16), prep(kc_ref[:, cs], kg, tc).astype(BF16),
                                         prep(kn_ref[:, cs], kg, tn).astype(BF16), kx], axis=0))
            vcat.append(jnp.concatenate([vp_ref[:, cs].astype(BF16), vc_ref[:, cs].astype(BF16),
                                         vn_ref[:, cs].astype(BF16), vx_ref[:, cs].astype(BF16)], axis=0))
        else:
            kcat.append(kx)
            vcat.append(vx_ref[:, cs].astype(BF16))
    hs = range(SWA_HEADS)
    qh = [prep(q_ref[:, h * d:(h + 1) * d], qg, tc).astype(BF16) for h in hs]
    s = [lax.dot_general(qh[h], kcat[h // R], (((1,), (1,)), ((), ())), preferred_element_type=F32) for h in hs]
    if local:
        s = [sh + bias for sh in s]
    sink = [sink_ref[h:h + 1, 0:1] for h in hs]
    m = [jnp.maximum(jnp.max(s[h], axis=-1, keepdims=True), sink[h]) for h in hs]
    p = [jnp.exp2(s[h] - m[h]).astype(BF16) for h in hs]
    vone = [jnp.concatenate([v, jnp.ones_like(v)], axis=1) for v in vcat]
    for h in hs:
        o = jnp.dot(p[h], vone[h // R], preferred_element_type=F32)
        den = o[:, d:] + jnp.exp2(sink[h] - m[h])
        o_ref[:, h * d:(h + 1) * d] = (o[:, :d] * (1.0 / den)).astype(o_ref.dtype)


def window_attention(zm, tab, qn_g, kn_g, sink, B, T, Lc, local):
    N = B * T
    Bk = SWA_BLOCK
    Q, KV = SWA_HEADS * SWA_HD, SWA_KV_HEADS * SWA_HD
    n_q = T if local else Lc
    nb = n_q // Bk
    row0 = 0 if local else N // Bk
    sink_b = jnp.broadcast_to(sink.astype(F32)[:, None], (SWA_HEADS, LANES))
    q_spec = pl.BlockSpec((Bk, Q), lambda b, n: (row0 + b * nb + n, MIX_SQ // Q))
    kx_spec = pl.BlockSpec((Lc, KV), lambda b, n: (N // Lc + b, MIX_SK // KV))
    vx_spec = pl.BlockSpec((Lc, KV), lambda b, n: (N // Lc + b, MIX_SV // KV))
    g_spec = pl.BlockSpec((1, SWA_HD), lambda b, n: (0, 0))
    s_spec = pl.BlockSpec((SWA_HEADS, LANES), lambda b, n: (0, 0))
    gains = (qn_g.reshape(1, SWA_HD).astype(F32) * (SWA_SCALE * LOG2E), kn_g.reshape(1, SWA_HD).astype(F32),
             sink_b * LOG2E)
    if local:
        pv = lambda n: jnp.maximum(n - 1, 0)
        nx = lambda n: jnp.minimum(n + 1, nb - 1)
        kv_spec = lambda f, col: pl.BlockSpec((Bk, KV), lambda b, n: (b * nb + f(n), col // KV))
        t_spec = lambda f: pl.BlockSpec((Bk, 3 * LANES), lambda b, n: (f(n), 0))
        same = lambda n: n
        in_specs = [q_spec, kv_spec(pv, MIX_SK), kv_spec(same, MIX_SK), kv_spec(nx, MIX_SK),
                    kv_spec(pv, MIX_SV), kv_spec(same, MIX_SV), kv_spec(nx, MIX_SV), kx_spec, vx_spec,
                    t_spec(pv), t_spec(same), t_spec(nx), g_spec, g_spec, s_spec]
        args = (zm,) * 9 + (tab,) * 3 + gains
    else:
        in_specs = [q_spec, kx_spec, vx_spec, g_spec, g_spec, s_spec]
        args = (zm,) * 3 + gains
    kern = functools.partial(_swa_kernel, local=local, n_blocks=nb)
    return pl.pallas_call(kern, grid=(B, nb), in_specs=in_specs,
                          out_specs=pl.BlockSpec((Bk, Q), lambda b, n: (b * nb + n, 0)),
                          out_shape=jax.ShapeDtypeStruct((B * n_q, Q), BF16),
                          compiler_params=_params("parallel", "arbitrary"))(*args)


def _split3(x):
    hi = x.astype(BF16)
    r1 = x - hi.astype(F32)
    mid = r1.astype(BF16)
    lo = (r1 - mid.astype(F32)).astype(BF16)
    return hi, mid, lo


def _dot_bf16(a, b):
    return jnp.dot(a.astype(BF16), b.astype(BF16), preferred_element_type=F32)


def _dot_nt(a, b):
    return lax.dot_general(a.astype(BF16), b.astype(BF16), (((1,), (1,)), ((), ())), preferred_element_type=F32)


def _gdn_prep_kernel(zp_ref, zc_ref, zn_ref, zs_ref, cw_ref, ad_ref, wq_ref, u_ref, qk_ref, kd_ref, tot_ref,
                     ext_scr, qkv_scr, *, n_ctx, n_chunks):
    C, H, dk, dv = GDN_CHUNK, GDN_HEADS, GDN_DK, GDN_DV
    P = 2 * C
    nk = H * dk
    halo = 8
    pad = (GDN_CONV - 1) // 2

    ch = pl.program_id(1)
    has_prev = jnp.where((ch == 0) | (ch == n_ctx), 0.0, 1.0)
    has_next = jnp.where((ch == n_ctx - 1) | (ch == n_chunks - 1), 0.0, 1.0)
    ext_scr[0:halo, :] = zp_ref[C - halo:C, :] * has_prev
    ext_scr[halo:halo + C, :] = zc_ref[...]
    ext_scr[halo + C:, :] = zn_ref[0:halo, :] * has_next
    for part in range(3):
        cs = slice(part * nk, (part + 1) * nk)
        y = sum(ext_scr[halo - pad + j:halo - pad + j + C, cs] * cw_ref[j:j + 1, cs] for j in range(GDN_CONV))
        qkv_scr[:, cs] = y * jax.nn.sigmoid(y)
    for h in range(H):
        for part, scale in ((0, dk ** -0.5), (1, 1.0)):
            cs = slice(part * nk + h * dk, part * nk + (h + 1) * dk)
            y = qkv_scr[:, cs]
            qkv_scr[:, cs] = y * (lax.rsqrt(jnp.sum(y * y, axis=-1, keepdims=True) + EPS) * scale)
    q_ref, k_ref, v_ref = qkv_scr.at[:, 0:nk], qkv_scr.at[:, nk:2 * nk], qkv_scr.at[:, 2 * nk:3 * nk]

    zs = zs_ref[...]
    beta = jax.nn.sigmoid(zs)
    a = pltpu.roll(zs, LANES - 2 * H, 1) + ad_ref[1:2, :]
    g = -ad_ref[0:1, :] * (jnp.maximum(a, 0.0) + jnp.log(1.0 + jnp.exp(-jnp.abs(a))))

    ii = lax.broadcasted_iota(jnp.int32, (C, C), 0)
    jj = lax.broadcasted_iota(jnp.int32, (C, C), 1)
    low = (ii >= jj).astype(BF16)
    upp = (ii <= jj).astype(BF16)
    r = lax.broadcasted_iota(jnp.int32, (P, P), 0)
    c = lax.broadcasted_iota(jnp.int32, (P, P), 1)
    rq = jnp.where(r < C, 0, 1)
    cq = jnp.where(c < C, 0, 1)
    ahead = (r - c) * (1 - 2 * rq)
    causal = (rq == cq) & (ahead >= 0)
    strict = (rq == cq) & (ahead > 0)
    eye = (r == c).astype(F32)
    row_fwd = lax.broadcasted_iota(jnp.int32, (P, 1), 0) < C

    g3 = _split3(g)
    tri = jnp.concatenate([low, upp], axis=0)
    dcol_all = sum(jnp.dot(tri, p, preferred_element_type=F32) for p in g3)
    tdot = lambda p, t: lax.dot_general(p, t, (((0,), (0,)), ((), ())), preferred_element_type=F32)
    drow_all = jnp.concatenate([sum(tdot(p, upp) for p in g3)[:H], sum(tdot(p, low) for p in g3)[H:2 * H]],
                               axis=1)
    beta2 = jnp.concatenate([beta, beta], axis=0)

    def pair_col(a, h):
        return jnp.where(row_fwd, a[:, h:h + 1], a[:, H + h:H + h + 1])

    stack = lambda ref, h, w: jnp.concatenate([ref[:, h * w:(h + 1) * w]] * 2, axis=0)

    group = 8
    for h0 in range(0, H, group):
        hs = range(h0, h0 + group)
        dc = [pair_col(dcol_all, h) for h in hs]
        seg = [jnp.exp(jnp.where(causal, dc[a] - drow_all[h:h + 1, :], NEG_BIG)) for a, h in enumerate(hs)]
        b2 = [pair_col(beta2, h) for h in hs]
        kk = [_dot_nt(stack(k_ref, h, dk) * b2[a], stack(k_ref, h, dk)) for a, h in enumerate(hs)]
        pw = [jnp.where(strict, kk[a] * seg[a], 0.0) for a in range(group)]
        inv = [eye - m for m in pw]
        k = 2
        while k < C:
            pw = [_dot_bf16(m, m) for m in pw]
            inv = [t + _dot_bf16(t, m) for t, m in zip(inv, pw)]
            k *= 2
        for a, h in enumerate(hs):
            k2, q2, v2 = stack(k_ref, h, dk), stack(q_ref, h, dk), stack(v_ref, h, dv)
            ecol = jnp.exp(dc[a])
            rhs = jnp.concatenate([k2 * (b2[a] * ecol), v2 * b2[a]], axis=1)
            sol = rhs + _dot_bf16(inv[a] - eye, rhs)
            qk = _dot_nt(q2, k2) * seg[a]
            dlast = jnp.where(row_fwd, dc[a][C - 1:C], dc[a][C:C + 1])
            qd = q2 * ecol
            kd = k2 * jnp.exp(dlast - dc[a])
            tot = jnp.exp(dlast)
            for d, rs in enumerate((slice(0, C), slice(C, P))):
                hd = d * H + h
                wq_ref[hd, :C, :] = sol[rs, :dk].astype(wq_ref.dtype)
                wq_ref[hd, C:, :] = qd[rs].astype(wq_ref.dtype)
                u_ref[hd] = sol[rs, dk:]
                qk_ref[hd] = qk[rs, d * C:(d + 1) * C].astype(qk_ref.dtype)
                kd_ref[hd] = kd[rs].astype(kd_ref.dtype)
                tot_ref[hd] = jnp.broadcast_to(tot[d * C:d * C + 1], (1, LANES))


def gdn_prepare_chunks(zg, zs, conv_w, a_log, dt_bias, B, T, Lc):
    C, H = GDN_CHUNK, GDN_HEADS
    N = B * T
    n_ctx, n = Lc // C, (T + Lc) // C
    width = 3 * H * GDN_DK
    last = zg.shape[0] // C - 1

    def row_block(b, c):
        return jnp.where(c < n_ctx, (N + b * Lc) // C + c, (b * T) // C + c - n_ctx)

    z_spec = lambda d: pl.BlockSpec((C, width), lambda b, c: (jnp.clip(row_block(b, c) + d, 0, last), 0))
    lanes = lambda a: jnp.pad(a.astype(F32), (0, LANES - 2 * H))
    ad = jnp.zeros((8, LANES), F32).at[0].set(lanes(jnp.exp(a_log.astype(F32)))).at[1].set(lanes(dt_bias))
    per = lambda r, w: pl.BlockSpec((None, None, 2 * H, r, w), lambda b, c: (b, c, 0, 0, 0))
    shp = lambda r, w, dt: jax.ShapeDtypeStruct((B, n, 2 * H, r, w), dt)
    return pl.pallas_call(
        functools.partial(_gdn_prep_kernel, n_ctx=n_ctx, n_chunks=n), grid=(B, n),
        in_specs=[z_spec(-1), z_spec(0), z_spec(1),
                  pl.BlockSpec((C, LANES), lambda b, c: (row_block(b, c), 0)),
                  pl.BlockSpec((GDN_CONV, width), lambda b, c: (0, 0)),
                  pl.BlockSpec((8, LANES), lambda b, c: (0, 0))],
        out_specs=[per(2 * C, GDN_DK), per(C, GDN_DV), per(C, C), per(C, GDN_DK), per(1, LANES)],
        out_shape=[shp(2 * C, GDN_DK, BF16), shp(C, GDN_DV, F32), shp(C, C, BF16), shp(C, GDN_DK, BF16),
                   shp(1, LANES, F32)],
        scratch_shapes=[pltpu.VMEM((C + 16, width), F32), pltpu.VMEM((C, width), F32)],
        compiler_params=_params("parallel", "parallel"))(
            zg, zg, zg, zs, conv_w.reshape(GDN_CONV, width).astype(F32), ad)


def _gdn_scan_kernel(wqf, uf, qkf, kdf, totf, wqb, ub, qkb, kdb, totb, of_ref, ob_ref, s_ref):
    C, H, dv = GDN_CHUNK, GDN_HEADS, GDN_DV

    @pl.when(pl.program_id(1) == 0)
    def _():
        s_ref[...] = jnp.zeros_like(s_ref)

    for d, (wq, u, qk, kd, tot, o_ref) in enumerate(((wqf, uf, qkf, kdf, totf, of_ref),
                                                     (wqb, ub, qkb, kdb, totb, ob_ref))):
        hs = range(H)
        ws = [jnp.dot(wq[h], s_ref[d * H + h].astype(BF16), preferred_element_type=F32) for h in hs]
        v_new = [(u[h] - ws[h][:C]).astype(BF16) for h in hs]
        for h in hs:
            o_ref[:, h * dv:(h + 1) * dv] = ws[h][C:] + jnp.dot(qk[h], v_new[h], preferred_element_type=F32)
        upd = [lax.dot_general(kd[h], v_new[h], (((0,), (0,)), ((), ())), preferred_element_type=F32) for h in hs]
        for h in hs:
            s_ref[d * H + h] = s_ref[d * H + h] * tot[h] + upd[h]


def gdn_scan(wq, u, qk, kd, tot, n_ctx, T, Lc):
    B, n = wq.shape[0], wq.shape[1]
    C, H = GDN_CHUNK, GDN_HEADS
    N = B * T

    def bwd(s):
        return jnp.where(s < n_ctx, n_ctx - 1 - s, n - 1 - (s - n_ctx))

    def row_block(b, c):
        return jnp.where(c < n_ctx, (N + b * Lc) // C + c, (b * T) // C + c - n_ctx)

    fw = lambda r, w: pl.BlockSpec((None, None, H, r, w), lambda b, s: (b, s, 0, 0, 0))
    bw = lambda r, w: pl.BlockSpec((None, None, H, r, w), lambda b, s: (b, bwd(s), 1, 0, 0))
    shapes = ((2 * C, GDN_DK), (C, GDN_DV), (C, C), (C, GDN_DK), (1, LANES))
    o_shape = jax.ShapeDtypeStruct((B * n * C, H * GDN_DV), F32)
    return pl.pallas_call(
        _gdn_scan_kernel, grid=(B, n),
        in_specs=[fw(*s) for s in shapes] + [bw(*s) for s in shapes],
        out_specs=[pl.BlockSpec((C, H * GDN_DV), lambda b, s: (row_block(b, s), 0)),
                   pl.BlockSpec((C, H * GDN_DV), lambda b, s: (row_block(b, bwd(s)), 0))],
        out_shape=[o_shape, o_shape],
        scratch_shapes=[pltpu.VMEM((2 * H, GDN_DK, GDN_DV), F32)],
        compiler_params=_params("parallel", "arbitrary"))(wq, u, qk, kd, tot, wq, u, qk, kd, tot)


def kernel(x, c, ctx, c_ctx, w_mod, b_mod, norm_mix_g, norm_ffn_g, w_in, gdn_conv_w, gdn_a_log, gdn_dt_bias,
           gdn_norm_g, mla_q_norm_g, mla_kv_norm_g, mla_w_uq, mla_w_ukv, mla_qn_g, mla_kn_g, swa_qn_g, swa_kn_g,
           swa_sink, w_branch_a, w_branch_b, w_branch_c, w_out, ffn_w_gate, ffn_w_up, ffn_w_down, moe_router,
           moe_router_bias, moe_w_gate, moe_w_up, moe_w_down):
    B, T, D = x.shape
    Lc = ctx.shape[1]
    depth = w_mod.shape[0]
    N, Nc = B * T, B * Lc
    M = N + Nc
    TM = _pow2_tile(MAX_ROW_TILE, T, Nc)
    gran = min(TM, 256)
    tseq = _pow2_tile(256, T, Lc)
    assert T % Lc == 0 and Lc % SWA_BLOCK == 0 and Lc % GDN_CHUNK == 0
    H = GDN_HEADS
    nk = H * GDN_DK

    sizes = (nk, nk, H * GDN_DV, H * GDN_DV, 2 * H, 2 * H, MLA_Q_RANK, MLA_KV_RANK, MLA_ROPE,
             SWA_HEADS * SWA_HD, SWA_KV_HEADS * SWA_HD, SWA_KV_HEADS * SWA_HD, N_BRANCH * D)
    off = np.concatenate([[0], np.cumsum(sizes)])
    n_gdn = int(off[4])
    tn_gdn = _pow2_tile(512, n_gdn)
    tn_mix = 512
    tn_gate = _pow2_tile(512, D)

    rows_all = jnp.concatenate([x.reshape(N, D), ctx.reshape(Nc, D)], axis=0)
    tile_batch = np.minimum(np.arange(M // gran) * gran // T, B)
    tile_batch = np.where(np.arange(M // gran) * gran < N, tile_batch, B)
    cvec = jnp.concatenate([c, c_ctx[None, :], jnp.zeros((MOD_ROWS - B - 1, D), F32)], axis=0)
    cvec = jax.nn.silu(cvec)
    tab_mla = _rope_tables(T, Lc, MLA_ROPE)
    tab_swa = _rope_tables(T, 0, SWA_HD)
    n_ctx_chunks = Lc // GDN_CHUNK
    S = Lc + T

    for l in range(depth):
        need_ctx = l < depth - 1
        rows_out = M if need_ctx else N

        mod = matmul(cvec, w_mod, w_lead=l, tm=MOD_ROWS, tn=_pow2_tile(1024, 6 * D), out_dtype=F32,
                     epilogue="bias", bias=b_mod[l][None, :])
        mod = mod.reshape(MOD_ROWS, 6, D)[:B + 1]
        mod = jnp.pad(mod, ((0, 0), (0, MOD_ROWS - 6), (0, 0)))
        modt = mod[tile_batch]

        h = modulate(rows_all, norm_mix_g[l], modt, M, gran, shift_row=0, scale_row=1)
        w_l = w_in[l]
        seg = lambda a, b: w_l[:, off[a]:off[b]]
        zcols = lambda n: jnp.zeros((D, n), w_l.dtype)
        w_small = jnp.concatenate([seg(4, 6), zcols(LANES - 4 * H)], axis=1)
        w_mix = jnp.concatenate([seg(6, 7), seg(10, 11), seg(9, 10), seg(7, 8), seg(11, 12), seg(8, 9),
                                 zcols(MIX_WIDTH - MIX_KR - MLA_ROPE)], axis=1)
        zg = matmul(h, w_in, w_lead=l, ncols=n_gdn, tm=TM, tn=tn_gdn, out_dtype=F32)
        zs = matmul(h, w_small, tm=TM, tn=LANES, out_dtype=F32)
        zm = matmul(h, w_mix, tm=TM, tn=tn_mix, out_dtype=F32)

        wq_c, u_c, qk_c, kd_c, tot_c = gdn_prepare_chunks(zg, zs, gdn_conv_w[l], gdn_a_log[l], gdn_dt_bias[l], B, T, Lc)
        o_f, o_b2 = gdn_scan(wq_c, u_c, qk_c, kd_c, tot_c, n_ctx_chunks, T, Lc)

        mq = mla_project_q(zm, rows_out, mla_q_norm_g[l], mla_w_uq[l], mla_qn_g[l], tab_mla, B, T, Lc, tseq)
        mk, mv = mla_project_kv(zm, mla_kv_norm_g[l], mla_w_ukv[l], mla_kn_g[l], tab_mla, B, T, Lc, tseq)
        o_b = full_attention(mq, mk, mv, _pow2_tile(MAX_ROW_TILE, T), T, 0, S, 0)
        o_c = window_attention(zm, tab_swa, swa_qn_g[l], swa_kn_g[l], swa_sink[l], B, T, Lc, local=True)
        if need_ctx:
            o_b = jnp.concatenate([o_b, full_attention(mq, mk, mv, _pow2_tile(256, Lc), Lc, T, Lc, T // Lc)], axis=0)
            o_c = jnp.concatenate([o_c, window_attention(zm, None, swa_qn_g[l], swa_kn_g[l], swa_sink[l], B, T, Lc,
                                                         local=False)], axis=0)

        merged = merge_branches(h, seg(12, 13), o_f, o_b2, zg, 3 * nk, gdn_norm_g[l], o_b, o_c, w_branch_a, w_branch_b,
                                w_branch_c, l, rows_out, TM // 2, tn_gate)
        rows_new = matmul(merged, w_out, w_lead=l, tm=TM, tn=tn_gate, out_dtype=F32, epilogue="residual",
                          resid=rows_all, modt=modt, gran=gran, gate_row=2)

        i = l // 2
        if l % 2 == 0:
            h2 = modulate(rows_new, norm_ffn_g[l], modt, rows_out, gran, shift_row=3, scale_row=4)
            nblk = rows_out // TM
            F = ffn_w_gate.shape[-1]
            y = swiglu_grouped(h2, ffn_w_gate[:, None], ffn_w_up[:, None], ffn_w_down[:, None], i,
                               jnp.zeros((nblk,), jnp.int32), jnp.full((nblk,), 2, jnp.int32),
                               jnp.ones((rows_out, 1), F32), rows_out, TM, _pow2_tile(256, F))
        else:
            E = moe_w_gate.shape[1]
            rw = jnp.pad(moe_router[i], ((0, 0), (0, LANES - E)))
            rb = jnp.pad(moe_router_bias[i].astype(F32), (0, LANES - E))[None, :]
            h2, logits = modulate(rows_new, norm_ffn_g[l], modt, rows_out, gran, shift_row=3, scale_row=4,
                                  router=(rw, rb))
            top_logit, top_idx = lax.top_k(logits[:, :E], TOP_K)
            top_w = jax.nn.softmax(top_logit, axis=-1)
            flat_e = top_idx.reshape(-1)
            onehot = (flat_e[:, None] == jnp.arange(E)[None, :]).astype(jnp.int32)
            rank = jnp.take_along_axis(jnp.cumsum(onehot, axis=0) - onehot, flat_e[:, None], axis=1)[:, 0]
            counts = jnp.sum(onehot, axis=0)
            padded = (counts + TM - 1) // TM * TM
            pstart = jnp.cumsum(padded) - padded
            dest = pstart[flat_e] + rank
            nblk = -(-(rows_out * TOP_K) // TM) + E
            slots = nblk * TM
            src = jnp.zeros((slots,), jnp.int32).at[dest].set(jnp.arange(rows_out * TOP_K, dtype=jnp.int32) // TOP_K)
            w_slot = jnp.zeros((slots,), F32).at[dest].set(top_w.reshape(-1))
            blk_start = jnp.arange(nblk, dtype=jnp.int32) * TM
            ends = jnp.cumsum(padded)
            block_e = jnp.minimum(jnp.searchsorted(ends, blk_start, side="right"), E - 1).astype(jnp.int32)
            used = blk_start < ends[-1]
            last_e = block_e[jnp.maximum(jnp.sum(used.astype(jnp.int32)) - 1, 0)]
            block_e = jnp.where(used, block_e, last_e)
            rows_in_block = jnp.clip((pstart + counts)[block_e] - blk_start, 0, TM)
            half = TM // 2
            block_valid = jnp.where(used, (rows_in_block + half - 1) // half, 0).astype(jnp.int32)
            F = moe_w_gate.shape[-1]
            ys = swiglu_grouped(h2[src], moe_w_gate, moe_w_up, moe_w_down, i, block_e, block_valid,
                                w_slot[:, None], slots, TM, _pow2_tile(256, F))
            dest2 = dest.reshape(rows_out, TOP_K)
            y = ys[dest2[:, 0]] + ys[dest2[:, 1]]
        gate_f = modt[:, 5][:rows_out // gran]
        rows_ffn = (rows_new.reshape(rows_out // gran, gran, D) + gate_f[:, None, :] * y.reshape(rows_out // gran, gran, D))
        rows_all = rows_ffn.reshape(rows_out, D)

    return rows_all[:N].reshape(B, T, D)
```

```python
import functools

import jax
import jax.numpy as jnp
import numpy as np
from jax import lax
from jax.experimental import pallas as pl
from jax.experimental.pallas import tpu as pltpu

F32 = jnp.float32
BF16 = jnp.bfloat16

GRID_W = 64
EPS = 1e-6
ROPE_BASE = 10000.0
N_BRANCH = 3
GDN_HEADS = 8
GDN_DK = 128
GDN_DV = 128
GDN_CONV = 5
GDN_CHUNK = 64
MLA_HEADS = 8
MLA_Q_RANK = 768
MLA_KV_RANK = 512
MLA_NOPE = 128
MLA_ROPE = 64
MLA_V = 128
MLA_QK = MLA_NOPE + MLA_ROPE
MLA_SCALE = MLA_QK ** -0.5
SWA_HEADS = 8
SWA_KV_HEADS = 2
SWA_HD = 128
SWA_WINDOW = 128
SWA_BLOCK = 128
SWA_SCALE = SWA_HD ** -0.5
N_EXPERTS = 8
TOP_K = 2

LANES = 128
VMEM_LIMIT_BYTES = 56 * 1024 * 1024
MAX_ROW_TILE = 1024
MAX_COL_TILE = 1024
MAX_FFN_TILE = 512
MOD_ROWS = 8
NEG_BIG = -1e30
LOG2E = 1.4426950408889634
ATTN_ROW_GROUP = 256
MLA_QK_PAD = 2 * LANES

MIX_CQ, MIX_SK, MIX_SQ, MIX_CKV, MIX_SV, MIX_KR, MIX_WIDTH = 0, 768, 1024, 2048, 2560, 2816, 3072


def _params(*sem):
    return pltpu.CompilerParams(dimension_semantics=sem, vmem_limit_bytes=VMEM_LIMIT_BYTES)


def _pow2_tile(limit, *dims):
    t = 1
    while t * 2 <= limit and all(d % (t * 2) == 0 for d in dims):
        t *= 2
    return t


def _rms_rows(x, g):
    return x * lax.rsqrt(jnp.mean(x * x, axis=-1, keepdims=True) + EPS) * g


def _modulate_kernel(x_ref, g_ref, mod_ref, *rest, shift_row, scale_row, with_router):
    mod = mod_ref[0]
    h = _rms_rows(x_ref[...], g_ref[...]) * (1.0 + mod[scale_row:scale_row + 1]) + mod[shift_row:shift_row + 1]
    if with_router:
        rw_ref, rb_ref, h_ref, lg_ref = rest
        lg_ref[...] = jnp.dot(h.astype(BF16), rw_ref[...].astype(BF16), preferred_element_type=F32) + rb_ref[...]
    else:
        (h_ref,) = rest
    h_ref[...] = h.astype(h_ref.dtype)


def modulate(x, gain, modt, rows, gran, shift_row, scale_row, router=None):
    D = x.shape[1]
    tm = gran
    kern = functools.partial(_modulate_kernel, shift_row=shift_row, scale_row=scale_row,
                             with_router=router is not None)
    in_specs = [pl.BlockSpec((tm, D), lambda i: (i, 0)),
                pl.BlockSpec((1, D), lambda i: (0, 0)),
                pl.BlockSpec((1, MOD_ROWS, D), lambda i: (i, 0, 0))]
    args = [x, gain.reshape(1, D), modt]
    out_shape = [jax.ShapeDtypeStruct((rows, D), BF16)]
    out_specs = [pl.BlockSpec((tm, D), lambda i: (i, 0))]
    if router is not None:
        rw, rb = router
        in_specs += [pl.BlockSpec((D, LANES), lambda i: (0, 0)), pl.BlockSpec((1, LANES), lambda i: (0, 0))]
        args += [rw, rb]
        out_shape.append(jax.ShapeDtypeStruct((rows, LANES), F32))
        out_specs.append(pl.BlockSpec((tm, LANES), lambda i: (i, 0)))
    out = pl.pallas_call(kern, grid=(rows // tm,), in_specs=in_specs, out_specs=out_specs,
                         out_shape=out_shape, compiler_params=_params("parallel"))(*args)
    return out if router is not None else out[0]


def _mm_kernel(a_ref, w_ref, *rest, epilogue, gate_row):
    acc = jnp.dot(a_ref[...].astype(BF16), w_ref[...].astype(BF16), preferred_element_type=F32)
    if epilogue == "bias":
        b_ref, o_ref = rest
        acc = acc + b_ref[...]
    elif epilogue == "sigmoid":
        (o_ref,) = rest
        acc = jax.nn.sigmoid(acc)
    elif epilogue == "residual":
        x_ref, mod_ref, o_ref = rest
        acc = x_ref[...] + mod_ref[0][gate_row:gate_row + 1] * acc
    else:
        (o_ref,) = rest
    o_ref[...] = acc.astype(o_ref.dtype)


def matmul(a, w, *, rows=None, w_lead=None, col0=0, ncols=None, tm, tn, out_dtype, epilogue=None,
           bias=None, resid=None, modt=None, gran=None, gate_row=0):
    rows = a.shape[0] if rows is None else rows
    K = a.shape[1]
    ncols = w.shape[-1] if ncols is None else ncols
    assert rows % tm == 0 and ncols % tn == 0 and col0 % tn == 0
    cb = col0 // tn
    if w.ndim == 3:
        w_spec = pl.BlockSpec((None, K, tn), lambda i, j: (w_lead, 0, cb + j))
    else:
        w_spec = pl.BlockSpec((K, tn), lambda i, j: (0, cb + j))
    in_specs = [pl.BlockSpec((tm, K), lambda i, j: (i, 0)), w_spec]
    args = [a, w]
    if epilogue == "bias":
        in_specs.append(pl.BlockSpec((1, tn), lambda i, j: (0, j)))
        args.append(bias)
    elif epilogue == "residual":
        step = tm // gran
        in_specs += [pl.BlockSpec((tm, tn), lambda i, j: (i, j)),
                     pl.BlockSpec((1, MOD_ROWS, tn), lambda i, j: (i * step, 0, j))]
        args += [resid, modt]
    kern = functools.partial(_mm_kernel, epilogue=epilogue, gate_row=gate_row)
    return pl.pallas_call(kern, grid=(rows // tm, ncols // tn), in_specs=in_specs,
                          out_specs=pl.BlockSpec((tm, tn), lambda i, j: (i, j)),
                          out_shape=jax.ShapeDtypeStruct((rows, ncols), out_dtype),
                          compiler_params=_params("parallel", "arbitrary"))(*args)


def _merge_kernel(h_ref, of_ref, ob2_ref, og_ref, gn_ref, ob_ref, oc_ref, wga_ref, wgb_ref, wgc_ref,
                  wa_ref, wb_ref, wc_ref, o_ref, oa_scr):
    @pl.when(pl.program_id(1) == 0)
    def _():
        for h in range(GDN_HEADS):
            cs = slice(h * GDN_DV, (h + 1) * GDN_DV)
            o = _rms_rows(of_ref[:, cs] + ob2_ref[:, cs], gn_ref[...])
            oa_scr[:, cs] = (o * jax.nn.silu(og_ref[:, cs])).astype(BF16)

    hm = h_ref[...]
    gate = lambda w_ref: jax.nn.sigmoid(jnp.dot(hm, w_ref[...], preferred_element_type=F32))
    acc = gate(wga_ref) * jnp.dot(oa_scr[...], wa_ref[...], preferred_element_type=F32)
    acc += gate(wgb_ref) * jnp.dot(ob_ref[...], wb_ref[...], preferred_element_type=F32)
    acc += gate(wgc_ref) * jnp.dot(oc_ref[...], wc_ref[...], preferred_element_type=F32)
    o_ref[...] = acc.astype(o_ref.dtype)


def merge_branches(h, w_gates, o_f, o_b2, zg, og_col, gdn_g, ob, oc, wa, wb, wc, l, rows, tm, tn):
    D = wa.shape[-1]
    nj = D // tn
    na = GDN_HEADS * GDN_DV
    row = lambda w, cb=0: pl.BlockSpec((tm, w), lambda i, j: (i, cb))
    g_spec = lambda k: pl.BlockSpec((D, tn), lambda i, j: (0, k * nj + j))
    wa, wb, wc = (w[l].astype(BF16) for w in (wa, wb, wc))
    w_gates = w_gates.astype(BF16)
    w_spec = lambda w: pl.BlockSpec((w.shape[0], tn), lambda i, j: (0, j))
    return pl.pallas_call(
        _merge_kernel, grid=(rows // tm, nj),
        in_specs=[row(D), row(na), row(na), row(na, og_col // na), pl.BlockSpec((1, GDN_DV), lambda i, j: (0, 0)),
                  row(ob.shape[1]), row(oc.shape[1]), g_spec(0), g_spec(1), g_spec(2),
                  w_spec(wa), w_spec(wb), w_spec(wc)],
        out_specs=pl.BlockSpec((tm, tn), lambda i, j: (i, j)),
        out_shape=jax.ShapeDtypeStruct((rows, D), BF16),
        scratch_shapes=[pltpu.VMEM((tm, na), BF16)],
        compiler_params=_params("parallel", "arbitrary"))(
            h, o_f, o_b2, zg, gdn_g.reshape(1, GDN_DV).astype(F32), ob, oc, w_gates, w_gates, w_gates, wa, wb, wc)


def _swiglu_kernel(be_ref, bv_ref, x_ref, wg_ref, wu_ref, wd_ref, rs_ref, o_ref):
    i, f = pl.program_id(0), pl.program_id(1)

    @pl.when(f == 0)
    def _():
        o_ref[...] = jnp.zeros_like(o_ref)

    def accumulate(n):
        x = x_ref[:n, :]
        g = jnp.dot(x, wg_ref[...].astype(BF16), preferred_element_type=F32)
        u = jnp.dot(x, wu_ref[...].astype(BF16), preferred_element_type=F32)
        h = (jax.nn.silu(g) * u).astype(BF16)
        o_ref[:n, :] += jnp.dot(h, wd_ref[...].astype(BF16), preferred_element_type=F32)

        @pl.when(f == pl.num_programs(1) - 1)
        def _():
            o_ref[:n, :] = o_ref[:n, :] * rs_ref[:n, :]

    tm = x_ref.shape[0]
    pl.when(bv_ref[i] == 2)(functools.partial(accumulate, tm))
    pl.when(bv_ref[i] == 1)(functools.partial(accumulate, tm // 2))


def swiglu_grouped(x, wg, wu, wd, l, block_e, block_valid, row_scale, rows, tm, tf):
    D = x.shape[1]
    F = wg.shape[-1]
    nf = F // tf
    assert rows % tm == 0 and F % tf == 0

    def f_idx(i, f, bv):
        return jnp.where(bv[i] > 0, f, nf - 1)

    grid_spec = pltpu.PrefetchScalarGridSpec(
        num_scalar_prefetch=2, grid=(rows // tm, nf),
        in_specs=[pl.BlockSpec((tm, D), lambda i, f, be, bv: (i, 0)),
                  pl.BlockSpec((None, None, D, tf), lambda i, f, be, bv: (l, be[i], 0, f_idx(i, f, bv))),
                  pl.BlockSpec((None, None, D, tf), lambda i, f, be, bv: (l, be[i], 0, f_idx(i, f, bv))),
                  pl.BlockSpec((None, None, tf, D), lambda i, f, be, bv: (l, be[i], f_idx(i, f, bv), 0)),
                  pl.BlockSpec((tm, 1), lambda i, f, be, bv: (i, 0))],
        out_specs=pl.BlockSpec((tm, D), lambda i, f, be, bv: (i, 0)))
    return pl.pallas_call(_swiglu_kernel, grid_spec=grid_spec,
                          out_shape=jax.ShapeDtypeStruct((rows, D), F32),
                          compiler_params=_params("parallel", "arbitrary"))(
                              block_e, block_valid, x, wg, wu, wd, row_scale)


def _rope_tables(T, extra, rot_dim):
    n = rot_dim // 4
    t = jnp.arange(T)
    inv = jnp.power(ROPE_BASE, -jnp.arange(n, dtype=F32) / n)
    ar = (t // GRID_W).astype(F32)[:, None] * inv
    ac = (t % GRID_W).astype(F32)[:, None] * inv
    z = jnp.zeros((T, n), F32)
    pad = lambda a, fill: jnp.concatenate([a, jnp.full((T, LANES - 4 * n), fill, F32)], axis=1)
    c = pad(jnp.concatenate([jnp.cos(ar), jnp.cos(ar), jnp.cos(ac), jnp.cos(ac)], axis=1), 1.0)
    a = pad(jnp.concatenate([-jnp.sin(ar), z, -jnp.sin(ac), z], axis=1), 0.0)
    b = pad(jnp.concatenate([z, jnp.sin(ar), z, jnp.sin(ac)], axis=1), 0.0)
    tab = jnp.concatenate([c, a, b], axis=1)
    ident = jnp.concatenate([jnp.ones((extra, LANES), F32), jnp.zeros((extra, 2 * LANES), F32)], axis=1)
    return jnp.concatenate([tab, ident], axis=0)


def _apply_rope(x, tab, half):
    return (x * tab[:, :LANES] + pltpu.roll(x, LANES - half, 1) * tab[:, LANES:2 * LANES]
            + pltpu.roll(x, half, 1) * tab[:, 2 * LANES:])


def _mla_q_kernel(c_ref, g_ref, w_ref, hg_ref, tab_ref, q_ref):
    a = _rms_rows(c_ref[...], g_ref[...]).astype(BF16)
    tab, hg = tab_ref[...], hg_ref[...]
    for h in range(MLA_HEADS):
        acc = jnp.dot(a, w_ref[:, h * MLA_QK_PAD:(h + 1) * MLA_QK_PAD], preferred_element_type=F32)
        y = acc * lax.rsqrt(jnp.sum(acc * acc, axis=-1, keepdims=True) * (1.0 / MLA_QK) + EPS) * hg
        q_ref[h, :, :LANES] = y[:, :LANES].astype(q_ref.dtype)
        q_ref[h, :, LANES:] = _apply_rope(y[:, LANES:], tab, MLA_ROPE // 4).astype(q_ref.dtype)


def _mla_kv_kernel(c_ref, kr_ref, g_ref, w_ref, hg_ref, tab_ref, k_ref, v_ref):
    a = _rms_rows(c_ref[...], g_ref[...]).astype(BF16)
    tab, hg = tab_ref[...], hg_ref[...]
    kr = kr_ref[...]
    kr_ss = jnp.sum(kr * kr, axis=-1, keepdims=True)
    width = MLA_NOPE + MLA_V
    for h in range(MLA_HEADS):
        acc = jnp.dot(a, w_ref[:, h * width:(h + 1) * width], preferred_element_type=F32)
        kn = acc[:, :LANES]
        r = lax.rsqrt((jnp.sum(kn * kn, axis=-1, keepdims=True) + kr_ss) * (1.0 / MLA_QK) + EPS)
        k_ref[h, :, :LANES] = (kn * r * hg[:, :LANES]).astype(k_ref.dtype)
        k_ref[h, :, LANES:] = _apply_rope(kr * r * hg[:, LANES:], tab, MLA_ROPE // 4).astype(k_ref.dtype)
        v_ref[h] = acc[:, LANES:].astype(v_ref.dtype)


def _seq_pos(i, tm, T, Lc, N):
    nl, nc, nlt = T // tm, Lc // tm, N // tm
    k = i - nlt
    return jnp.where(i < nlt, i // nl, k // nc), jnp.where(i < nlt, i % nl, nl + k % nc)


def mla_project_q(zm, rows, q_norm_g, w_uq_l, qn_g, tab, B, T, Lc, tm):
    N = B * T
    Lq = T + (Lc if rows > N else 0)
    H, R = MLA_HEADS, MLA_Q_RANK
    wp = jnp.pad(w_uq_l.reshape(R, H, MLA_QK), ((0, 0), (0, 0), (0, MLA_QK_PAD - MLA_QK)))
    wp = wp.reshape(R, H * MLA_QK_PAD).astype(BF16)
    hg = jnp.pad(qn_g.astype(F32) * (MLA_SCALE * LOG2E), (0, MLA_QK_PAD - MLA_QK))[None, :]
    pos = lambda i: _seq_pos(i, tm, T, Lc, N)
    return pl.pallas_call(
        _mla_q_kernel, grid=(rows // tm,),
        in_specs=[pl.BlockSpec((tm, R), lambda i: (i, MIX_CQ // R)),
                  pl.BlockSpec((1, R), lambda i: (0, 0)),
                  pl.BlockSpec((R, H * MLA_QK_PAD), lambda i: (0, 0)),
                  pl.BlockSpec((1, MLA_QK_PAD), lambda i: (0, 0)),
                  pl.BlockSpec((tm, 3 * LANES), lambda i: (pos(i)[1], 0))],
        out_specs=pl.BlockSpec((None, H, tm, MLA_QK_PAD), lambda i: (pos(i)[0], 0, pos(i)[1], 0)),
        out_shape=jax.ShapeDtypeStruct((B, H, Lq, MLA_QK_PAD), BF16),
        compiler_params=_params("parallel"))(zm, q_norm_g.reshape(1, R).astype(F32), wp, hg, tab)


def mla_project_kv(zm, kv_norm_g, w_ukv_l, kn_g, tab, B, T, Lc, tm):
    M = zm.shape[0]
    N = B * T
    H, R = MLA_HEADS, MLA_KV_RANK
    hg = jnp.pad(kn_g.astype(F32), (0, MLA_QK_PAD - MLA_QK))[None, :]
    pos = lambda i: _seq_pos(i, tm, T, Lc, N)
    o_spec = lambda w: pl.BlockSpec((None, H, tm, w), lambda i: (pos(i)[0], 0, pos(i)[1], 0))
    return pl.pallas_call(
        _mla_kv_kernel, grid=(M // tm,),
        in_specs=[pl.BlockSpec((tm, R), lambda i: (i, MIX_CKV // R)),
                  pl.BlockSpec((tm, LANES), lambda i: (i, MIX_KR // LANES)),
                  pl.BlockSpec((1, R), lambda i: (0, 0)),
                  pl.BlockSpec((R, H * (MLA_NOPE + MLA_V)), lambda i: (0, 0)),
                  pl.BlockSpec((1, MLA_QK_PAD), lambda i: (0, 0)),
                  pl.BlockSpec((tm, 3 * LANES), lambda i: (pos(i)[1], 0))],
        out_specs=[o_spec(MLA_QK_PAD), o_spec(MLA_V)],
        out_shape=[jax.ShapeDtypeStruct((B, H, T + Lc, MLA_QK_PAD), BF16),
                   jax.ShapeDtypeStruct((B, H, T + Lc, MLA_V), BF16)],
        compiler_params=_params("parallel"))(zm, zm, kv_norm_g.reshape(1, R).astype(F32), w_ukv_l.astype(BF16), hg, tab)


def _attn_kernel(q_ref, k_ref, v_ref, o_ref, *, sub):
    for r0 in range(0, q_ref.shape[0], sub):
        rows = slice(r0, r0 + sub)
        s = lax.dot_general(q_ref[rows, :], k_ref[...], (((1,), (1,)), ((), ())), preferred_element_type=F32)
        p = jnp.exp2(s - jnp.max(s, axis=-1, keepdims=True))
        den = jnp.sum(p, axis=-1, keepdims=True)
        o = jnp.dot(p.astype(BF16), v_ref[...], preferred_element_type=F32)
        o_ref[rows, :] = (o * (1.0 / den)).astype(o_ref.dtype)


def full_attention(q, k, v, tq, n_q, q_off, kl, k_blk):
    B, H, _, d = q.shape
    e = v.shape[3]
    nt = n_q // tq
    qo = q_off // tq
    return pl.pallas_call(
        functools.partial(_attn_kernel, sub=min(tq, ATTN_ROW_GROUP)), grid=(B, H, nt),
        in_specs=[pl.BlockSpec((None, None, tq, d), lambda b, h, i: (b, h, qo + i, 0)),
                  pl.BlockSpec((None, None, kl, d), lambda b, h, i: (b, h, k_blk, 0)),
                  pl.BlockSpec((None, None, kl, e), lambda b, h, i: (b, h, k_blk, 0))],
        out_specs=pl.BlockSpec((tq, e), lambda b, h, i: (b * nt + i, h)),
        out_shape=jax.ShapeDtypeStruct((B * n_q, H * e), BF16),
        compiler_params=_params("parallel", "parallel", "arbitrary"))(q, k, v)


def _swa_kernel(*refs, local, n_blocks):
    if local:
        (q_ref, kp_ref, kc_ref, kn_ref, vp_ref, vc_ref, vn_ref, kx_ref, vx_ref, tp_ref, tc_ref, tn_ref,
         qg_ref, kg_ref, sink_ref, o_ref) = refs
    else:
        q_ref, kx_ref, vx_ref, qg_ref, kg_ref, sink_ref, o_ref = refs
    n = pl.program_id(1)
    Bk, d = SWA_BLOCK, SWA_HD
    Lc = kx_ref.shape[0]
    R = SWA_HEADS // SWA_KV_HEADS
    half = SWA_HD // 4

    def prep(x, g, tab):
        y = _rms_rows(x, g)
        return y if tab is None else _apply_rope(y, tab, half)

    if local:
        iq = lax.broadcasted_iota(jnp.int32, (Bk, 3 * Bk), 0)
        jk = lax.broadcasted_iota(jnp.int32, (Bk, 3 * Bk), 1)
        valid = jnp.abs(iq + Bk - jk) <= SWA_WINDOW
        valid = valid & ((jk >= Bk) | (n > 0)) & ((jk < 2 * Bk) | (n < n_blocks - 1))
        bias = jnp.where(valid, 0.0, NEG_BIG).astype(F32)
        bias = jnp.concatenate([bias, jnp.zeros((Bk, Lc), F32)], axis=1)
        tp, tc, tn = tp_ref[...], tc_ref[...], tn_ref[...]
    else:
        tc = None
    kg, qg = kg_ref[...], qg_ref[...]
    kcat, vcat = [], []
    for g in range(SWA_KV_HEADS):
        cs = slice(g * d, (g + 1) * d)
        kx = prep(kx_ref[:, cs], kg, None).astype(BF16)
        if local:
            kcat.append(jnp.concatenate([prep(kp_ref[:, cs], kg, tp).astype(BF16), prep(kc_ref[:, cs], kg, tc).astype(BF16),
                                         prep(kn_ref[:, cs], kg, tn).astype(BF16), kx], axis=0))
            vcat.append(jnp.concatenate([vp_ref[:, cs].astype(BF16), vc_ref[:, cs].astype(BF16),
                                         vn_ref[:, cs].astype(BF16), vx_ref[:, cs].astype(BF16)], axis=0))
        else:
            kcat.append(kx)
            vcat.append(vx_ref[:, cs].astype(BF16))
    hs = range(SWA_HEADS)
    qh = [prep(q_ref[:, h * d:(h + 1) * d], qg, tc).astype(BF16) for h in hs]
    s = [lax.dot_general(qh[h], kcat[h // R], (((1,), (1,)), ((), ())), preferred_element_type=F32) for h in hs]
    if local:
        s = [sh + bias for sh in s]
    sink = [sink_ref[h:h + 1, 0:1] for h in hs]
    m = [jnp.maximum(jnp.max(s[h], axis=-1, keepdims=True), sink[h]) for h in hs]
    p = [jnp.exp2(s[h] - m[h]).astype(BF16) for h in hs]
    vone = [jnp.concatenate([v, jnp.ones_like(v)], axis=1) for v in vcat]
    for h in hs:
        o = jnp.dot(p[h], vone[h // R], preferred_element_type=F32)
        den = o[:, d:] + jnp.exp2(sink[h] - m[h])
        o_ref[:, h * d:(h + 1) * d] = (o[:, :d] * (1.0 / den)).astype(o_ref.dtype)


def window_attention(zm, tab, qn_g, kn_g, sink, B, T, Lc, local):
    N = B * T
    Bk = SWA_BLOCK
    Q, KV = SWA_HEADS * SWA_HD, SWA_KV_HEADS * SWA_HD
    n_q = T if local else Lc
    nb = n_q // Bk
    row0 = 0 if local else N // Bk
    sink_b = jnp.broadcast_to(sink.astype(F32)[:, None], (SWA_HEADS, LANES))
    q_spec = pl.BlockSpec((Bk, Q), lambda b, n: (row0 + b * nb + n, MIX_SQ // Q))
    kx_spec = pl.BlockSpec((Lc, KV), lambda b, n: (N // Lc + b, MIX_SK // KV))
    vx_spec = pl.BlockSpec((Lc, KV), lambda b, n: (N // Lc + b, MIX_SV // KV))
    g_spec = pl.BlockSpec((1, SWA_HD), lambda b, n: (0, 0))
    s_spec = pl.BlockSpec((SWA_HEADS, LANES), lambda b, n: (0, 0))
    gains = (qn_g.reshape(1, SWA_HD).astype(F32) * (SWA_SCALE * LOG2E), kn_g.reshape(1, SWA_HD).astype(F32),
             sink_b * LOG2E)
    if local:
        pv = lambda n: jnp.maximum(n - 1, 0)
        nx = lambda n: jnp.minimum(n + 1, nb - 1)
        kv_spec = lambda f, col: pl.BlockSpec((Bk, KV), lambda b, n: (b * nb + f(n), col // KV))
        t_spec = lambda f: pl.BlockSpec((Bk, 3 * LANES), lambda b, n: (f(n), 0))
        same = lambda n: n
        in_specs = [q_spec, kv_spec(pv, MIX_SK), kv_spec(same, MIX_SK), kv_spec(nx, MIX_SK),
                    kv_spec(pv, MIX_SV), kv_spec(same, MIX_SV), kv_spec(nx, MIX_SV), kx_spec, vx_spec,
                    t_spec(pv), t_spec(same), t_spec(nx), g_spec, g_spec, s_spec]
        args = (zm,) * 9 + (tab,) * 3 + gains
    else:
        in_specs = [q_spec, kx_spec, vx_spec, g_spec, g_spec, s_spec]
        args = (zm,) * 3 + gains
    kern = functools.partial(_swa_kernel, local=local, n_blocks=nb)
    return pl.pallas_call(kern, grid=(B, nb), in_specs=in_specs,
                          out_specs=pl.BlockSpec((Bk, Q), lambda b, n: (b * nb + n, 0)),
                          out_shape=jax.ShapeDtypeStruct((B * n_q, Q), BF16),
                          compiler_params=_params("parallel", "arbitrary"))(*args)


def _split3(x):
    hi = x.astype(BF16)
    r1 = x - hi.astype(F32)
    mid = r1.astype(BF16)
    lo = (r1 - mid.astype(F32)).astype(BF16)
    return hi, mid, lo


def _dot_bf16(a, b):
    return jnp.dot(a.astype(BF16), b.astype(BF16), preferred_element_type=F32)


def _dot_nt(a, b):
    return lax.dot_general(a.astype(BF16), b.astype(BF16), (((1,), (1,)), ((), ())), preferred_element_type=F32)


def _gdn_prep_kernel(zp_ref, zc_ref, zn_ref, zs_ref, cw_ref, ad_ref, wq_ref, u_ref, qk_ref, kd_ref, tot_ref,
                     ext_scr, qkv_scr, *, n_ctx, n_chunks):
    C, H, dk, dv = GDN_CHUNK, GDN_HEADS, GDN_DK, GDN_DV
    P = 2 * C
    nk = H * dk
    halo = 8
    pad = (GDN_CONV - 1) // 2

    ch = pl.program_id(1)
    has_prev = jnp.where((ch == 0) | (ch == n_ctx), 0.0, 1.0)
    has_next = jnp.where((ch == n_ctx - 1) | (ch == n_chunks - 1), 0.0, 1.0)
    ext_scr[0:halo, :] = zp_ref[C - halo:C, :] * has_prev
    ext_scr[halo:halo + C, :] = zc_ref[...]
    ext_scr[halo + C:, :] = zn_ref[0:halo, :] * has_next
    for part in range(3):
        cs = slice(part * nk, (part + 1) * nk)
        y = sum(ext_scr[halo - pad + j:halo - pad + j + C, cs] * cw_ref[j:j + 1, cs] for j in range(GDN_CONV))
        qkv_scr[:, cs] = y * jax.nn.sigmoid(y)
    for h in range(H):
        for part, scale in ((0, dk ** -0.5), (1, 1.0)):
            cs = slice(part * nk + h * dk, part * nk + (h + 1) * dk)
            y = qkv_scr[:, cs]
            qkv_scr[:, cs] = y * (lax.rsqrt(jnp.sum(y * y, axis=-1, keepdims=True) + EPS) * scale)
    q_ref, k_ref, v_ref = qkv_scr.at[:, 0:nk], qkv_scr.at[:, nk:2 * nk], qkv_scr.at[:, 2 * nk:3 * nk]

    zs = zs_ref[...]
    beta = jax.nn.sigmoid(zs)
    a = pltpu.roll(zs, LANES - 2 * H, 1) + ad_ref[1:2, :]
    g = -ad_ref[0:1, :] * (jnp.maximum(a, 0.0) + jnp.log(1.0 + jnp.exp(-jnp.abs(a))))

    ii = lax.broadcasted_iota(jnp.int32, (C, C), 0)
    jj = lax.broadcasted_iota(jnp.int32, (C, C), 1)
    low = (ii >= jj).astype(BF16)
    upp = (ii <= jj).astype(BF16)
    r = lax.broadcasted_iota(jnp.int32, (P, P), 0)
    c = lax.broadcasted_iota(jnp.int32, (P, P), 1)
    rq = jnp.where(r < C, 0, 1)
    cq = jnp.where(c < C, 0, 1)
    ahead = (r - c) * (1 - 2 * rq)
    causal = (rq == cq) & (ahead >= 0)
    strict = (rq == cq) & (ahead > 0)
    eye = (r == c).astype(F32)
    row_fwd = lax.broadcasted_iota(jnp.int32, (P, 1), 0) < C

    g3 = _split3(g)
    tri = jnp.concatenate([low, upp], axis=0)
    dcol_all = sum(jnp.dot(tri, p, preferred_element_type=F32) for p in g3)
    tdot = lambda p, t: lax.dot_general(p, t, (((0,), (0,)), ((), ())), preferred_element_type=F32)
    drow_all = jnp.concatenate([sum(tdot(p, upp) for p in g3)[:H], sum(tdot(p, low) for p in g3)[H:2 * H]],
                               axis=1)
    beta2 = jnp.concatenate([beta, beta], axis=0)

    def pair_col(a, h):
        return jnp.where(row_fwd, a[:, h:h + 1], a[:, H + h:H + h + 1])

    stack = lambda ref, h, w: jnp.concatenate([ref[:, h * w:(h + 1) * w]] * 2, axis=0)

    group = 8
    for h0 in range(0, H, group):
        hs = range(h0, h0 + group)
        dc = [pair_col(dcol_all, h) for h in hs]
        seg = [jnp.exp(jnp.where(causal, dc[a] - drow_all[h:h + 1, :], NEG_BIG)) for a, h in enumerate(hs)]
        b2 = [pair_col(beta2, h) for h in hs]
        kk = [_dot_nt(stack(k_ref, h, dk) * b2[a], stack(k_ref, h, dk)) for a, h in enumerate(hs)]
        pw = [jnp.where(strict, kk[a] * seg[a], 0.0) for a in range(group)]
        inv = [eye - m for m in pw]
        k = 2
        while k < C:
            pw = [_dot_bf16(m, m) for m in pw]
            inv = [t + _dot_bf16(t, m) for t, m in zip(inv, pw)]
            k *= 2
        for a, h in enumerate(hs):
            k2, q2, v2 = stack(k_ref, h, dk), stack(q_ref, h, dk), stack(v_ref, h, dv)
            ecol = jnp.exp(dc[a])
            rhs = jnp.concatenate([k2 * (b2[a] * ecol), v2 * b2[a]], axis=1)
            sol = rhs + _dot_bf16(inv[a] - eye, rhs)
            qk = _dot_nt(q2, k2) * seg[a]
            dlast = jnp.where(row_fwd, dc[a][C - 1:C], dc[a][C:C + 1])
            qd = q2 * ecol
            kd = k2 * jnp.exp(dlast - dc[a])
            tot = jnp.exp(dlast)
            for d, rs in enumerate((slice(0, C), slice(C, P))):
                hd = d * H + h
                wq_ref[hd, :C, :] = sol[rs, :dk].astype(wq_ref.dtype)
                wq_ref[hd, C:, :] = qd[rs].astype(wq_ref.dtype)
                u_ref[hd] = sol[rs, dk:]
                qk_ref[hd] = qk[rs, d * C:(d + 1) * C].astype(qk_ref.dtype)
                kd_ref[hd] = kd[rs].astype(kd_ref.dtype)
                tot_ref[hd] = jnp.broadcast_to(tot[d * C:d * C + 1], (1, LANES))


def gdn_prepare_chunks(zg, zs, conv_w, a_log, dt_bias, B, T, Lc):
    C, H = GDN_CHUNK, GDN_HEADS
    N = B * T
    n_ctx, n = Lc // C, (T + Lc) // C
    width = 3 * H * GDN_DK
    last = zg.shape[0] // C - 1

    def row_block(b, c):
        return jnp.where(c < n_ctx, (N + b * Lc) // C + c, (b * T) // C + c - n_ctx)

    z_spec = lambda d: pl.BlockSpec((C, width), lambda b, c: (jnp.clip(row_block(b, c) + d, 0, last), 0))
    lanes = lambda a: jnp.pad(a.astype(F32), (0, LANES - 2 * H))
    ad = jnp.zeros((8, LANES), F32).at[0].set(lanes(jnp.exp(a_log.astype(F32)))).at[1].set(lanes(dt_bias))
    per = lambda r, w: pl.BlockSpec((None, None, 2 * H, r, w), lambda b, c: (b, c, 0, 0, 0))
    shp = lambda r, w, dt: jax.ShapeDtypeStruct((B, n, 2 * H, r, w), dt)
    return pl.pallas_call(
        functools.partial(_gdn_prep_kernel, n_ctx=n_ctx, n_chunks=n), grid=(B, n),
        in_specs=[z_spec(-1), z_spec(0), z_spec(1),
                  pl.BlockSpec((C, LANES), lambda b, c: (row_block(b, c), 0)),
                  pl.BlockSpec((GDN_CONV, width), lambda b, c: (0, 0)),
                  pl.BlockSpec((8, LANES), lambda b, c: (0, 0))],
        out_specs=[per(2 * C, GDN_DK), per(C, GDN_DV), per(C, C), per(C, GDN_DK), per(1, LANES)],
        out_shape=[shp(2 * C, GDN_DK, BF16), shp(C, GDN_DV, F32), shp(C, C, BF16), shp(C, GDN_DK, BF16),
                   shp(1, LANES, F32)],
        scratch_shapes=[pltpu.VMEM((C + 16, width), F32), pltpu.VMEM((C, width), F32)],
        compiler_params=_params("parallel", "parallel"))(
            zg, zg, zg, zs, conv_w.reshape(GDN_CONV, width).astype(F32), ad)


def _gdn_scan_kernel(wqf, uf, qkf, kdf, totf, wqb, ub, qkb, kdb, totb, of_ref, ob_ref, s_ref):
    C, H, dv = GDN_CHUNK, GDN_HEADS, GDN_DV

    @pl.when(pl.program_id(1) == 0)
    def _():
        s_ref[...] = jnp.zeros_like(s_ref)

    for d, (wq, u, qk, kd, tot, o_ref) in enumerate(((wqf, uf, qkf, kdf, totf, of_ref),
                                                     (wqb, ub, qkb, kdb, totb, ob_ref))):
        hs = range(H)
        ws = [jnp.dot(wq[h], s_ref[d * H + h].astype(BF16), preferred_element_type=F32) for h in hs]
        v_new = [(u[h] - ws[h][:C]).astype(BF16) for h in hs]
        for h in hs:
            o_ref[:, h * dv:(h + 1) * dv] = ws[h][C:] + jnp.dot(qk[h], v_new[h], preferred_element_type=F32)
        upd = [lax.dot_general(kd[h], v_new[h], (((0,), (0,)), ((), ())), preferred_element_type=F32) for h in hs]
        for h in hs:
            s_ref[d * H + h] = s_ref[d * H + h] * tot[h] + upd[h]


def gdn_scan(wq, u, qk, kd, tot, n_ctx, T, Lc):
    B, n = wq.shape[0], wq.shape[1]
    C, H = GDN_CHUNK, GDN_HEADS
    N = B * T

    def bwd(s):
        return jnp.where(s < n_ctx, n_ctx - 1 - s, n - 1 - (s - n_ctx))

    def row_block(b, c):
        return jnp.where(c < n_ctx, (N + b * Lc) // C + c, (b * T) // C + c - n_ctx)

    fw = lambda r, w: pl.BlockSpec((None, None, H, r, w), lambda b, s: (b, s, 0, 0, 0))
    bw = lambda r, w: pl.BlockSpec((None, None, H, r, w), lambda b, s: (b, bwd(s), 1, 0, 0))
    shapes = ((2 * C, GDN_DK), (C, GDN_DV), (C, C), (C, GDN_DK), (1, LANES))
    o_shape = jax.ShapeDtypeStruct((B * n * C, H * GDN_DV), F32)
    return pl.pallas_call(
        _gdn_scan_kernel, grid=(B, n),
        in_specs=[fw(*s) for s in shapes] + [bw(*s) for s in shapes],
        out_specs=[pl.BlockSpec((C, H * GDN_DV), lambda b, s: (row_block(b, s), 0)),
                   pl.BlockSpec((C, H * GDN_DV), lambda b, s: (row_block(b, bwd(s)), 0))],
        out_shape=[o_shape, o_shape],
        scratch_shapes=[pltpu.VMEM((2 * H, GDN_DK, GDN_DV), F32)],
        compiler_params=_params("parallel", "arbitrary"))(wq, u, qk, kd, tot, wq, u, qk, kd, tot)


def kernel(x, c, ctx, c_ctx, w_mod, b_mod, norm_mix_g, norm_ffn_g, w_in, gdn_conv_w, gdn_a_log, gdn_dt_bias,
           gdn_norm_g, mla_q_norm_g, mla_kv_norm_g, mla_w_uq, mla_w_ukv, mla_qn_g, mla_kn_g, swa_qn_g, swa_kn_g,
           swa_sink, w_branch_a, w_branch_b, w_branch_c, w_out, ffn_w_gate, ffn_w_up, ffn_w_down, moe_router,
           moe_router_bias, moe_w_gate, moe_w_up, moe_w_down):
    B, T, D = x.shape
    Lc = ctx.shape[1]
    depth = w_mod.shape[0]
    N, Nc = B * T, B * Lc
    M = N + Nc
    TM = _pow2_tile(MAX_ROW_TILE, T, Nc)
    gran = min(TM, 256)
    tseq = _pow2_tile(256, T, Lc)
    assert T % Lc == 0 and Lc % SWA_BLOCK == 0 and Lc % GDN_CHUNK == 0
    H = GDN_HEADS
    nk = H * GDN_DK

    sizes = (nk, nk, H * GDN_DV, H * GDN_DV, 2 * H, 2 * H, MLA_Q_RANK, MLA_KV_RANK, MLA_ROPE,
             SWA_HEADS * SWA_HD, SWA_KV_HEADS * SWA_HD, SWA_KV_HEADS * SWA_HD, N_BRANCH * D)
    off = np.concatenate([[0], np.cumsum(sizes)])
    n_gdn = int(off[4])
    tn_gdn = _pow2_tile(MAX_COL_TILE, n_gdn)
    tn_mix = _pow2_tile(MAX_COL_TILE, MIX_WIDTH)
    tn_out = _pow2_tile(MAX_COL_TILE, D)
    tn_gate = _pow2_tile(512, D)

    rows_all = jnp.concatenate([x.reshape(N, D), ctx.reshape(Nc, D)], axis=0)
    tile_batch = np.minimum(np.arange(M // gran) * gran // T, B)
    tile_batch = np.where(np.arange(M // gran) * gran < N, tile_batch, B)
    cvec = jnp.concatenate([c, c_ctx[None, :], jnp.zeros((MOD_ROWS - B - 1, D), F32)], axis=0)
    cvec = jax.nn.silu(cvec)
    tab_mla = _rope_tables(T, Lc, MLA_ROPE)
    tab_swa = _rope_tables(T, 0, SWA_HD)
    n_ctx_chunks = Lc // GDN_CHUNK
    S = Lc + T

    for l in range(depth):
        need_ctx = l < depth - 1
        rows_out = M if need_ctx else N

        mod = matmul(cvec, w_mod, w_lead=l, tm=MOD_ROWS, tn=_pow2_tile(1024, 6 * D), out_dtype=F32,
                     epilogue="bias", bias=b_mod[l][None, :])
        mod = mod.reshape(MOD_ROWS, 6, D)[:B + 1]
        mod = jnp.pad(mod, ((0, 0), (0, MOD_ROWS - 6), (0, 0)))
        modt = mod[tile_batch]

        h = modulate(rows_all, norm_mix_g[l], modt, M, gran, shift_row=0, scale_row=1)
        w_l = w_in[l]
        seg = lambda a, b: w_l[:, off[a]:off[b]]
        zcols = lambda n: jnp.zeros((D, n), w_l.dtype)
        w_small = jnp.concatenate([seg(4, 6), zcols(LANES - 4 * H)], axis=1)
        w_mix = jnp.concatenate([seg(6, 7), seg(10, 11), seg(9, 10), seg(7, 8), seg(11, 12), seg(8, 9),
                                 zcols(MIX_WIDTH - MIX_KR - MLA_ROPE)], axis=1)
        zg = matmul(h, w_in, w_lead=l, ncols=n_gdn, tm=TM, tn=tn_gdn, out_dtype=F32)
        zs = matmul(h, w_small, tm=TM, tn=LANES, out_dtype=F32)
        zm = matmul(h, w_mix, tm=TM, tn=tn_mix, out_dtype=F32)

        wq_c, u_c, qk_c, kd_c, tot_c = gdn_prepare_chunks(zg, zs, gdn_conv_w[l], gdn_a_log[l], gdn_dt_bias[l], B, T, Lc)
        o_f, o_b2 = gdn_scan(wq_c, u_c, qk_c, kd_c, tot_c, n_ctx_chunks, T, Lc)

        mq = mla_project_q(zm, rows_out, mla_q_norm_g[l], mla_w_uq[l], mla_qn_g[l], tab_mla, B, T, Lc, tseq)
        mk, mv = mla_project_kv(zm, mla_kv_norm_g[l], mla_w_ukv[l], mla_kn_g[l], tab_mla, B, T, Lc, tseq)
        o_b = full_attention(mq, mk, mv, _pow2_tile(MAX_ROW_TILE, T), T, 0, S, 0)
        o_c = window_attention(zm, tab_swa, swa_qn_g[l], swa_kn_g[l], swa_sink[l], B, T, Lc, local=True)
        if need_ctx:
            o_b = jnp.concatenate([o_b, full_attention(mq, mk, mv, _pow2_tile(256, Lc), Lc, T, Lc, T // Lc)], axis=0)
            o_c = jnp.concatenate([o_c, window_attention(zm, None, swa_qn_g[l], swa_kn_g[l], swa_sink[l], B, T, Lc,
                                                         local=False)], axis=0)

        merged = merge_branches(h, seg(12, 13), o_f, o_b2, zg, 3 * nk, gdn_norm_g[l], o_b, o_c, w_branch_a, w_branch_b,
                                w_branch_c, l, rows_out, TM // 2, tn_gate)
        rows_new = matmul(merged, w_out, w_lead=l, tm=TM, tn=tn_out, out_dtype=F32, epilogue="residual",
                          resid=rows_all, modt=modt, gran=gran, gate_row=2)

        i = l // 2
        if l % 2 == 0:
            h2 = modulate(rows_new, norm_ffn_g[l], modt, rows_out, gran, shift_row=3, scale_row=4)
            nblk = rows_out // TM
            F = ffn_w_gate.shape[-1]
            y = swiglu_grouped(h2, ffn_w_gate[:, None], ffn_w_up[:, None], ffn_w_down[:, None], i,
                               jnp.zeros((nblk,), jnp.int32), jnp.full((nblk,), 2, jnp.int32),
                               jnp.ones((rows_out, 1), F32), rows_out, TM, _pow2_tile(MAX_FFN_TILE, F))
        else:
            E = moe_w_gate.shape[1]
            rw = jnp.pad(moe_router[i], ((0, 0), (0, LANES - E)))
            rb = jnp.pad(moe_router_bias[i].astype(F32), (0, LANES - E))[None, :]
            h2, logits = modulate(rows_new, norm_ffn_g[l], modt, rows_out, gran, shift_row=3, scale_row=4,
                                  router=(rw, rb))
            top_logit, top_idx = lax.top_k(logits[:, :E], TOP_K)
            top_w = jax.nn.softmax(top_logit, axis=-1)
            flat_e = top_idx.reshape(-1)
            onehot = (flat_e[:, None] == jnp.arange(E)[None, :]).astype(jnp.int32)
            rank = jnp.take_along_axis(jnp.cumsum(onehot, axis=0) - onehot, flat_e[:, None], axis=1)[:, 0]
            counts = jnp.sum(onehot, axis=0)
            padded = (counts + TM - 1) // TM * TM
            pstart = jnp.cumsum(padded) - padded
            dest = pstart[flat_e] + rank
            nblk = -(-(rows_out * TOP_K) // TM) + E
            slots = nblk * TM
            src = jnp.zeros((slots,), jnp.int32).at[dest].set(jnp.arange(rows_out * TOP_K, dtype=jnp.int32) // TOP_K)
            w_slot = jnp.zeros((slots,), F32).at[dest].set(top_w.reshape(-1))
            blk_start = jnp.arange(nblk, dtype=jnp.int32) * TM
            ends = jnp.cumsum(padded)
            block_e = jnp.minimum(jnp.searchsorted(ends, blk_start, side="right"), E - 1).astype(jnp.int32)
            used = blk_start < ends[-1]
            last_e = block_e[jnp.maximum(jnp.sum(used.astype(jnp.int32)) - 1, 0)]
            block_e = jnp.where(used, block_e, last_e)
            rows_in_block = jnp.clip((pstart + counts)[block_e] - blk_start, 0, TM)
            half = TM // 2
            block_valid = jnp.where(used, (rows_in_block + half - 1) // half, 0).astype(jnp.int32)
            F = moe_w_gate.shape[-1]
            ys = swiglu_grouped(h2[src], moe_w_gate, moe_w_up, moe_w_down, i, block_e, block_valid,
                                w_slot[:, None], slots, TM, _pow2_tile(MAX_FFN_TILE, F))
            dest2 = dest.reshape(rows_out, TOP_K)
            y = ys[dest2[:, 0]] + ys[dest2[:, 1]]
        gate_f = modt[:, 5][:rows_out // gran]
        rows_ffn = (rows_new.reshape(rows_out // gran, gran, D) + gate_f[:, None, :] * y.reshape(rows_out // gran, gran, D))
        rows_all = rows_ffn.reshape(rows_out, D)

    return rows_all[:N].reshape(B, T, D)
```

```python
import functools

import jax
import jax.numpy as jnp
import numpy as np
from jax import lax
from jax.experimental import pallas as pl
from jax.experimental.pallas import tpu as pltpu

F32 = jnp.float32
BF16 = jnp.bfloat16

GRID_W = 64
EPS = 1e-6
ROPE_BASE = 10000.0
N_BRANCH = 3
GDN_HEADS = 8
GDN_DK = 128
GDN_DV = 128
GDN_CONV = 5
GDN_CHUNK = 64
MLA_HEADS = 8
MLA_Q_RANK = 768
MLA_KV_RANK = 512
MLA_NOPE = 128
MLA_ROPE = 64
MLA_V = 128
MLA_QK = MLA_NOPE + MLA_ROPE
MLA_SCALE = MLA_QK ** -0.5
SWA_HEADS = 8
SWA_KV_HEADS = 2
SWA_HD = 128
SWA_WINDOW = 128
SWA_BLOCK = 128
SWA_SCALE = SWA_HD ** -0.5
N_EXPERTS = 8
TOP_K = 2

LANES = 128
VMEM_LIMIT_BYTES = 56 * 1024 * 1024
MAX_ROW_TILE = 1024
MAX_COL_TILE = 1024
MAX_FFN_TILE = 512
MOD_ROWS = 8
NEG_BIG = -1e30
LOG2E = 1.4426950408889634
ATTN_ROW_GROUP = 256
MLA_QK_PAD = 2 * LANES

MIX_CQ, MIX_SK, MIX_SQ, MIX_CKV, MIX_SV, MIX_KR, MIX_ZS, MIX_WIDTH = 0, 768, 1024, 2048, 2560, 2816, 2944, 3072


def _params(*sem):
    return pltpu.CompilerParams(dimension_semantics=sem, vmem_limit_bytes=VMEM_LIMIT_BYTES)


def _pow2_tile(limit, *dims):
    t = 1
    while t * 2 <= limit and all(d % (t * 2) == 0 for d in dims):
        t *= 2
    return t


def _rms_rows(x, g):
    return x * lax.rsqrt(jnp.mean(x * x, axis=-1, keepdims=True) + EPS) * g


def _modulate_kernel(x_ref, g_ref, mod_ref, *rest, shift_row, scale_row, with_router):
    mod = mod_ref[0]
    h = _rms_rows(x_ref[...], g_ref[...]) * (1.0 + mod[scale_row:scale_row + 1]) + mod[shift_row:shift_row + 1]
    if with_router:
        rw_ref, rb_ref, h_ref, lg_ref = rest
        lg_ref[...] = jnp.dot(h.astype(BF16), rw_ref[...].astype(BF16), preferred_element_type=F32) + rb_ref[...]
    else:
        (h_ref,) = rest
    h_ref[...] = h.astype(h_ref.dtype)


def modulate(x, gain, modt, rows, gran, shift_row, scale_row, router=None):
    D = x.shape[1]
    tm = gran
    kern = functools.partial(_modulate_kernel, shift_row=shift_row, scale_row=scale_row,
                             with_router=router is not None)
    in_specs = [pl.BlockSpec((tm, D), lambda i: (i, 0)),
                pl.BlockSpec((1, D), lambda i: (0, 0)),
                pl.BlockSpec((1, MOD_ROWS, D), lambda i: (i, 0, 0))]
    args = [x, gain.reshape(1, D), modt]
    out_shape = [jax.ShapeDtypeStruct((rows, D), BF16)]
    out_specs = [pl.BlockSpec((tm, D), lambda i: (i, 0))]
    if router is not None:
        rw, rb = router
        in_specs += [pl.BlockSpec((D, LANES), lambda i: (0, 0)), pl.BlockSpec((1, LANES), lambda i: (0, 0))]
        args += [rw, rb]
        out_shape.append(jax.ShapeDtypeStruct((rows, LANES), F32))
        out_specs.append(pl.BlockSpec((tm, LANES), lambda i: (i, 0)))
    out = pl.pallas_call(kern, grid=(rows // tm,), in_specs=in_specs, out_specs=out_specs,
                         out_shape=out_shape, compiler_params=_params("parallel"))(*args)
    return out if router is not None else out[0]


def _mm_kernel(a_ref, w_ref, *rest, epilogue, gate_row):
    acc = jnp.dot(a_ref[...].astype(BF16), w_ref[...].astype(BF16), preferred_element_type=F32)
    if epilogue == "bias":
        b_ref, o_ref = rest
        acc = acc + b_ref[...]
    elif epilogue == "sigmoid":
        (o_ref,) = rest
        acc = jax.nn.sigmoid(acc)
    elif epilogue == "residual":
        x_ref, mod_ref, o_ref = rest
        acc = x_ref[...] + mod_ref[0][gate_row:gate_row + 1] * acc
    else:
        (o_ref,) = rest
    o_ref[...] = acc.astype(o_ref.dtype)


def matmul(a, w, *, rows=None, w_lead=None, col0=0, ncols=None, tm, tn, out_dtype, epilogue=None,
           bias=None, resid=None, modt=None, gran=None, gate_row=0):
    rows = a.shape[0] if rows is None else rows
    K = a.shape[1]
    ncols = w.shape[-1] if ncols is None else ncols
    assert rows % tm == 0 and ncols % tn == 0 and col0 % tn == 0
    cb = col0 // tn
    if w.ndim == 3:
        w_spec = pl.BlockSpec((None, K, tn), lambda i, j: (w_lead, 0, cb + j))
    else:
        w_spec = pl.BlockSpec((K, tn), lambda i, j: (0, cb + j))
    in_specs = [pl.BlockSpec((tm, K), lambda i, j: (i, 0)), w_spec]
    args = [a, w]
    if epilogue == "bias":
        in_specs.append(pl.BlockSpec((1, tn), lambda i, j: (0, j)))
        args.append(bias)
    elif epilogue == "residual":
        step = tm // gran
        in_specs += [pl.BlockSpec((tm, tn), lambda i, j: (i, j)),
                     pl.BlockSpec((1, MOD_ROWS, tn), lambda i, j: (i * step, 0, j))]
        args += [resid, modt]
    kern = functools.partial(_mm_kernel, epilogue=epilogue, gate_row=gate_row)
    return pl.pallas_call(kern, grid=(rows // tm, ncols // tn), in_specs=in_specs,
                          out_specs=pl.BlockSpec((tm, tn), lambda i, j: (i, j)),
                          out_shape=jax.ShapeDtypeStruct((rows, ncols), out_dtype),
                          compiler_params=_params("parallel", "arbitrary"))(*args)


def _merge_kernel(h_ref, of_ref, ob2_ref, og_ref, gn_ref, ob_ref, oc_ref, wga_ref, wgb_ref, wgc_ref,
                  wa_ref, wb_ref, wc_ref, o_ref, oa_scr):
    @pl.when(pl.program_id(1) == 0)
    def _():
        for h in range(GDN_HEADS):
            cs = slice(h * GDN_DV, (h + 1) * GDN_DV)
            o = _rms_rows(of_ref[:, cs] + ob2_ref[:, cs], gn_ref[...])
            oa_scr[:, cs] = (o * jax.nn.silu(og_ref[:, cs])).astype(BF16)

    hm = h_ref[...]
    gate = lambda w_ref: jax.nn.sigmoid(jnp.dot(hm, w_ref[...], preferred_element_type=F32))
    acc = gate(wga_ref) * jnp.dot(oa_scr[...], wa_ref[...], preferred_element_type=F32)
    acc += gate(wgb_ref) * jnp.dot(ob_ref[...], wb_ref[...], preferred_element_type=F32)
    acc += gate(wgc_ref) * jnp.dot(oc_ref[...], wc_ref[...], preferred_element_type=F32)
    o_ref[...] = acc.astype(o_ref.dtype)


def merge_branches(h, w_gates, o_f, o_b2, zg, og_col, gdn_g, ob, oc, wa, wb, wc, l, rows, tm, tn):
    D = wa.shape[-1]
    nj = D // tn
    na = GDN_HEADS * GDN_DV
    row = lambda w, cb=0: pl.BlockSpec((tm, w), lambda i, j: (i, cb))
    g_spec = lambda k: pl.BlockSpec((D, tn), lambda i, j: (0, k * nj + j))
    wa, wb, wc = (w[l].astype(BF16) for w in (wa, wb, wc))
    w_gates = w_gates.astype(BF16)
    w_spec = lambda w: pl.BlockSpec((w.shape[0], tn), lambda i, j: (0, j))
    return pl.pallas_call(
        _merge_kernel, grid=(rows // tm, nj),
        in_specs=[row(D), row(na), row(na), row(na, og_col // na), pl.BlockSpec((1, GDN_DV), lambda i, j: (0, 0)),
                  row(ob.shape[1]), row(oc.shape[1]), g_spec(0), g_spec(1), g_spec(2),
                  w_spec(wa), w_spec(wb), w_spec(wc)],
        out_specs=pl.BlockSpec((tm, tn), lambda i, j: (i, j)),
        out_shape=jax.ShapeDtypeStruct((rows, D), BF16),
        scratch_shapes=[pltpu.VMEM((tm, na), BF16)],
        compiler_params=_params("parallel", "arbitrary"))(
            h, o_f, o_b2, zg, gdn_g.reshape(1, GDN_DV).astype(F32), ob, oc, w_gates, w_gates, w_gates, wa, wb, wc)


def _swiglu_kernel(be_ref, bv_ref, x_ref, wg_ref, wu_ref, wd_ref, rs_ref, o_ref):
    i, f = pl.program_id(0), pl.program_id(1)

    @pl.when(f == 0)
    def _():
        o_ref[...] = jnp.zeros_like(o_ref)

    def accumulate(n):
        x = x_ref[:n, :]
        g = jnp.dot(x, wg_ref[...].astype(BF16), preferred_element_type=F32)
        u = jnp.dot(x, wu_ref[...].astype(BF16), preferred_element_type=F32)
        h = (jax.nn.silu(g) * u).astype(BF16)
        o_ref[:n, :] += jnp.dot(h, wd_ref[...].astype(BF16), preferred_element_type=F32)

        @pl.when(f == pl.num_programs(1) - 1)
        def _():
            o_ref[:n, :] = o_ref[:n, :] * rs_ref[:n, :]

    tm = x_ref.shape[0]
    pl.when(bv_ref[i] == 2)(functools.partial(accumulate, tm))
    pl.when(bv_ref[i] == 1)(functools.partial(accumulate, tm // 2))


def swiglu_grouped(x, wg, wu, wd, l, block_e, block_valid, row_scale, rows, tm, tf):
    D = x.shape[1]
    F = wg.shape[-1]
    nf = F // tf
    assert rows % tm == 0 and F % tf == 0

    def f_idx(i, f, bv):
        return jnp.where(bv[i] > 0, f, nf - 1)

    grid_spec = pltpu.PrefetchScalarGridSpec(
        num_scalar_prefetch=2, grid=(rows // tm, nf),
        in_specs=[pl.BlockSpec((tm, D), lambda i, f, be, bv: (i, 0)),
                  pl.BlockSpec((None, None, D, tf), lambda i, f, be, bv: (l, be[i], 0, f_idx(i, f, bv))),
                  pl.BlockSpec((None, None, D, tf), lambda i, f, be, bv: (l, be[i], 0, f_idx(i, f, bv))),
                  pl.BlockSpec((None, None, tf, D), lambda i, f, be, bv: (l, be[i], f_idx(i, f, bv), 0)),
                  pl.BlockSpec((tm, 1), lambda i, f, be, bv: (i, 0))],
        out_specs=pl.BlockSpec((tm, D), lambda i, f, be, bv: (i, 0)))
    return pl.pallas_call(_swiglu_kernel, grid_spec=grid_spec,
                          out_shape=jax.ShapeDtypeStruct((rows, D), F32),
                          compiler_params=_params("parallel", "arbitrary"))(
                              block_e, block_valid, x, wg, wu, wd, row_scale)


def _rope_tables(T, extra, rot_dim):
    n = rot_dim // 4
    t = jnp.arange(T)
    inv = jnp.power(ROPE_BASE, -jnp.arange(n, dtype=F32) / n)
    ar = (t // GRID_W).astype(F32)[:, None] * inv
    ac = (t % GRID_W).astype(F32)[:, None] * inv
    z = jnp.zeros((T, n), F32)
    pad = lambda a, fill: jnp.concatenate([a, jnp.full((T, LANES - 4 * n), fill, F32)], axis=1)
    c = pad(jnp.concatenate([jnp.cos(ar), jnp.cos(ar), jnp.cos(ac), jnp.cos(ac)], axis=1), 1.0)
    a = pad(jnp.concatenate([-jnp.sin(ar), z, -jnp.sin(ac), z], axis=1), 0.0)
    b = pad(jnp.concatenate([z, jnp.sin(ar), z, jnp.sin(ac)], axis=1), 0.0)
    tab = jnp.concatenate([c, a, b], axis=1)
    ident = jnp.concatenate([jnp.ones((extra, LANES), F32), jnp.zeros((extra, 2 * LANES), F32)], axis=1)
    return jnp.concatenate([tab, ident], axis=0)


def _apply_rope(x, tab, half):
    return (x * tab[:, :LANES] + pltpu.roll(x, LANES - half, 1) * tab[:, LANES:2 * LANES]
            + pltpu.roll(x, half, 1) * tab[:, 2 * LANES:])


def _mla_q_kernel(c_ref, g_ref, w_ref, hg_ref, tab_ref, q_ref):
    a = _rms_rows(c_ref[...], g_ref[...]).astype(BF16)
    tab, hg = tab_ref[...], hg_ref[...]
    for h in range(MLA_HEADS):
        acc = jnp.dot(a, w_ref[:, h * MLA_QK_PAD:(h + 1) * MLA_QK_PAD], preferred_element_type=F32)
        y = acc * lax.rsqrt(jnp.sum(acc * acc, axis=-1, keepdims=True) * (1.0 / MLA_QK) + EPS) * hg
        q_ref[h, :, :LANES] = y[:, :LANES].astype(q_ref.dtype)
        q_ref[h, :, LANES:] = _apply_rope(y[:, LANES:], tab, MLA_ROPE // 4).astype(q_ref.dtype)


def _mla_kv_kernel(c_ref, kr_ref, g_ref, w_ref, hg_ref, tab_ref, k_ref, v_ref):
    a = _rms_rows(c_ref[...], g_ref[...]).astype(BF16)
    tab, hg = tab_ref[...], hg_ref[...]
    kr = kr_ref[...]
    kr_ss = jnp.sum(kr * kr, axis=-1, keepdims=True)
    width = MLA_NOPE + MLA_V
    for h in range(MLA_HEADS):
        acc = jnp.dot(a, w_ref[:, h * width:(h + 1) * width], preferred_element_type=F32)
        kn = acc[:, :LANES]
        r = lax.rsqrt((jnp.sum(kn * kn, axis=-1, keepdims=True) + kr_ss) * (1.0 / MLA_QK) + EPS)
        k_ref[h, :, :LANES] = (kn * r * hg[:, :LANES]).astype(k_ref.dtype)
        k_ref[h, :, LANES:] = _apply_rope(kr * r * hg[:, LANES:], tab, MLA_ROPE // 4).astype(k_ref.dtype)
        v_ref[h] = acc[:, LANES:].astype(v_ref.dtype)


def _seq_pos(i, tm, T, Lc, N):
    nl, nc, nlt = T // tm, Lc // tm, N // tm
    k = i - nlt
    return jnp.where(i < nlt, i // nl, k // nc), jnp.where(i < nlt, i % nl, nl + k % nc)


def mla_project_q(zm, rows, q_norm_g, w_uq_l, qn_g, tab, B, T, Lc, tm):
    N = B * T
    Lq = T + (Lc if rows > N else 0)
    H, R = MLA_HEADS, MLA_Q_RANK
    wp = jnp.pad(w_uq_l.reshape(R, H, MLA_QK), ((0, 0), (0, 0), (0, MLA_QK_PAD - MLA_QK)))
    wp = wp.reshape(R, H * MLA_QK_PAD).astype(BF16)
    hg = jnp.pad(qn_g.astype(F32) * (MLA_SCALE * LOG2E), (0, MLA_QK_PAD - MLA_QK))[None, :]
    pos = lambda i: _seq_pos(i, tm, T, Lc, N)
    return pl.pallas_call(
        _mla_q_kernel, grid=(rows // tm,),
        in_specs=[pl.BlockSpec((tm, R), lambda i: (i, MIX_CQ // R)),
                  pl.BlockSpec((1, R), lambda i: (0, 0)),
                  pl.BlockSpec((R, H * MLA_QK_PAD), lambda i: (0, 0)),
                  pl.BlockSpec((1, MLA_QK_PAD), lambda i: (0, 0)),
                  pl.BlockSpec((tm, 3 * LANES), lambda i: (pos(i)[1], 0))],
        out_specs=pl.BlockSpec((None, H, tm, MLA_QK_PAD), lambda i: (pos(i)[0], 0, pos(i)[1], 0)),
        out_shape=jax.ShapeDtypeStruct((B, H, Lq, MLA_QK_PAD), BF16),
        compiler_params=_params("parallel"))(zm, q_norm_g.reshape(1, R).astype(F32), wp, hg, tab)


def mla_project_kv(zm, kv_norm_g, w_ukv_l, kn_g, tab, B, T, Lc, tm):
    M = zm.shape[0]
    N = B * T
    H, R = MLA_HEADS, MLA_KV_RANK
    hg = jnp.pad(kn_g.astype(F32), (0, MLA_QK_PAD - MLA_QK))[None, :]
    pos = lambda i: _seq_pos(i, tm, T, Lc, N)
    o_spec = lambda w: pl.BlockSpec((None, H, tm, w), lambda i: (pos(i)[0], 0, pos(i)[1], 0))
    return pl.pallas_call(
        _mla_kv_kernel, grid=(M // tm,),
        in_specs=[pl.BlockSpec((tm, R), lambda i: (i, MIX_CKV // R)),
                  pl.BlockSpec((tm, LANES), lambda i: (i, MIX_KR // LANES)),
                  pl.BlockSpec((1, R), lambda i: (0, 0)),
                  pl.BlockSpec((R, H * (MLA_NOPE + MLA_V)), lambda i: (0, 0)),
                  pl.BlockSpec((1, MLA_QK_PAD), lambda i: (0, 0)),
                  pl.BlockSpec((tm, 3 * LANES), lambda i: (pos(i)[1], 0))],
        out_specs=[o_spec(MLA_QK_PAD), o_spec(MLA_V)],
        out_shape=[jax.ShapeDtypeStruct((B, H, T + Lc, MLA_QK_PAD), BF16),
                   jax.ShapeDtypeStruct((B, H, T + Lc, MLA_V), BF16)],
        compiler_params=_params("parallel"))(zm, zm, kv_norm_g.reshape(1, R).astype(F32), w_ukv_l.astype(BF16), hg, tab)


def _attn_kernel(q_ref, k_ref, v_ref, o_ref, *, sub):
    for r0 in range(0, q_ref.shape[0], sub):
        rows = slice(r0, r0 + sub)
        s = lax.dot_general(q_ref[rows, :], k_ref[...], (((1,), (1,)), ((), ())), preferred_element_type=F32)
        p = jnp.exp2(s - jnp.max(s, axis=-1, keepdims=True))
        den = jnp.sum(p, axis=-1, keepdims=True)
        o = jnp.dot(p.astype(BF16), v_ref[...], preferred_element_type=F32)
        o_ref[rows, :] = (o * (1.0 / den)).astype(o_ref.dtype)


def full_attention(q, k, v, tq, n_q, q_off, kl, k_blk):
    B, H, _, d = q.shape
    e = v.shape[3]
    nt = n_q // tq
    qo = q_off // tq
    return pl.pallas_call(
        functools.partial(_attn_kernel, sub=min(tq, ATTN_ROW_GROUP)), grid=(B, H, nt),
        in_specs=[pl.BlockSpec((None, None, tq, d), lambda b, h, i: (b, h, qo + i, 0)),
                  pl.BlockSpec((None, None, kl, d), lambda b, h, i: (b, h, k_blk, 0)),
                  pl.BlockSpec((None, None, kl, e), lambda b, h, i: (b, h, k_blk, 0))],
        out_specs=pl.BlockSpec((tq, e), lambda b, h, i: (b * nt + i, h)),
        out_shape=jax.ShapeDtypeStruct((B * n_q, H * e), BF16),
        compiler_params=_params("parallel", "parallel", "arbitrary"))(q, k, v)


def _swa_kernel(*refs, local, n_blocks):
    if local:
        (q_ref, kp_ref, kc_ref, kn_ref, vp_ref, vc_ref, vn_ref, kx_ref, vx_ref, tp_ref, tc_ref, tn_ref,
         qg_ref, kg_ref, sink_ref, o_ref) = refs
    else:
        q_ref, kx_ref, vx_ref, qg_ref, kg_ref, sink_ref, o_ref = refs
    n = pl.program_id(1)
    Bk, d = SWA_BLOCK, SWA_HD
    Lc = kx_ref.shape[0]
    R = SWA_HEADS // SWA_KV_HEADS
    half = SWA_HD // 4

    def prep(x, g, tab):
        y = _rms_rows(x, g)
        return y if tab is None else _apply_rope(y, tab, half)

    if local:
        iq = lax.broadcasted_iota(jnp.int32, (Bk, 3 * Bk), 0)
        jk = lax.broadcasted_iota(jnp.int32, (Bk, 3 * Bk), 1)
        valid = jnp.abs(iq + Bk - jk) <= SWA_WINDOW
        valid = valid & ((jk >= Bk) | (n > 0)) & ((jk < 2 * Bk) | (n < n_blocks - 1))
        bias = jnp.where(valid, 0.0, NEG_BIG).astype(F32)
        bias = jnp.concatenate([bias, jnp.zeros((Bk, Lc), F32)], axis=1)
        tp, tc, tn = tp_ref[...], tc_ref[...], tn_ref[...]
    else:
        tc = None
    kg, qg = kg_ref[...], qg_ref[...]
    kcat, vcat = [], []
    for g in range(SWA_KV_HEADS):
        cs = slice(g * d, (g + 1) * d)
        kx = prep(kx_ref[:, cs], kg, None).astype(BF16)
        if local:
            kcat.append(jnp.concatenate([prep(kp_ref[:, cs], kg, tp).astype(BF16), prep(kc_ref[:, cs], kg, tc).astype(BF16),
                                         prep(kn_ref[:, cs], kg, tn).astype(BF16), kx], axis=0))
            vcat.append(jnp.concatenate([vp_ref[:, cs].astype(BF16), vc_ref[:, cs].astype(BF16),
                                         vn_ref[:, cs].astype(BF16), vx_ref[:, cs].astype(BF16)], axis=0))
        else:
            kcat.append(kx)
            vcat.append(vx_ref[:, cs].astype(BF16))
    hs = range(SWA_HEADS)
    qh = [prep(q_ref[:, h * d:(h + 1) * d], qg, tc).astype(BF16) for h in hs]
    s = [lax.dot_general(qh[h], kcat[h // R], (((1,), (1,)), ((), ())), preferred_element_type=F32) for h in hs]
    if local:
        s = [sh + bias for sh in s]
    sink = [sink_ref[h:h + 1, 0:1] for h in hs]
    m = [jnp.maximum(jnp.max(s[h], axis=-1, keepdims=True), sink[h]) for h in hs]
    p = [jnp.exp2(s[h] - m[h]).astype(BF16) for h in hs]
    vone = [jnp.concatenate([v, jnp.ones_like(v)], axis=1) for v in vcat]
    for h in hs:
        o = jnp.dot(p[h], vone[h // R], preferred_element_type=F32)
        den = o[:, d:] + jnp.exp2(sink[h] - m[h])
        o_ref[:, h * d:(h + 1) * d] = (o[:, :d] * (1.0 / den)).astype(o_ref.dtype)


def window_attention(zm, tab, qn_g, kn_g, sink, B, T, Lc, local):
    N = B * T
    Bk = SWA_BLOCK
    Q, KV = SWA_HEADS * SWA_HD, SWA_KV_HEADS * SWA_HD
    n_q = T if local else Lc
    nb = n_q // Bk
    row0 = 0 if local else N // Bk
    sink_b = jnp.broadcast_to(sink.astype(F32)[:, None], (SWA_HEADS, LANES))
    q_spec = pl.BlockSpec((Bk, Q), lambda b, n: (row0 + b * nb + n, MIX_SQ // Q))
    kx_spec = pl.BlockSpec((Lc, KV), lambda b, n: (N // Lc + b, MIX_SK // KV))
    vx_spec = pl.BlockSpec((Lc, KV), lambda b, n: (N // Lc + b, MIX_SV // KV))
    g_spec = pl.BlockSpec((1, SWA_HD), lambda b, n: (0, 0))
    s_spec = pl.BlockSpec((SWA_HEADS, LANES), lambda b, n: (0, 0))
    gains = (qn_g.reshape(1, SWA_HD).astype(F32) * (SWA_SCALE * LOG2E), kn_g.reshape(1, SWA_HD).astype(F32),
             sink_b * LOG2E)
    if local:
        pv = lambda n: jnp.maximum(n - 1, 0)
        nx = lambda n: jnp.minimum(n + 1, nb - 1)
        kv_spec = lambda f, col: pl.BlockSpec((Bk, KV), lambda b, n: (b * nb + f(n), col // KV))
        t_spec = lambda f: pl.BlockSpec((Bk, 3 * LANES), lambda b, n: (f(n), 0))
        same = lambda n: n
        in_specs = [q_spec, kv_spec(pv, MIX_SK), kv_spec(same, MIX_SK), kv_spec(nx, MIX_SK),
                    kv_spec(pv, MIX_SV), kv_spec(same, MIX_SV), kv_spec(nx, MIX_SV), kx_spec, vx_spec,
                    t_spec(pv), t_spec(same), t_spec(nx), g_spec, g_spec, s_spec]
        args = (zm,) * 9 + (tab,) * 3 + gains
    else:
        in_specs = [q_spec, kx_spec, vx_spec, g_spec, g_spec, s_spec]
        args = (zm,) * 3 + gains
    kern = functools.partial(_swa_kernel, local=local, n_blocks=nb)
    return pl.pallas_call(kern, grid=(B, nb), in_specs=in_specs,
                          out_specs=pl.BlockSpec((Bk, Q), lambda b, n: (b * nb + n, 0)),
                          out_shape=jax.ShapeDtypeStruct((B * n_q, Q), BF16),
                          compiler_params=_params("parallel", "arbitrary"))(*args)


def _split3(x):
    hi = x.astype(BF16)
    r1 = x - hi.astype(F32)
    mid = r1.astype(BF16)
    lo = (r1 - mid.astype(F32)).astype(BF16)
    return hi, mid, lo


def _dot_bf16(a, b):
    return jnp.dot(a.astype(BF16), b.astype(BF16), preferred_element_type=F32)


def _dot_nt(a, b):
    return lax.dot_general(a.astype(BF16), b.astype(BF16), (((1,), (1,)), ((), ())), preferred_element_type=F32)


def _gdn_prep_kernel(zp_ref, zc_ref, zn_ref, zs_ref, cw_ref, ad_ref, wq_ref, u_ref, qk_ref, kd_ref, tot_ref,
                     ext_scr, qkv_scr, *, n_ctx, n_chunks):
    C, H, dk, dv = GDN_CHUNK, GDN_HEADS, GDN_DK, GDN_DV
    P = 2 * C
    nk = H * dk
    halo = 8
    pad = (GDN_CONV - 1) // 2

    ch = pl.program_id(1)
    has_prev = jnp.where((ch == 0) | (ch == n_ctx), 0.0, 1.0)
    has_next = jnp.where((ch == n_ctx - 1) | (ch == n_chunks - 1), 0.0, 1.0)
    ext_scr[0:halo, :] = zp_ref[C - halo:C, :] * has_prev
    ext_scr[halo:halo + C, :] = zc_ref[...]
    ext_scr[halo + C:, :] = zn_ref[0:halo, :] * has_next
    for part in range(3):
        cs = slice(part * nk, (part + 1) * nk)
        y = sum(ext_scr[halo - pad + j:halo - pad + j + C, cs] * cw_ref[j:j + 1, cs] for j in range(GDN_CONV))
        qkv_scr[:, cs] = y * jax.nn.sigmoid(y)
    for h in range(H):
        for part, scale in ((0, dk ** -0.5), (1, 1.0)):
            cs = slice(part * nk + h * dk, part * nk + (h + 1) * dk)
            y = qkv_scr[:, cs]
            qkv_scr[:, cs] = y * (lax.rsqrt(jnp.sum(y * y, axis=-1, keepdims=True) + EPS) * scale)
    q_ref, k_ref, v_ref = qkv_scr.at[:, 0:nk], qkv_scr.at[:, nk:2 * nk], qkv_scr.at[:, 2 * nk:3 * nk]

    zs = zs_ref[...]
    beta = jax.nn.sigmoid(zs)
    a = pltpu.roll(zs, LANES - 2 * H, 1) + ad_ref[1:2, :]
    g = -ad_ref[0:1, :] * (jnp.maximum(a, 0.0) + jnp.log(1.0 + jnp.exp(-jnp.abs(a))))

    ii = lax.broadcasted_iota(jnp.int32, (C, C), 0)
    jj = lax.broadcasted_iota(jnp.int32, (C, C), 1)
    low = (ii >= jj).astype(BF16)
    upp = (ii <= jj).astype(BF16)
    r = lax.broadcasted_iota(jnp.int32, (P, P), 0)
    c = lax.broadcasted_iota(jnp.int32, (P, P), 1)
    rq = jnp.where(r < C, 0, 1)
    cq = jnp.where(c < C, 0, 1)
    ahead = (r - c) * (1 - 2 * rq)
    causal = (rq == cq) & (ahead >= 0)
    strict = (rq == cq) & (ahead > 0)
    eye = (r == c).astype(F32)
    row_fwd = lax.broadcasted_iota(jnp.int32, (P, 1), 0) < C

    g3 = _split3(g)
    tri = jnp.concatenate([low, upp], axis=0)
    dcol_all = sum(jnp.dot(tri, p, preferred_element_type=F32) for p in g3)
    tdot = lambda p, t: lax.dot_general(p, t, (((0,), (0,)), ((), ())), preferred_element_type=F32)
    drow_all = jnp.concatenate([sum(tdot(p, upp) for p in g3)[:H], sum(tdot(p, low) for p in g3)[H:2 * H]],
                               axis=1)
    beta2 = jnp.concatenate([beta, beta], axis=0)

    def pair_col(a, h):
        return jnp.where(row_fwd, a[:, h:h + 1], a[:, H + h:H + h + 1])

    stack = lambda ref, h, w: jnp.concatenate([ref[:, h * w:(h + 1) * w]] * 2, axis=0)

    group = 8
    for h0 in range(0, H, group):
        hs = range(h0, h0 + group)
        dc = [pair_col(dcol_all, h) for h in hs]
        seg = [jnp.exp(jnp.where(causal, dc[a] - drow_all[h:h + 1, :], NEG_BIG)) for a, h in enumerate(hs)]
        b2 = [pair_col(beta2, h) for h in hs]
        kk = [_dot_nt(stack(k_ref, h, dk) * b2[a], stack(k_ref, h, dk)) for a, h in enumerate(hs)]
        pw = [jnp.where(strict, kk[a] * seg[a], 0.0) for a in range(group)]
        inv = [eye - m for m in pw]
        k = 2
        while k < C:
            pw = [_dot_bf16(m, m) for m in pw]
            inv = [t + _dot_bf16(t, m) for t, m in zip(inv, pw)]
            k *= 2
        for a, h in enumerate(hs):
            k2, q2, v2 = stack(k_ref, h, dk), stack(q_ref, h, dk), stack(v_ref, h, dv)
            ecol = jnp.exp(dc[a])
            rhs = jnp.concatenate([k2 * (b2[a] * ecol), v2 * b2[a]], axis=1)
            sol = rhs + _dot_bf16(inv[a] - eye, rhs)
            qk = _dot_nt(q2, k2) * seg[a]
            dlast = jnp.where(row_fwd, dc[a][C - 1:C], dc[a][C:C + 1])
            qd = q2 * ecol
            kd = k2 * jnp.exp(dlast - dc[a])
            tot = jnp.exp(dlast)
            for d, rs in enumerate((slice(0, C), slice(C, P))):
                hd = d * H + h
                wq_ref[hd, :C, :] = sol[rs, :dk].astype(wq_ref.dtype)
                wq_ref[hd, C:, :] = qd[rs].astype(wq_ref.dtype)
                u_ref[hd] = sol[rs, dk:]
                qk_ref[hd] = qk[rs, d * C:(d + 1) * C].astype(qk_ref.dtype)
                kd_ref[hd] = kd[rs].astype(kd_ref.dtype)
                tot_ref[hd] = jnp.broadcast_to(tot[d * C:d * C + 1], (1, LANES))


def gdn_prepare_chunks(zg, zs, conv_w, a_log, dt_bias, B, T, Lc):
    C, H = GDN_CHUNK, GDN_HEADS
    N = B * T
    n_ctx, n = Lc // C, (T + Lc) // C
    width = 3 * H * GDN_DK
    last = zg.shape[0] // C - 1

    def row_block(b, c):
        return jnp.where(c < n_ctx, (N + b * Lc) // C + c, (b * T) // C + c - n_ctx)

    z_spec = lambda d: pl.BlockSpec((C, width), lambda b, c: (jnp.clip(row_block(b, c) + d, 0, last), 0))
    lanes = lambda a: jnp.pad(a.astype(F32), (0, LANES - 2 * H))
    ad = jnp.zeros((8, LANES), F32).at[0].set(lanes(jnp.exp(a_log.astype(F32)))).at[1].set(lanes(dt_bias))
    per = lambda r, w: pl.BlockSpec((None, None, 2 * H, r, w), lambda b, c: (b, c, 0, 0, 0))
    shp = lambda r, w, dt: jax.ShapeDtypeStruct((B, n, 2 * H, r, w), dt)
    return pl.pallas_call(
        functools.partial(_gdn_prep_kernel, n_ctx=n_ctx, n_chunks=n), grid=(B, n),
        in_specs=[z_spec(-1), z_spec(0), z_spec(1),
                  pl.BlockSpec((C, LANES), lambda b, c: (row_block(b, c), MIX_ZS // LANES)),
                  pl.BlockSpec((GDN_CONV, width), lambda b, c: (0, 0)),
                  pl.BlockSpec((8, LANES), lambda b, c: (0, 0))],
        out_specs=[per(2 * C, GDN_DK), per(C, GDN_DV), per(C, C), per(C, GDN_DK), per(1, LANES)],
        out_shape=[shp(2 * C, GDN_DK, BF16), shp(C, GDN_DV, F32), shp(C, C, BF16), shp(C, GDN_DK, BF16),
                   shp(1, LANES, F32)],
        scratch_shapes=[pltpu.VMEM((C + 16, width), F32), pltpu.VMEM((C, width), F32)],
        compiler_params=_params("parallel", "parallel"))(
            zg, zg, zg, zs, conv_w.reshape(GDN_CONV, width).astype(F32), ad)


def _gdn_scan_kernel(wqf, uf, qkf, kdf, totf, wqb, ub, qkb, kdb, totb, of_ref, ob_ref, s_ref):
    C, H, dv = GDN_CHUNK, GDN_HEADS, GDN_DV

    @pl.when(pl.program_id(1) == 0)
    def _():
        s_ref[...] = jnp.zeros_like(s_ref)

    for d, (wq, u, qk, kd, tot, o_ref) in enumerate(((wqf, uf, qkf, kdf, totf, of_ref),
                                                     (wqb, ub, qkb, kdb, totb, ob_ref))):
        hs = range(H)
        ws = [jnp.dot(wq[h], s_ref[d * H + h].astype(BF16), preferred_element_type=F32) for h in hs]
        v_new = [(u[h] - ws[h][:C]).astype(BF16) for h in hs]
        for h in hs:
            o_ref[:, h * dv:(h + 1) * dv] = ws[h][C:] + jnp.dot(qk[h], v_new[h], preferred_element_type=F32)
        upd = [lax.dot_general(kd[h], v_new[h], (((0,), (0,)), ((), ())), preferred_element_type=F32) for h in hs]
        for h in hs:
            s_ref[d * H + h] = s_ref[d * H + h] * tot[h] + upd[h]


def gdn_scan(wq, u, qk, kd, tot, n_ctx, T, Lc):
    B, n = wq.shape[0], wq.shape[1]
    C, H = GDN_CHUNK, GDN_HEADS
    N = B * T

    def bwd(s):
        return jnp.where(s < n_ctx, n_ctx - 1 - s, n - 1 - (s - n_ctx))

    def row_block(b, c):
        return jnp.where(c < n_ctx, (N + b * Lc) // C + c, (b * T) // C + c - n_ctx)

    fw = lambda r, w: pl.BlockSpec((None, None, H, r, w), lambda b, s: (b, s, 0, 0, 0))
    bw = lambda r, w: pl.BlockSpec((None, None, H, r, w), lambda b, s: (b, bwd(s), 1, 0, 0))
    shapes = ((2 * C, GDN_DK), (C, GDN_DV), (C, C), (C, GDN_DK), (1, LANES))
    o_shape = jax.ShapeDtypeStruct((B * n * C, H * GDN_DV), F32)
    return pl.pallas_call(
        _gdn_scan_kernel, grid=(B, n),
        in_specs=[fw(*s) for s in shapes] + [bw(*s) for s in shapes],
        out_specs=[pl.BlockSpec((C, H * GDN_DV), lambda b, s: (row_block(b, s), 0)),
                   pl.BlockSpec((C, H * GDN_DV), lambda b, s: (row_block(b, bwd(s)), 0))],
        out_shape=[o_shape, o_shape],
        scratch_shapes=[pltpu.VMEM((2 * H, GDN_DK, GDN_DV), F32)],
        compiler_params=_params("parallel", "arbitrary"))(wq, u, qk, kd, tot, wq, u, qk, kd, tot)


def kernel(x, c, ctx, c_ctx, w_mod, b_mod, norm_mix_g, norm_ffn_g, w_in, gdn_conv_w, gdn_a_log, gdn_dt_bias,
           gdn_norm_g, mla_q_norm_g, mla_kv_norm_g, mla_w_uq, mla_w_ukv, mla_qn_g, mla_kn_g, swa_qn_g, swa_kn_g,
           swa_sink, w_branch_a, w_branch_b, w_branch_c, w_out, ffn_w_gate, ffn_w_up, ffn_w_down, moe_router,
           moe_router_bias, moe_w_gate, moe_w_up, moe_w_down):
    B, T, D = x.shape
    Lc = ctx.shape[1]
    depth = w_mod.shape[0]
    N, Nc = B * T, B * Lc
    M = N + Nc
    TM = _pow2_tile(MAX_ROW_TILE, T, Nc)
    gran = min(TM, 256)
    tseq = _pow2_tile(256, T, Lc)
    assert T % Lc == 0 and Lc % SWA_BLOCK == 0 and Lc % GDN_CHUNK == 0
    H = GDN_HEADS
    nk = H * GDN_DK

    sizes = (nk, nk, H * GDN_DV, H * GDN_DV, 2 * H, 2 * H, MLA_Q_RANK, MLA_KV_RANK, MLA_ROPE,
             SWA_HEADS * SWA_HD, SWA_KV_HEADS * SWA_HD, SWA_KV_HEADS * SWA_HD, N_BRANCH * D)
    off = np.concatenate([[0], np.cumsum(sizes)])
    n_gdn = int(off[4])
    tn_gdn = _pow2_tile(MAX_COL_TILE, n_gdn)
    tn_mix = _pow2_tile(MAX_COL_TILE, MIX_WIDTH)
    tn_out = _pow2_tile(MAX_COL_TILE, D)
    tn_gate = _pow2_tile(512, D)

    rows_all = jnp.concatenate([x.reshape(N, D), ctx.reshape(Nc, D)], axis=0)
    tile_batch = np.minimum(np.arange(M // gran) * gran // T, B)
    tile_batch = np.where(np.arange(M // gran) * gran < N, tile_batch, B)
    cvec = jnp.concatenate([c, c_ctx[None, :], jnp.zeros((MOD_ROWS - B - 1, D), F32)], axis=0)
    cvec = jax.nn.silu(cvec)
    tab_mla = _rope_tables(T, Lc, MLA_ROPE)
    tab_swa = _rope_tables(T, 0, SWA_HD)
    n_ctx_chunks = Lc // GDN_CHUNK
    S = Lc + T

    for l in range(depth):
        need_ctx = l < depth - 1
        rows_out = M if need_ctx else N

        mod = matmul(cvec, w_mod, w_lead=l, tm=MOD_ROWS, tn=_pow2_tile(1024, 6 * D), out_dtype=F32,
                     epilogue="bias", bias=b_mod[l][None, :])
        mod = mod.reshape(MOD_ROWS, 6, D)[:B + 1]
        mod = jnp.pad(mod, ((0, 0), (0, MOD_ROWS - 6), (0, 0)))
        modt = mod[tile_batch]

        h = modulate(rows_all, norm_mix_g[l], modt, M, gran, shift_row=0, scale_row=1)
        w_l = w_in[l]
        seg = lambda a, b: w_l[:, off[a]:off[b]]
        zcols = lambda n: jnp.zeros((D, n), w_l.dtype)
        w_mix = jnp.concatenate([seg(6, 7), seg(10, 11), seg(9, 10), seg(7, 8), seg(11, 12), seg(8, 9),
                                 zcols(MIX_ZS - MIX_KR - MLA_ROPE), seg(4, 6), zcols(MIX_WIDTH - MIX_ZS - 4 * H)], axis=1)
        zg = matmul(h, w_in, w_lead=l, ncols=n_gdn, tm=TM, tn=tn_gdn, out_dtype=F32)
        zm = matmul(h, w_mix, tm=TM, tn=tn_mix, out_dtype=F32)

        wq_c, u_c, qk_c, kd_c, tot_c = gdn_prepare_chunks(zg, zm, gdn_conv_w[l], gdn_a_log[l], gdn_dt_bias[l], B, T, Lc)
        o_f, o_b2 = gdn_scan(wq_c, u_c, qk_c, kd_c, tot_c, n_ctx_chunks, T, Lc)

        mq = mla_project_q(zm, rows_out, mla_q_norm_g[l], mla_w_uq[l], mla_qn_g[l], tab_mla, B, T, Lc, tseq)
        mk, mv = mla_project_kv(zm, mla_kv_norm_g[l], mla_w_ukv[l], mla_kn_g[l], tab_mla, B, T, Lc, tseq)
        o_b = full_attention(mq, mk, mv, _pow2_tile(MAX_ROW_TILE, T), T, 0, S, 0)
        o_c = window_attention(zm, tab_swa, swa_qn_g[l], swa_kn_g[l], swa_sink[l], B, T, Lc, local=True)
        if need_ctx:
            o_b = jnp.concatenate([o_b, full_attention(mq, mk, mv, _pow2_tile(256, Lc), Lc, T, Lc, T // Lc)], axis=0)
            o_c = jnp.concatenate([o_c, window_attention(zm, None, swa_qn_g[l], swa_kn_g[l], swa_sink[l], B, T, Lc,
                                                         local=False)], axis=0)

        merged = merge_branches(h, seg(12, 13), o_f, o_b2, zg, 3 * nk, gdn_norm_g[l], o_b, o_c, w_branch_a, w_branch_b,
                                w_branch_c, l, rows_out, TM // 2, tn_gate)
        rows_new = matmul(merged, w_out, w_lead=l, tm=TM, tn=tn_out, out_dtype=F32, epilogue="residual",
                          resid=rows_all, modt=modt, gran=gran, gate_row=2)

        i = l // 2
        if l % 2 == 0:
            h2 = modulate(rows_new, norm_ffn_g[l], modt, rows_out, gran, shift_row=3, scale_row=4)
            nblk = rows_out // TM
            F = ffn_w_gate.shape[-1]
            y = swiglu_grouped(h2, ffn_w_gate[:, None], ffn_w_up[:, None], ffn_w_down[:, None], i,
                               jnp.zeros((nblk,), jnp.int32), jnp.full((nblk,), 2, jnp.int32),
                               jnp.ones((rows_out, 1), F32), rows_out, TM, _pow2_tile(MAX_FFN_TILE, F))
        else:
            E = moe_w_gate.shape[1]
            rw = jnp.pad(moe_router[i], ((0, 0), (0, LANES - E)))
            rb = jnp.pad(moe_router_bias[i].astype(F32), (0, LANES - E))[None, :]
            h2, logits = modulate(rows_new, norm_ffn_g[l], modt, rows_out, gran, shift_row=3, scale_row=4,
                                  router=(rw, rb))
            top_logit, top_idx = lax.top_k(logits[:, :E], TOP_K)
            top_w = jax.nn.softmax(top_logit, axis=-1)
            flat_e = top_idx.reshape(-1)
            onehot = (flat_e[:, None] == jnp.arange(E)[None, :]).astype(jnp.int32)
            rank = jnp.take_along_axis(jnp.cumsum(onehot, axis=0) - onehot, flat_e[:, None], axis=1)[:, 0]
            counts = jnp.sum(onehot, axis=0)
            padded = (counts + TM - 1) // TM * TM
            pstart = jnp.cumsum(padded) - padded
            dest = pstart[flat_e] + rank
            nblk = -(-(rows_out * TOP_K) // TM) + E
            slots = nblk * TM
            src = jnp.zeros((slots,), jnp.int32).at[dest].set(jnp.arange(rows_out * TOP_K, dtype=jnp.int32) // TOP_K)
            w_slot = jnp.zeros((slots,), F32).at[dest].set(top_w.reshape(-1))
            blk_start = jnp.arange(nblk, dtype=jnp.int32) * TM
            ends = jnp.cumsum(padded)
            block_e = jnp.minimum(jnp.searchsorted(ends, blk_start, side="right"), E - 1).astype(jnp.int32)
            used = blk_start < ends[-1]
            last_e = block_e[jnp.maximum(jnp.sum(used.astype(jnp.int32)) - 1, 0)]
            block_e = jnp.where(used, block_e, last_e)
            rows_in_block = jnp.clip((pstart + counts)[block_e] - blk_start, 0, TM)
            half = TM // 2
            block_valid = jnp.where(used, (rows_in_block + half - 1) // half, 0).astype(jnp.int32)
            F = moe_w_gate.shape[-1]
            ys = swiglu_grouped(h2[src], moe_w_gate, moe_w_up, moe_w_down, i, block_e, block_valid,
                                w_slot[:, None], slots, TM, _pow2_tile(MAX_FFN_TILE, F))
            dest2 = dest.reshape(rows_out, TOP_K)
            y = ys[dest2[:, 0]] + ys[dest2[:, 1]]
        gate_f = modt[:, 5][:rows_out // gran]
        rows_ffn = (rows_new.reshape(rows_out // gran, gran, D) + gate_f[:, None, :] * y.reshape(rows_out // gran, gran, D))
        rows_all = rows_ffn.reshape(rows_out, D)

    return rows_all[:N].reshape(B, T, D)
```

```python
import functools

import jax
import jax.numpy as jnp
import numpy as np
from jax import lax
from jax.experimental import pallas as pl
from jax.experimental.pallas import tpu as pltpu

F32 = jnp.float32
BF16 = jnp.bfloat16

GRID_W = 64
EPS = 1e-6
ROPE_BASE = 10000.0
N_BRANCH = 3
GDN_HEADS = 8
GDN_DK = 128
GDN_DV = 128
GDN_CONV = 5
GDN_CHUNK = 64
MLA_HEADS = 8
MLA_Q_RANK = 768
MLA_KV_RANK = 512
MLA_NOPE = 128
MLA_ROPE = 64
MLA_V = 128
MLA_QK = MLA_NOPE + MLA_ROPE
MLA_SCALE = MLA_QK ** -0.5
SWA_HEADS = 8
SWA_KV_HEADS = 2
SWA_HD = 128
SWA_WINDOW = 128
SWA_BLOCK = 128
SWA_SCALE = SWA_HD ** -0.5
N_EXPERTS = 8
TOP_K = 2

LANES = 128
VMEM_LIMIT_BYTES = 56 * 1024 * 1024
MAX_ROW_TILE = 1024
MAX_COL_TILE = 1024
MAX_FFN_TILE = 512
MOD_ROWS = 8
NEG_BIG = -1e30
LOG2E = 1.4426950408889634
ATTN_ROW_GROUP = 256
MLA_QK_PAD = 2 * LANES

MIX_CQ, MIX_SK, MIX_SQ, MIX_CKV, MIX_SV, MIX_KR, MIX_ZS, MIX_WIDTH = 0, 768, 1024, 2048, 2560, 2816, 2944, 3072


def _params(*sem):
    return pltpu.CompilerParams(dimension_semantics=sem, vmem_limit_bytes=VMEM_LIMIT_BYTES)


def _pow2_tile(limit, *dims):
    t = 1
    while t * 2 <= limit and all(d % (t * 2) == 0 for d in dims):
        t *= 2
    return t


def _rms_rows(x, g):
    return x * lax.rsqrt(jnp.mean(x * x, axis=-1, keepdims=True) + EPS) * g


def _modulate_kernel(x_ref, g_ref, mod_ref, *rest, shift_row, scale_row, with_router):
    mod = mod_ref[0]
    h = _rms_rows(x_ref[...], g_ref[...]) * (1.0 + mod[scale_row:scale_row + 1]) + mod[shift_row:shift_row + 1]
    if with_router:
        rw_ref, rb_ref, h_ref, lg_ref = rest
        lg_ref[...] = jnp.dot(h.astype(BF16), rw_ref[...].astype(BF16), preferred_element_type=F32) + rb_ref[...]
    else:
        (h_ref,) = rest
    h_ref[...] = h.astype(h_ref.dtype)


def modulate(x, gain, modt, rows, gran, shift_row, scale_row, router=None, tm=None):
    D = x.shape[1]
    tm = gran if tm is None else tm
    step = tm // gran
    kern = functools.partial(_modulate_kernel, shift_row=shift_row, scale_row=scale_row,
                             with_router=router is not None)
    in_specs = [pl.BlockSpec((tm, D), lambda i: (i, 0)),
                pl.BlockSpec((1, D), lambda i: (0, 0)),
                pl.BlockSpec((1, MOD_ROWS, D), lambda i: (i * step, 0, 0))]
    args = [x, gain.reshape(1, D), modt]
    out_shape = [jax.ShapeDtypeStruct((rows, D), BF16)]
    out_specs = [pl.BlockSpec((tm, D), lambda i: (i, 0))]
    if router is not None:
        rw, rb = router
        in_specs += [pl.BlockSpec((D, LANES), lambda i: (0, 0)), pl.BlockSpec((1, LANES), lambda i: (0, 0))]
        args += [rw, rb]
        out_shape.append(jax.ShapeDtypeStruct((rows, LANES), F32))
        out_specs.append(pl.BlockSpec((tm, LANES), lambda i: (i, 0)))
    out = pl.pallas_call(kern, grid=(rows // tm,), in_specs=in_specs, out_specs=out_specs,
                         out_shape=out_shape, compiler_params=_params("parallel"))(*args)
    return out if router is not None else out[0]


def _mm_kernel(a_ref, w_ref, *rest, epilogue, gate_row):
    acc = jnp.dot(a_ref[...].astype(BF16), w_ref[...].astype(BF16), preferred_element_type=F32)
    if epilogue == "bias":
        b_ref, o_ref = rest
        acc = acc + b_ref[...]
    elif epilogue == "sigmoid":
        (o_ref,) = rest
        acc = jax.nn.sigmoid(acc)
    elif epilogue == "residual":
        x_ref, mod_ref, o_ref = rest
        acc = x_ref[...] + mod_ref[0][gate_row:gate_row + 1] * acc
    else:
        (o_ref,) = rest
    o_ref[...] = acc.astype(o_ref.dtype)


def matmul(a, w, *, rows=None, w_lead=None, col0=0, ncols=None, tm, tn, out_dtype, epilogue=None,
           bias=None, resid=None, modt=None, gran=None, gate_row=0):
    rows = a.shape[0] if rows is None else rows
    K = a.shape[1]
    ncols = w.shape[-1] if ncols is None else ncols
    assert rows % tm == 0 and ncols % tn == 0 and col0 % tn == 0
    cb = col0 // tn
    if w.ndim == 3:
        w_spec = pl.BlockSpec((None, K, tn), lambda i, j: (w_lead, 0, cb + j))
    else:
        w_spec = pl.BlockSpec((K, tn), lambda i, j: (0, cb + j))
    in_specs = [pl.BlockSpec((tm, K), lambda i, j: (i, 0)), w_spec]
    args = [a, w]
    if epilogue == "bias":
        in_specs.append(pl.BlockSpec((1, tn), lambda i, j: (0, j)))
        args.append(bias)
    elif epilogue == "residual":
        step = tm // gran
        in_specs += [pl.BlockSpec((tm, tn), lambda i, j: (i, j)),
                     pl.BlockSpec((1, MOD_ROWS, tn), lambda i, j: (i * step, 0, j))]
        args += [resid, modt]
    kern = functools.partial(_mm_kernel, epilogue=epilogue, gate_row=gate_row)
    return pl.pallas_call(kern, grid=(rows // tm, ncols // tn), in_specs=in_specs,
                          out_specs=pl.BlockSpec((tm, tn), lambda i, j: (i, j)),
                          out_shape=jax.ShapeDtypeStruct((rows, ncols), out_dtype),
                          compiler_params=_params("parallel", "arbitrary"))(*args)


def _merge_kernel(h_ref, of_ref, ob2_ref, og_ref, gn_ref, ob_ref, oc_ref, wga_ref, wgb_ref, wgc_ref,
                  wa_ref, wb_ref, wc_ref, o_ref, oa_scr):
    @pl.when(pl.program_id(1) == 0)
    def _():
        for h in range(GDN_HEADS):
            cs = slice(h * GDN_DV, (h + 1) * GDN_DV)
            o = _rms_rows(of_ref[:, cs] + ob2_ref[:, cs], gn_ref[...])
            oa_scr[:, cs] = (o * jax.nn.silu(og_ref[:, cs])).astype(BF16)

    hm = h_ref[...]
    gate = lambda w_ref: jax.nn.sigmoid(jnp.dot(hm, w_ref[...], preferred_element_type=F32))
    acc = gate(wga_ref) * jnp.dot(oa_scr[...], wa_ref[...], preferred_element_type=F32)
    acc += gate(wgb_ref) * jnp.dot(ob_ref[...], wb_ref[...], preferred_element_type=F32)
    acc += gate(wgc_ref) * jnp.dot(oc_ref[...], wc_ref[...], preferred_element_type=F32)
    o_ref[...] = acc.astype(o_ref.dtype)


def merge_branches(h, w_gates, o_f, o_b2, zg, og_col, gdn_g, ob, oc, wa, wb, wc, l, rows, tm, tn):
    D = wa.shape[-1]
    nj = D // tn
    na = GDN_HEADS * GDN_DV
    row = lambda w, cb=0: pl.BlockSpec((tm, w), lambda i, j: (i, cb))
    g_spec = lambda k: pl.BlockSpec((D, tn), lambda i, j: (0, k * nj + j))
    wa, wb, wc = (w[l].astype(BF16) for w in (wa, wb, wc))
    w_gates = w_gates.astype(BF16)
    w_spec = lambda w: pl.BlockSpec((w.shape[0], tn), lambda i, j: (0, j))
    return pl.pallas_call(
        _merge_kernel, grid=(rows // tm, nj),
        in_specs=[row(D), row(na), row(na), row(na, og_col // na), pl.BlockSpec((1, GDN_DV), lambda i, j: (0, 0)),
                  row(ob.shape[1]), row(oc.shape[1]), g_spec(0), g_spec(1), g_spec(2),
                  w_spec(wa), w_spec(wb), w_spec(wc)],
        out_specs=pl.BlockSpec((tm, tn), lambda i, j: (i, j)),
        out_shape=jax.ShapeDtypeStruct((rows, D), BF16),
        scratch_shapes=[pltpu.VMEM((tm, na), BF16)],
        compiler_params=_params("parallel", "arbitrary"))(
            h, o_f, o_b2, zg, gdn_g.reshape(1, GDN_DV).astype(F32), ob, oc, w_gates, w_gates, w_gates, wa, wb, wc)


def _swiglu_kernel(be_ref, bv_ref, x_ref, wg_ref, wu_ref, wd_ref, rs_ref, o_ref):
    i, f = pl.program_id(0), pl.program_id(1)

    @pl.when(f == 0)
    def _():
        o_ref[...] = jnp.zeros_like(o_ref)

    def accumulate(n):
        x = x_ref[:n, :]
        g = jnp.dot(x, wg_ref[...].astype(BF16), preferred_element_type=F32)
        u = jnp.dot(x, wu_ref[...].astype(BF16), preferred_element_type=F32)
        h = (jax.nn.silu(g) * u).astype(BF16)
        o_ref[:n, :] += jnp.dot(h, wd_ref[...].astype(BF16), preferred_element_type=F32)

        @pl.when(f == pl.num_programs(1) - 1)
        def _():
            o_ref[:n, :] = o_ref[:n, :] * rs_ref[:n, :]

    tm = x_ref.shape[0]
    pl.when(bv_ref[i] == 2)(functools.partial(accumulate, tm))
    pl.when(bv_ref[i] == 1)(functools.partial(accumulate, tm // 2))


def swiglu_grouped(x, wg, wu, wd, l, block_e, block_valid, row_scale, rows, tm, tf):
    D = x.shape[1]
    F = wg.shape[-1]
    nf = F // tf
    assert rows % tm == 0 and F % tf == 0

    def f_idx(i, f, bv):
        return jnp.where(bv[i] > 0, f, nf - 1)

    grid_spec = pltpu.PrefetchScalarGridSpec(
        num_scalar_prefetch=2, grid=(rows // tm, nf),
        in_specs=[pl.BlockSpec((tm, D), lambda i, f, be, bv: (i, 0)),
                  pl.BlockSpec((None, None, D, tf), lambda i, f, be, bv: (l, be[i], 0, f_idx(i, f, bv))),
                  pl.BlockSpec((None, None, D, tf), lambda i, f, be, bv: (l, be[i], 0, f_idx(i, f, bv))),
                  pl.BlockSpec((None, None, tf, D), lambda i, f, be, bv: (l, be[i], f_idx(i, f, bv), 0)),
                  pl.BlockSpec((tm, 1), lambda i, f, be, bv: (i, 0))],
        out_specs=pl.BlockSpec((tm, D), lambda i, f, be, bv: (i, 0)))
    return pl.pallas_call(_swiglu_kernel, grid_spec=grid_spec,
                          out_shape=jax.ShapeDtypeStruct((rows, D), F32),
                          compiler_params=_params("parallel", "arbitrary"))(
                              block_e, block_valid, x, wg, wu, wd, row_scale)


def _rope_tables(T, extra, rot_dim):
    n = rot_dim // 4
    t = jnp.arange(T)
    inv = jnp.power(ROPE_BASE, -jnp.arange(n, dtype=F32) / n)
    ar = (t // GRID_W).astype(F32)[:, None] * inv
    ac = (t % GRID_W).astype(F32)[:, None] * inv
    z = jnp.zeros((T, n), F32)
    pad = lambda a, fill: jnp.concatenate([a, jnp.full((T, LANES - 4 * n), fill, F32)], axis=1)
    c = pad(jnp.concatenate([jnp.cos(ar), jnp.cos(ar), jnp.cos(ac), jnp.cos(ac)], axis=1), 1.0)
    a = pad(jnp.concatenate([-jnp.sin(ar), z, -jnp.sin(ac), z], axis=1), 0.0)
    b = pad(jnp.concatenate([z, jnp.sin(ar), z, jnp.sin(ac)], axis=1), 0.0)
    tab = jnp.concatenate([c, a, b], axis=1)
    ident = jnp.concatenate([jnp.ones((extra, LANES), F32), jnp.zeros((extra, 2 * LANES), F32)], axis=1)
    return jnp.concatenate([tab, ident], axis=0)


def _apply_rope(x, tab, half):
    return (x * tab[:, :LANES] + pltpu.roll(x, LANES - half, 1) * tab[:, LANES:2 * LANES]
            + pltpu.roll(x, half, 1) * tab[:, 2 * LANES:])


def _mla_q_kernel(c_ref, g_ref, w_ref, hg_ref, tab_ref, q_ref):
    a = _rms_rows(c_ref[...], g_ref[...]).astype(BF16)
    tab, hg = tab_ref[...], hg_ref[...]
    for h in range(MLA_HEADS):
        acc = jnp.dot(a, w_ref[:, h * MLA_QK_PAD:(h + 1) * MLA_QK_PAD], preferred_element_type=F32)
        y = acc * lax.rsqrt(jnp.sum(acc * acc, axis=-1, keepdims=True) * (1.0 / MLA_QK) + EPS) * hg
        q_ref[h, :, :LANES] = y[:, :LANES].astype(q_ref.dtype)
        q_ref[h, :, LANES:] = _apply_rope(y[:, LANES:], tab, MLA_ROPE // 4).astype(q_ref.dtype)


def _mla_kv_kernel(c_ref, kr_ref, g_ref, w_ref, hg_ref, tab_ref, k_ref, v_ref):
    a = _rms_rows(c_ref[...], g_ref[...]).astype(BF16)
    tab, hg = tab_ref[...], hg_ref[...]
    kr = kr_ref[...]
    kr_ss = jnp.sum(kr * kr, axis=-1, keepdims=True)
    width = MLA_NOPE + MLA_V
    for h in range(MLA_HEADS):
        acc = jnp.dot(a, w_ref[:, h * width:(h + 1) * width], preferred_element_type=F32)
        kn = acc[:, :LANES]
        r = lax.rsqrt((jnp.sum(kn * kn, axis=-1, keepdims=True) + kr_ss) * (1.0 / MLA_QK) + EPS)
        k_ref[h, :, :LANES] = (kn * r * hg[:, :LANES]).astype(k_ref.dtype)
        k_ref[h, :, LANES:] = _apply_rope(kr * r * hg[:, LANES:], tab, MLA_ROPE // 4).astype(k_ref.dtype)
        v_ref[h] = acc[:, LANES:].astype(v_ref.dtype)


def _seq_pos(i, tm, T, Lc, N):
    nl, nc, nlt = T // tm, Lc // tm, N // tm
    k = i - nlt
    return jnp.where(i < nlt, i // nl, k // nc), jnp.where(i < nlt, i % nl, nl + k % nc)


def mla_project_q(zm, rows, q_norm_g, w_uq_l, qn_g, tab, B, T, Lc, tm):
    N = B * T
    Lq = T + (Lc if rows > N else 0)
    H, R = MLA_HEADS, MLA_Q_RANK
    wp = jnp.pad(w_uq_l.reshape(R, H, MLA_QK), ((0, 0), (0, 0), (0, MLA_QK_PAD - MLA_QK)))
    wp = wp.reshape(R, H * MLA_QK_PAD).astype(BF16)
    hg = jnp.pad(qn_g.astype(F32) * (MLA_SCALE * LOG2E), (0, MLA_QK_PAD - MLA_QK))[None, :]
    pos = lambda i: _seq_pos(i, tm, T, Lc, N)
    return pl.pallas_call(
        _mla_q_kernel, grid=(rows // tm,),
        in_specs=[pl.BlockSpec((tm, R), lambda i: (i, MIX_CQ // R)),
                  pl.BlockSpec((1, R), lambda i: (0, 0)),
                  pl.BlockSpec((R, H * MLA_QK_PAD), lambda i: (0, 0)),
                  pl.BlockSpec((1, MLA_QK_PAD), lambda i: (0, 0)),
                  pl.BlockSpec((tm, 3 * LANES), lambda i: (pos(i)[1], 0))],
        out_specs=pl.BlockSpec((None, H, tm, MLA_QK_PAD), lambda i: (pos(i)[0], 0, pos(i)[1], 0)),
        out_shape=jax.ShapeDtypeStruct((B, H, Lq, MLA_QK_PAD), BF16),
        compiler_params=_params("parallel"))(zm, q_norm_g.reshape(1, R).astype(F32), wp, hg, tab)


def mla_project_kv(zm, kv_norm_g, w_ukv_l, kn_g, tab, B, T, Lc, tm):
    M = zm.shape[0]
    N = B * T
    H, R = MLA_HEADS, MLA_KV_RANK
    hg = jnp.pad(kn_g.astype(F32), (0, MLA_QK_PAD - MLA_QK))[None, :]
    pos = lambda i: _seq_pos(i, tm, T, Lc, N)
    o_spec = lambda w: pl.BlockSpec((None, H, tm, w), lambda i: (pos(i)[0], 0, pos(i)[1], 0))
    return pl.pallas_call(
        _mla_kv_kernel, grid=(M // tm,),
        in_specs=[pl.BlockSpec((tm, R), lambda i: (i, MIX_CKV // R)),
                  pl.BlockSpec((tm, LANES), lambda i: (i, MIX_KR // LANES)),
                  pl.BlockSpec((1, R), lambda i: (0, 0)),
                  pl.BlockSpec((R, H * (MLA_NOPE + MLA_V)), lambda i: (0, 0)),
                  pl.BlockSpec((1, MLA_QK_PAD), lambda i: (0, 0)),
                  pl.BlockSpec((tm, 3 * LANES), lambda i: (pos(i)[1], 0))],
        out_specs=[o_spec(MLA_QK_PAD), o_spec(MLA_V)],
        out_shape=[jax.ShapeDtypeStruct((B, H, T + Lc, MLA_QK_PAD), BF16),
                   jax.ShapeDtypeStruct((B, H, T + Lc, MLA_V), BF16)],
        compiler_params=_params("parallel"))(zm, zm, kv_norm_g.reshape(1, R).astype(F32), w_ukv_l.astype(BF16), hg, tab)


def _attn_kernel(q_ref, k_ref, v_ref, o_ref, *, sub):
    for r0 in range(0, q_ref.shape[0], sub):
        rows = slice(r0, r0 + sub)
        s = lax.dot_general(q_ref[rows, :], k_ref[...], (((1,), (1,)), ((), ())), preferred_element_type=F32)
        p = jnp.exp2(s - jnp.max(s, axis=-1, keepdims=True))
        den = jnp.sum(p, axis=-1, keepdims=True)
        o = jnp.dot(p.astype(BF16), v_ref[...], preferred_element_type=F32)
        o_ref[rows, :] = (o * (1.0 / den)).astype(o_ref.dtype)


def full_attention(q, k, v, tq, n_q, q_off, kl, k_blk):
    B, H, _, d = q.shape
    e = v.shape[3]
    nt = n_q // tq
    qo = q_off // tq
    return pl.pallas_call(
        functools.partial(_attn_kernel, sub=min(tq, ATTN_ROW_GROUP)), grid=(B, H, nt),
        in_specs=[pl.BlockSpec((None, None, tq, d), lambda b, h, i: (b, h, qo + i, 0)),
                  pl.BlockSpec((None, None, kl, d), lambda b, h, i: (b, h, k_blk, 0)),
                  pl.BlockSpec((None, None, kl, e), lambda b, h, i: (b, h, k_blk, 0))],
        out_specs=pl.BlockSpec((tq, e), lambda b, h, i: (b * nt + i, h)),
        out_shape=jax.ShapeDtypeStruct((B * n_q, H * e), BF16),
        compiler_params=_params("parallel", "parallel", "arbitrary"))(q, k, v)


def _swa_kernel(*refs, local, n_blocks):
    if local:
        (q_ref, kp_ref, kc_ref, kn_ref, vp_ref, vc_ref, vn_ref, kx_ref, vx_ref, tp_ref, tc_ref, tn_ref,
         qg_ref, kg_ref, sink_ref, o_ref) = refs
    else:
        q_ref, kx_ref, vx_ref, qg_ref, kg_ref, sink_ref, o_ref = refs
    n = pl.program_id(1)
    Bk, d = SWA_BLOCK, SWA_HD
    Lc = kx_ref.shape[0]
    R = SWA_HEADS // SWA_KV_HEADS
    half = SWA_HD // 4

    def prep(x, g, tab):
        y = _rms_rows(x, g)
        return y if tab is None else _apply_rope(y, tab, half)

    if local:
        iq = lax.broadcasted_iota(jnp.int32, (Bk, 3 * Bk), 0)
        jk = lax.broadcasted_iota(jnp.int32, (Bk, 3 * Bk), 1)
        valid = jnp.abs(iq + Bk - jk) <= SWA_WINDOW
        valid = valid & ((jk >= Bk) | (n > 0)) & ((jk < 2 * Bk) | (n < n_blocks - 1))
        bias = jnp.where(valid, 0.0, NEG_BIG).astype(F32)
        bias = jnp.concatenate([bias, jnp.zeros((Bk, Lc), F32)], axis=1)
        tp, tc, tn = tp_ref[...], tc_ref[...], tn_ref[...]
    else:
        tc = None
    kg, qg = kg_ref[...], qg_ref[...]
    kcat, vcat = [], []
    for g in range(SWA_KV_HEADS):
        cs = slice(g * d, (g + 1) * d)
        kx = prep(kx_ref[:, cs], kg, None).astype(BF16)
        if local:
            kcat.append(jnp.concatenate([prep(kp_ref[:, cs], kg, tp).astype(BF16), prep(kc_ref[:, cs], kg, tc).astype(BF16),
                                         prep(kn_ref[:, cs], kg, tn).astype(BF16), kx], axis=0))
            vcat.append(jnp.concatenate([vp_ref[:, cs].astype(BF16), vc_ref[:, cs].astype(BF16),
                                         vn_ref[:, cs].astype(BF16), vx_ref[:, cs].astype(BF16)], axis=0))
        else:
            kcat.append(kx)
            vcat.append(vx_ref[:, cs].astype(BF16))
    hs = range(SWA_HEADS)
    qh = [prep(q_ref[:, h * d:(h + 1) * d], qg, tc).astype(BF16) for h in hs]
    s = [lax.dot_general(qh[h], kcat[h // R], (((1,), (1,)), ((), ())), preferred_element_type=F32) for h in hs]
    if local:
        s = [sh + bias for sh in s]
    sink = [sink_ref[h:h + 1, 0:1] for h in hs]
    m = [jnp.maximum(jnp.max(s[h], axis=-1, keepdims=True), sink[h]) for h in hs]
    p = [jnp.exp2(s[h] - m[h]).astype(BF16) for h in hs]
    vone = [jnp.concatenate([v, jnp.ones_like(v)], axis=1) for v in vcat]
    for h in hs:
        o = jnp.dot(p[h], vone[h // R], preferred_element_type=F32)
        den = o[:, d:] + jnp.exp2(sink[h] - m[h])
        o_ref[:, h * d:(h + 1) * d] = (o[:, :d] * (1.0 / den)).astype(o_ref.dtype)


def window_attention(zm, tab, qn_g, kn_g, sink, B, T, Lc, local):
    N = B * T
    Bk = SWA_BLOCK
    Q, KV = SWA_HEADS * SWA_HD, SWA_KV_HEADS * SWA_HD
    n_q = T if local else Lc
    nb = n_q // Bk
    row0 = 0 if local else N // Bk
    sink_b = jnp.broadcast_to(sink.astype(F32)[:, None], (SWA_HEADS, LANES))
    q_spec = pl.BlockSpec((Bk, Q), lambda b, n: (row0 + b * nb + n, MIX_SQ // Q))
    kx_spec = pl.BlockSpec((Lc, KV), lambda b, n: (N // Lc + b, MIX_SK // KV))
    vx_spec = pl.BlockSpec((Lc, KV), lambda b, n: (N // Lc + b, MIX_SV // KV))
    g_spec = pl.BlockSpec((1, SWA_HD), lambda b, n: (0, 0))
    s_spec = pl.BlockSpec((SWA_HEADS, LANES), lambda b, n: (0, 0))
    gains = (qn_g.reshape(1, SWA_HD).astype(F32) * (SWA_SCALE * LOG2E), kn_g.reshape(1, SWA_HD).astype(F32),
             sink_b * LOG2E)
    if local:
        pv = lambda n: jnp.maximum(n - 1, 0)
        nx = lambda n: jnp.minimum(n + 1, nb - 1)
        kv_spec = lambda f, col: pl.BlockSpec((Bk, KV), lambda b, n: (b * nb + f(n), col // KV))
        t_spec = lambda f: pl.BlockSpec((Bk, 3 * LANES), lambda b, n: (f(n), 0))
        same = lambda n: n
        in_specs = [q_spec, kv_spec(pv, MIX_SK), kv_spec(same, MIX_SK), kv_spec(nx, MIX_SK),
                    kv_spec(pv, MIX_SV), kv_spec(same, MIX_SV), kv_spec(nx, MIX_SV), kx_spec, vx_spec,
                    t_spec(pv), t_spec(same), t_spec(nx), g_spec, g_spec, s_spec]
        args = (zm,) * 9 + (tab,) * 3 + gains
    else:
        in_specs = [q_spec, kx_spec, vx_spec, g_spec, g_spec, s_spec]
        args = (zm,) * 3 + gains
    kern = functools.partial(_swa_kernel, local=local, n_blocks=nb)
    return pl.pallas_call(kern, grid=(B, nb), in_specs=in_specs,
                          out_specs=pl.BlockSpec((Bk, Q), lambda b, n: (b * nb + n, 0)),
                          out_shape=jax.ShapeDtypeStruct((B * n_q, Q), BF16),
                          compiler_params=_params("parallel", "arbitrary"))(*args)


def _split3(x):
    hi = x.astype(BF16)
    r1 = x - hi.astype(F32)
    mid = r1.astype(BF16)
    lo = (r1 - mid.astype(F32)).astype(BF16)
    return hi, mid, lo


def _dot_bf16(a, b):
    return jnp.dot(a.astype(BF16), b.astype(BF16), preferred_element_type=F32)


def _dot_nt(a, b):
    return lax.dot_general(a.astype(BF16), b.astype(BF16), (((1,), (1,)), ((), ())), preferred_element_type=F32)


def _gdn_prep_kernel(zp_ref, zc_ref, zn_ref, zs_ref, cw_ref, ad_ref, wq_ref, u_ref, qk_ref, kd_ref, tot_ref,
                     ext_scr, qkv_scr, *, n_ctx, n_chunks):
    C, H, dk, dv = GDN_CHUNK, GDN_HEADS, GDN_DK, GDN_DV
    P = 2 * C
    nk = H * dk
    halo = 8
    pad = (GDN_CONV - 1) // 2

    ch = pl.program_id(1)
    has_prev = jnp.where((ch == 0) | (ch == n_ctx), 0.0, 1.0)
    has_next = jnp.where((ch == n_ctx - 1) | (ch == n_chunks - 1), 0.0, 1.0)
    ext_scr[0:halo, :] = zp_ref[C - halo:C, :] * has_prev
    ext_scr[halo:halo + C, :] = zc_ref[...]
    ext_scr[halo + C:, :] = zn_ref[0:halo, :] * has_next
    for part in range(3):
        cs = slice(part * nk, (part + 1) * nk)
        y = sum(ext_scr[halo - pad + j:halo - pad + j + C, cs] * cw_ref[j:j + 1, cs] for j in range(GDN_CONV))
        qkv_scr[:, cs] = y * jax.nn.sigmoid(y)
    for h in range(H):
        for part, scale in ((0, dk ** -0.5), (1, 1.0)):
            cs = slice(part * nk + h * dk, part * nk + (h + 1) * dk)
            y = qkv_scr[:, cs]
            qkv_scr[:, cs] = y * (lax.rsqrt(jnp.sum(y * y, axis=-1, keepdims=True) + EPS) * scale)
    q_ref, k_ref, v_ref = qkv_scr.at[:, 0:nk], qkv_scr.at[:, nk:2 * nk], qkv_scr.at[:, 2 * nk:3 * nk]

    zs = zs_ref[...]
    beta = jax.nn.sigmoid(zs)
    a = pltpu.roll(zs, LANES - 2 * H, 1) + ad_ref[1:2, :]
    g = -ad_ref[0:1, :] * (jnp.maximum(a, 0.0) + jnp.log(1.0 + jnp.exp(-jnp.abs(a))))

    ii = lax.broadcasted_iota(jnp.int32, (C, C), 0)
    jj = lax.broadcasted_iota(jnp.int32, (C, C), 1)
    low = (ii >= jj).astype(BF16)
    upp = (ii <= jj).astype(BF16)
    r = lax.broadcasted_iota(jnp.int32, (P, P), 0)
    c = lax.broadcasted_iota(jnp.int32, (P, P), 1)
    rq = jnp.where(r < C, 0, 1)
    cq = jnp.where(c < C, 0, 1)
    ahead = (r - c) * (1 - 2 * rq)
    causal = (rq == cq) & (ahead >= 0)
    strict = (rq == cq) & (ahead > 0)
    eye = (r == c).astype(F32)
    row_fwd = lax.broadcasted_iota(jnp.int32, (P, 1), 0) < C

    g3 = _split3(g)
    tri = jnp.concatenate([low, upp], axis=0)
    dcol_all = sum(jnp.dot(tri, p, preferred_element_type=F32) for p in g3)
    tdot = lambda p, t: lax.dot_general(p, t, (((0,), (0,)), ((), ())), preferred_element_type=F32)
    drow_all = jnp.concatenate([sum(tdot(p, upp) for p in g3)[:H], sum(tdot(p, low) for p in g3)[H:2 * H]],
                               axis=1)
    beta2 = jnp.concatenate([beta, beta], axis=0)

    def pair_col(a, h):
        return jnp.where(row_fwd, a[:, h:h + 1], a[:, H + h:H + h + 1])

    stack = lambda ref, h, w: jnp.concatenate([ref[:, h * w:(h + 1) * w]] * 2, axis=0)

    group = 8
    for h0 in range(0, H, group):
        hs = range(h0, h0 + group)
        dc = [pair_col(dcol_all, h) for h in hs]
        seg = [jnp.exp(jnp.where(causal, dc[a] - drow_all[h:h + 1, :], NEG_BIG)) for a, h in enumerate(hs)]
        b2 = [pair_col(beta2, h) for h in hs]
        kk = [_dot_nt(stack(k_ref, h, dk) * b2[a], stack(k_ref, h, dk)) for a, h in enumerate(hs)]
        pw = [jnp.where(strict, kk[a] * seg[a], 0.0) for a in range(group)]
        inv = [eye - m for m in pw]
        k = 2
        while k < C:
            pw = [_dot_bf16(m, m) for m in pw]
            inv = [t + _dot_bf16(t, m) for t, m in zip(inv, pw)]
            k *= 2
        for a, h in enumerate(hs):
            k2, q2, v2 = stack(k_ref, h, dk), stack(q_ref, h, dk), stack(v_ref, h, dv)
            ecol = jnp.exp(dc[a])
            rhs = jnp.concatenate([k2 * (b2[a] * ecol), v2 * b2[a]], axis=1)
            sol = rhs + _dot_bf16(inv[a] - eye, rhs)
            qk = _dot_nt(q2, k2) * seg[a]
            dlast = jnp.where(row_fwd, dc[a][C - 1:C], dc[a][C:C + 1])
            qd = q2 * ecol
            kd = k2 * jnp.exp(dlast - dc[a])
            tot = jnp.exp(dlast)
            for d, rs in enumerate((slice(0, C), slice(C, P))):
                hd = d * H + h
                wq_ref[hd, :C, :] = sol[rs, :dk].astype(wq_ref.dtype)
                wq_ref[hd, C:, :] = qd[rs].astype(wq_ref.dtype)
                u_ref[hd] = sol[rs, dk:]
                qk_ref[hd] = qk[rs, d * C:(d + 1) * C].astype(qk_ref.dtype)
                kd_ref[hd] = kd[rs].astype(kd_ref.dtype)
                tot_ref[hd] = jnp.broadcast_to(tot[d * C:d * C + 1], (1, LANES))


def gdn_prepare_chunks(zg, zs, conv_w, a_log, dt_bias, B, T, Lc):
    C, H = GDN_CHUNK, GDN_HEADS
    N = B * T
    n_ctx, n = Lc // C, (T + Lc) // C
    width = 3 * H * GDN_DK
    last = zg.shape[0] // C - 1

    def row_block(b, c):
        return jnp.where(c < n_ctx, (N + b * Lc) // C + c, (b * T) // C + c - n_ctx)

    z_spec = lambda d: pl.BlockSpec((C, width), lambda b, c: (jnp.clip(row_block(b, c) + d, 0, last), 0))
    lanes = lambda a: jnp.pad(a.astype(F32), (0, LANES - 2 * H))
    ad = jnp.zeros((8, LANES), F32).at[0].set(lanes(jnp.exp(a_log.astype(F32)))).at[1].set(lanes(dt_bias))
    per = lambda r, w: pl.BlockSpec((None, None, 2 * H, r, w), lambda b, c: (b, c, 0, 0, 0))
    shp = lambda r, w, dt: jax.ShapeDtypeStruct((B, n, 2 * H, r, w), dt)
    return pl.pallas_call(
        functools.partial(_gdn_prep_kernel, n_ctx=n_ctx, n_chunks=n), grid=(B, n),
        in_specs=[z_spec(-1), z_spec(0), z_spec(1),
                  pl.BlockSpec((C, LANES), lambda b, c: (row_block(b, c), MIX_ZS // LANES)),
                  pl.BlockSpec((GDN_CONV, width), lambda b, c: (0, 0)),
                  pl.BlockSpec((8, LANES), lambda b, c: (0, 0))],
        out_specs=[per(2 * C, GDN_DK), per(C, GDN_DV), per(C, C), per(C, GDN_DK), per(1, LANES)],
        out_shape=[shp(2 * C, GDN_DK, BF16), shp(C, GDN_DV, F32), shp(C, C, BF16), shp(C, GDN_DK, BF16),
                   shp(1, LANES, F32)],
        scratch_shapes=[pltpu.VMEM((C + 16, width), F32), pltpu.VMEM((C, width), F32)],
        compiler_params=_params("parallel", "parallel"))(
            zg, zg, zg, zs, conv_w.reshape(GDN_CONV, width).astype(F32), ad)


def _gdn_scan_kernel(wqf, uf, qkf, kdf, totf, wqb, ub, qkb, kdb, totb, of_ref, ob_ref, s_ref):
    C, H, dv = GDN_CHUNK, GDN_HEADS, GDN_DV

    @pl.when(pl.program_id(1) == 0)
    def _():
        s_ref[...] = jnp.zeros_like(s_ref)

    for d, (wq, u, qk, kd, tot, o_ref) in enumerate(((wqf, uf, qkf, kdf, totf, of_ref),
                                                     (wqb, ub, qkb, kdb, totb, ob_ref))):
        hs = range(H)
        ws = [jnp.dot(wq[h], s_ref[d * H + h].astype(BF16), preferred_element_type=F32) for h in hs]
        v_new = [(u[h] - ws[h][:C]).astype(BF16) for h in hs]
        for h in hs:
            o_ref[:, h * dv:(h + 1) * dv] = ws[h][C:] + jnp.dot(qk[h], v_new[h], preferred_element_type=F32)
        upd = [lax.dot_general(kd[h], v_new[h], (((0,), (0,)), ((), ())), preferred_element_type=F32) for h in hs]
        for h in hs:
            s_ref[d * H + h] = s_ref[d * H + h] * tot[h] + upd[h]


def gdn_scan(wq, u, qk, kd, tot, n_ctx, T, Lc):
    B, n = wq.shape[0], wq.shape[1]
    C, H = GDN_CHUNK, GDN_HEADS
    N = B * T

    def bwd(s):
        return jnp.where(s < n_ctx, n_ctx - 1 - s, n - 1 - (s - n_ctx))

    def row_block(b, c):
        return jnp.where(c < n_ctx, (N + b * Lc) // C + c, (b * T) // C + c - n_ctx)

    fw = lambda r, w: pl.BlockSpec((None, None, H, r, w), lambda b, s: (b, s, 0, 0, 0))
    bw = lambda r, w: pl.BlockSpec((None, None, H, r, w), lambda b, s: (b, bwd(s), 1, 0, 0))
    shapes = ((2 * C, GDN_DK), (C, GDN_DV), (C, C), (C, GDN_DK), (1, LANES))
    o_shape = jax.ShapeDtypeStruct((B * n * C, H * GDN_DV), F32)
    return pl.pallas_call(
        _gdn_scan_kernel, grid=(B, n),
        in_specs=[fw(*s) for s in shapes] + [bw(*s) for s in shapes],
        out_specs=[pl.BlockSpec((C, H * GDN_DV), lambda b, s: (row_block(b, s), 0)),
                   pl.BlockSpec((C, H * GDN_DV), lambda b, s: (row_block(b, bwd(s)), 0))],
        out_shape=[o_shape, o_shape],
        scratch_shapes=[pltpu.VMEM((2 * H, GDN_DK, GDN_DV), F32)],
        compiler_params=_params("parallel", "arbitrary"))(wq, u, qk, kd, tot, wq, u, qk, kd, tot)


def kernel(x, c, ctx, c_ctx, w_mod, b_mod, norm_mix_g, norm_ffn_g, w_in, gdn_conv_w, gdn_a_log, gdn_dt_bias,
           gdn_norm_g, mla_q_norm_g, mla_kv_norm_g, mla_w_uq, mla_w_ukv, mla_qn_g, mla_kn_g, swa_qn_g, swa_kn_g,
           swa_sink, w_branch_a, w_branch_b, w_branch_c, w_out, ffn_w_gate, ffn_w_up, ffn_w_down, moe_router,
           moe_router_bias, moe_w_gate, moe_w_up, moe_w_down):
    B, T, D = x.shape
    Lc = ctx.shape[1]
    depth = w_mod.shape[0]
    N, Nc = B * T, B * Lc
    M = N + Nc
    TM = _pow2_tile(MAX_ROW_TILE, T, Nc)
    gran = min(TM, 256)
    tseq = _pow2_tile(256, T, Lc)
    assert T % Lc == 0 and Lc % SWA_BLOCK == 0 and Lc % GDN_CHUNK == 0
    H = GDN_HEADS
    nk = H * GDN_DK

    sizes = (nk, nk, H * GDN_DV, H * GDN_DV, 2 * H, 2 * H, MLA_Q_RANK, MLA_KV_RANK, MLA_ROPE,
             SWA_HEADS * SWA_HD, SWA_KV_HEADS * SWA_HD, SWA_KV_HEADS * SWA_HD, N_BRANCH * D)
    off = np.concatenate([[0], np.cumsum(sizes)])
    n_gdn = int(off[4])
    tn_gdn = _pow2_tile(MAX_COL_TILE, n_gdn)
    tn_mix = _pow2_tile(MAX_COL_TILE, MIX_WIDTH)
    tn_out = _pow2_tile(MAX_COL_TILE, D)
    tn_gate = _pow2_tile(512, D)

    rows_all = jnp.concatenate([x.reshape(N, D), ctx.reshape(Nc, D)], axis=0)
    tile_batch = np.minimum(np.arange(M // gran) * gran // T, B)
    tile_batch = np.where(np.arange(M // gran) * gran < N, tile_batch, B)
    cvec = jnp.concatenate([c, c_ctx[None, :], jnp.zeros((MOD_ROWS - B - 1, D), F32)], axis=0)
    cvec = jax.nn.silu(cvec)
    tab_mla = _rope_tables(T, Lc, MLA_ROPE)
    tab_swa = _rope_tables(T, 0, SWA_HD)
    n_ctx_chunks = Lc // GDN_CHUNK
    S = Lc + T

    for l in range(depth):
        need_ctx = l < depth - 1
        rows_out = M if need_ctx else N

        mod = matmul(cvec, w_mod, w_lead=l, tm=MOD_ROWS, tn=_pow2_tile(1024, 6 * D), out_dtype=F32,
                     epilogue="bias", bias=b_mod[l][None, :])
        mod = mod.reshape(MOD_ROWS, 6, D)[:B + 1]
        mod = jnp.pad(mod, ((0, 0), (0, MOD_ROWS - 6), (0, 0)))
        modt = mod[tile_batch]

        h = modulate(rows_all, norm_mix_g[l], modt, M, gran, shift_row=0, scale_row=1, tm=TM)
        w_l = w_in[l]
        seg = lambda a, b: w_l[:, off[a]:off[b]]
        zcols = lambda n: jnp.zeros((D, n), w_l.dtype)
        w_mix = jnp.concatenate([seg(6, 7), seg(10, 11), seg(9, 10), seg(7, 8), seg(11, 12), seg(8, 9),
                                 zcols(MIX_ZS - MIX_KR - MLA_ROPE), seg(4, 6), zcols(MIX_WIDTH - MIX_ZS - 4 * H)], axis=1)
        zg = matmul(h, w_in, w_lead=l, ncols=n_gdn, tm=TM, tn=tn_gdn, out_dtype=F32)
        zm = matmul(h, w_mix, tm=TM, tn=tn_mix, out_dtype=F32)

        wq_c, u_c, qk_c, kd_c, tot_c = gdn_prepare_chunks(zg, zm, gdn_conv_w[l], gdn_a_log[l], gdn_dt_bias[l], B, T, Lc)
        o_f, o_b2 = gdn_scan(wq_c, u_c, qk_c, kd_c, tot_c, n_ctx_chunks, T, Lc)

        mq = mla_project_q(zm, rows_out, mla_q_norm_g[l], mla_w_uq[l], mla_qn_g[l], tab_mla, B, T, Lc, tseq)
        mk, mv = mla_project_kv(zm, mla_kv_norm_g[l], mla_w_ukv[l], mla_kn_g[l], tab_mla, B, T, Lc, tseq)
        o_b = full_attention(mq, mk, mv, _pow2_tile(MAX_ROW_TILE, T), T, 0, S, 0)
        o_c = window_attention(zm, tab_swa, swa_qn_g[l], swa_kn_g[l], swa_sink[l], B, T, Lc, local=True)
        if need_ctx:
            o_b = jnp.concatenate([o_b, full_attention(mq, mk, mv, _pow2_tile(256, Lc), Lc, T, Lc, T // Lc)], axis=0)
            o_c = jnp.concatenate([o_c, window_attention(zm, None, swa_qn_g[l], swa_kn_g[l], swa_sink[l], B, T, Lc,
                                                         local=False)], axis=0)

        merged = merge_branches(h, seg(12, 13), o_f, o_b2, zg, 3 * nk, gdn_norm_g[l], o_b, o_c, w_branch_a, w_branch_b,
                                w_branch_c, l, rows_out, TM // 2, tn_gate)
        rows_new = matmul(merged, w_out, w_lead=l, tm=TM, tn=tn_out, out_dtype=F32, epilogue="residual",
                          resid=rows_all, modt=modt, gran=gran, gate_row=2)

        i = l // 2
        if l % 2 == 0:
            h2 = modulate(rows_new, norm_ffn_g[l], modt, rows_out, gran, shift_row=3, scale_row=4, tm=TM)
            nblk = rows_out // TM
            F = ffn_w_gate.shape[-1]
            y = swiglu_grouped(h2, ffn_w_gate[:, None], ffn_w_up[:, None], ffn_w_down[:, None], i,
                               jnp.zeros((nblk,), jnp.int32), jnp.full((nblk,), 2, jnp.int32),
                               jnp.ones((rows_out, 1), F32), rows_out, TM, _pow2_tile(MAX_FFN_TILE, F))
        else:
            E = moe_w_gate.shape[1]
            rw = jnp.pad(moe_router[i], ((0, 0), (0, LANES - E)))
            rb = jnp.pad(moe_router_bias[i].astype(F32), (0, LANES - E))[None, :]
            h2, logits = modulate(rows_new, norm_ffn_g[l], modt, rows_out, gran, shift_row=3, scale_row=4,
                                  router=(rw, rb))
            top_logit, top_idx = lax.top_k(logits[:, :E], TOP_K)
            top_w = jax.nn.softmax(top_logit, axis=-1)
            flat_e = top_idx.reshape(-1)
            onehot = (flat_e[:, None] == jnp.arange(E)[None, :]).astype(jnp.int32)
            rank = jnp.take_along_axis(jnp.cumsum(onehot, axis=0) - onehot, flat_e[:, None], axis=1)[:, 0]
            counts = jnp.sum(onehot, axis=0)
            padded = (counts + TM - 1) // TM * TM
            pstart = jnp.cumsum(padded) - padded
            dest = pstart[flat_e] + rank
            nblk = -(-(rows_out * TOP_K) // TM) + E
            slots = nblk * TM
            src = jnp.zeros((slots,), jnp.int32).at[dest].set(jnp.arange(rows_out * TOP_K, dtype=jnp.int32) // TOP_K)
            w_slot = jnp.zeros((slots,), F32).at[dest].set(top_w.reshape(-1))
            blk_start = jnp.arange(nblk, dtype=jnp.int32) * TM
            ends = jnp.cumsum(padded)
            block_e = jnp.minimum(jnp.searchsorted(ends, blk_start, side="right"), E - 1).astype(jnp.int32)
            used = blk_start < ends[-1]
            last_e = block_e[jnp.maximum(jnp.sum(used.astype(jnp.int32)) - 1, 0)]
            block_e = jnp.where(used, block_e, last_e)
            rows_in_block = jnp.clip((pstart + counts)[block_e] - blk_start, 0, TM)
            half = TM // 2
            block_valid = jnp.where(used, (rows_in_block + half - 1) // half, 0).astype(jnp.int32)
            F = moe_w_gate.shape[-1]
            ys = swiglu_grouped(h2[src], moe_w_gate, moe_w_up, moe_w_down, i, block_e, block_valid,
                                w_slot[:, None], slots, TM, _pow2_tile(MAX_FFN_TILE, F))
            dest2 = dest.reshape(rows_out, TOP_K)
            y = ys[dest2[:, 0]] + ys[dest2[:, 1]]
        gate_f = modt[:, 5][:rows_out // gran]
        rows_ffn = (rows_new.reshape(rows_out // gran, gran, D) + gate_f[:, None, :] * y.reshape(rows_out // gran, gran, D))
        rows_all = rows_ffn.reshape(rows_out, D)

    return rows_all[:N].reshape(B, T, D)
```
